```python
import math
import jax, jax.numpy as jnp
from jax import lax
import numpy as np

D_MODEL = 2048
BATCH = 16
SEQ = 256
DEPTH = 2
DEC_BATCH = 2
DEC_SEQ = 2048
PAST_LEN = 256

GRID_W = 64
N_HEADS = 8
N_KV_HEADS = 2
HEAD_DIM = 128
Q_GROUP = N_HEADS // N_KV_HEADS
WINDOW = 128
ATTN_BLOCK = 128
ROPE_BASE = 10000.0
ATT_Q_W = N_HEADS * HEAD_DIM
ATT_KV_W = N_KV_HEADS * HEAD_DIM
HG_HEADS = 8
HG_DK = 128
HG_DV = 128
HG_W = HG_HEADS * HG_DK
HG_CHUNK = 16
S5_W = 1024
S5_CH = 16
S5_GROUPS = S5_W // S5_CH
S5_STATE = 64
S5_DT_MIN = 0.001
S5_DT_MAX = 0.1
N_EXPERTS = 32
TOP_K = 4
D_FF = 2048
SWIGLU_ALPHA = 1.702
SWIGLU_LIMIT = 7.0
N_MOD = 6
EPS = 1e-6
IN_SIZES = (ATT_Q_W, ATT_KV_W, ATT_KV_W, HG_W, HG_W, HG_W, HG_W, HG_W, S5_W, D_MODEL, D_MODEL, D_MODEL)
IN_COLS = sum(IN_SIZES)

kernel_name = 'hybrid_dit_prefix_denoise_step'


def rms_norm(x, g):
    xf = x.astype(jnp.float32)
    y = xf * lax.rsqrt(jnp.mean(xf * xf, axis=-1, keepdims=True) + EPS)
    return (y * g.astype(jnp.float32)).astype(x.dtype)


def modulation(cond, w_mod, b_mod):
    m = jax.nn.silu(cond) @ w_mod + b_mod
    return jnp.split(m[..., None, :], N_MOD, axis=-1)


def split_in(z):
    return jnp.split(z, np.cumsum(IN_SIZES)[:-1].tolist(), axis=-1)


def attn_heads(q, k, v, q_g, k_g):
    b, t = q.shape[:2]
    q = rms_norm(q.reshape(b, t, N_HEADS, HEAD_DIM), q_g)
    k = rms_norm(k.reshape(b, t, N_KV_HEADS, HEAD_DIM), k_g)
    return q, k, v.reshape(b, t, N_KV_HEADS, HEAD_DIM)


def _rotate(x, pos):
    nf = x.shape[-1] // 2
    inv = ROPE_BASE ** (-jnp.arange(nf, dtype=jnp.float32) / nf)
    ang = pos.astype(jnp.float32)[:, None] * inv[None, :]
    cos = jnp.cos(ang)[None, :, None, :]
    sin = jnp.sin(ang)[None, :, None, :]
    xf = x.astype(jnp.float32)
    x1, x2 = xf[..., :nf], xf[..., nf:]
    return jnp.concatenate([x1 * cos - x2 * sin, x1 * sin + x2 * cos], axis=-1).astype(x.dtype)


def axial_rope(x):
    t = x.shape[1]
    rows = t // GRID_W
    row = jnp.repeat(jnp.arange(rows), GRID_W)
    col = jnp.tile(jnp.arange(GRID_W), rows)
    half = HEAD_DIM // 2
    return jnp.concatenate([_rotate(x[..., :half], row), _rotate(x[..., half:], col)], axis=-1)


def _sink_column(sink, lead_shape):
    s = sink.astype(jnp.float32).reshape(N_KV_HEADS, Q_GROUP, 1, 1)
    return jnp.broadcast_to(s, lead_shape + (1,))


def context_attention(q, k, v, sink):
    b, l = q.shape[:2]
    nb = l // ATTN_BLOCK
    scale = HEAD_DIM ** -0.5
    qb = q.reshape(b, nb, ATTN_BLOCK, N_KV_HEADS, Q_GROUP, HEAD_DIM).transpose(1, 0, 2, 3, 4, 5)

    def one_block(qi):
        s = jnp.einsum('bqhgd,bshd->bhgqs', qi, k).astype(jnp.float32) * scale
        s = jnp.concatenate([s, _sink_column(sink, s.shape[:-1])], axis=-1)
        pr = jax.nn.softmax(s, axis=-1)[..., :l].astype(v.dtype)
        return jnp.einsum('bhgqs,bshd->bqhgd', pr, v)

    o = lax.map(one_block, qb)
    return o.transpose(1, 0, 2, 3, 4, 5).reshape(b, l, N_HEADS * HEAD_DIM)


def window_attention(q, k, v, k_ctx, v_ctx, sink):
    b, t = q.shape[:2]
    nb = t // ATTN_BLOCK
    lc = k_ctx.shape[1]
    scale = HEAD_DIM ** -0.5
    qb = q.reshape(b, nb, ATTN_BLOCK, N_KV_HEADS, Q_GROUP, HEAD_DIM)
    pad = ((0, 0), (ATTN_BLOCK, ATTN_BLOCK), (0, 0), (0, 0))

    def bands(a):
        ap = jnp.pad(a, pad).reshape(b, nb + 2, ATTN_BLOCK, N_KV_HEADS, HEAD_DIM)
        return jnp.concatenate([ap[:, :-2], ap[:, 1:-1], ap[:, 2:]], axis=2)

    kb, vb = bands(k), bands(v)
    blk = jnp.arange(nb)[:, None] * ATTN_BLOCK
    qpos = blk + jnp.arange(ATTN_BLOCK)[None, :]
    kpos = blk - ATTN_BLOCK + jnp.arange(3 * ATTN_BLOCK)[None, :]
    rel = kpos[:, None, :] - qpos[:, :, None]
    valid = (jnp.abs(rel) <= WINDOW) & (kpos[:, None, :] >= 0) & (kpos[:, None, :] < t)
    s_loc = jnp.einsum('bnqhgd,bnshd->bnhgqs', qb, kb).astype(jnp.float32) * scale
    s_loc = jnp.where(valid[None, :, None, None], s_loc, -jnp.inf)
    s_ctx = jnp.einsum('bnqhgd,bshd->bnhgqs', qb, k_ctx).astype(jnp.float32) * scale
    s = jnp.concatenate([s_loc, s_ctx, _sink_column(sink, s_loc.shape[:-1])], axis=-1)
    pr = jax.nn.softmax(s, axis=-1).astype(v.dtype)
    w = 3 * ATTN_BLOCK
    o = (jnp.einsum('bnhgqs,bnshd->bnqhgd', pr[..., :w], vb)
         + jnp.einsum('bnhgqs,bshd->bnqhgd', pr[..., w:w + lc], v_ctx))
    return o.reshape(b, t, N_HEADS * HEAD_DIM)


def hgrn2_forget(f_pre, lb):
    x = f_pre.astype(jnp.float32)
    lb = lb.astype(jnp.float32)
    log_f = jnp.logaddexp(jnp.log(lb), jnp.log1p(-lb) + jax.nn.log_sigmoid(x))
    return 1.0 - jnp.exp(log_f), log_f


def gla_chunkwise(q, k, v, log_f, s0):
    b, t, h, dk = q.shape
    dv = v.shape[-1]
    nc = t // HG_CHUNK
    q, k, log_f = (a.reshape(b, nc, HG_CHUNK, h, dk) for a in (q, k, log_f))
    v = v.reshape(b, nc, HG_CHUNK, h, dv)
    cum = jnp.cumsum(log_f, axis=2)
    last = cum[:, :, -1]
    causal = jnp.tril(jnp.ones((HG_CHUNK, HG_CHUNK), dtype=bool))[None, None, :, :, None, None]
    diff = cum[:, :, :, None] - cum[:, :, None, :]
    decay = jnp.exp(jnp.where(causal, diff, -jnp.inf))
    scores = jnp.einsum('bctshd,bcthd,bcshd->bchts', decay, q, k)
    o_intra = jnp.einsum('bchts,bcshe->bcthe', scores, v)
    kv = jnp.einsum('bcshd,bcshe->bchde', k * jnp.exp(last[:, :, None] - cum), v)

    def step(state, inp):
        g_c, kv_c = inp
        return jnp.exp(g_c)[..., None] * state + kv_c, state

    s_final, s_prev = lax.scan(step, s0, (jnp.moveaxis(last, 1, 0), jnp.moveaxis(kv, 1, 0)))
    s_prev = jnp.moveaxis(s_prev, 0, 1)
    o_inter = jnp.einsum('bcthd,bchde->bcthe', q * jnp.exp(cum), s_prev)
    return (o_intra + o_inter).reshape(b, t, h, dv), s_final


def hgrn2_bidir(hq, hff, hfb, hi, hg, lb, norm_g, s0):
    b, t = hq.shape[:2]

    def heads(a):
        return a.astype(jnp.float32).reshape(b, t, HG_HEADS, HG_DK)

    def flip(a):
        return jnp.flip(a, axis=1)

    q = heads(jax.nn.silu(hq.astype(jnp.float32)) * HG_DK ** -0.5)
    v = heads(hi)
    k_f, lf_f = hgrn2_forget(hff, lb[0])
    k_b, lf_b = hgrn2_forget(hfb, lb[1])
    s0 = s0.astype(jnp.float32)
    o_f, s_f = gla_chunkwise(q, heads(k_f), v, heads(lf_f), s0[:, 0])
    o_b, s_b = gla_chunkwise(flip(q), flip(heads(k_b)), flip(v), flip(heads(lf_b)), s0[:, 1])
    o = rms_norm(o_f + flip(o_b), norm_g) * jax.nn.silu(heads(hg))
    return o.reshape(b, t, HG_W).astype(hq.dtype), jnp.stack([s_f, s_b], axis=1)


def _lin_combine(c1, c2):
    a1, b1 = c1
    a2, b2 = c2
    return a1 * a2, a2 * b1 + b2


def s5_bidir(u, p, x0_re, x0_im):
    b, t, _ = u.shape
    f32 = jnp.float32
    uf = u.astype(f32).reshape(b, t, S5_GROUPS, S5_CH)
    b_mat = lax.complex(p['s5_b_re'].astype(f32), p['s5_b_im'].astype(f32))
    c_mat = lax.complex(p['s5_c_re'].astype(f32), p['s5_c_im'].astype(f32))
    lam = lax.complex(p['s5_a_re'].astype(f32), p['s5_a_im'].astype(f32))
    dt = jnp.exp(p['s5_log_dt'].astype(f32))[..., None]
    a_bar = jnp.exp(lam * dt)
    gamma = (a_bar - 1.0) / lam
    x0 = lax.complex(x0_re.astype(f32), x0_im.astype(f32))
    bu = jnp.einsum('gpc,btgc->btgp', b_mat, uf.astype(jnp.complex64))

    def run(d, seq):
        elems_b = (gamma[d] * seq).at[:, 0].add(a_bar[d] * x0[:, d])
        elems_a = jnp.broadcast_to(a_bar[d], elems_b.shape)
        _, xs = lax.associative_scan(_lin_combine, (elems_a, elems_b), axis=1)
        return xs

    xs_f = run(0, bu)
    xs_b = run(1, jnp.flip(bu, axis=1))
    y = jnp.real(jnp.einsum('gcp,btgp->btgc', c_mat, xs_f + jnp.flip(xs_b, axis=1)))
    y = y + p['s5_d'].astype(f32).reshape(S5_GROUPS, S5_CH) * uf
    y = jax.nn.gelu(y.reshape(b, t, S5_W)).astype(u.dtype)
    out = y * jax.nn.sigmoid(y @ p['s5_w_glu'] + p['s5_b_glu'])
    final = jnp.stack([xs_f[:, -1], xs_b[:, -1]], axis=1)
    return out, jnp.real(final), jnp.imag(final)


def merge_branches(a_out, h_out, s_out, ga, gh, gs, p):
    y = (jax.nn.sigmoid(ga) * (a_out @ p['w_br_attn'])
         + jax.nn.sigmoid(gh) * (h_out @ p['w_br_hg'])
         + jax.nn.sigmoid(gs) * (s_out @ p['w_br_s5']))
    return y @ p['w_out']


def moe_ffn(h, p):
    b, t, d = h.shape
    ht = h.reshape(b * t, d)
    logits = (ht @ p['router_w'] + p['router_b']).astype(jnp.float32)
    top_val, top_idx = lax.top_k(logits, TOP_K)
    weights = jax.nn.softmax(top_val, axis=-1)
    gates = jnp.einsum('nk,nke->ne', weights,
                       jax.nn.one_hot(top_idx, N_EXPERTS, dtype=jnp.float32)).astype(h.dtype)
    y = jnp.zeros_like(ht)
    for e in range(N_EXPERTS):
        z = ht @ p['w1'][e] + p['b1'][e]
        g_lin = jnp.minimum(z[:, :D_FF], SWIGLU_LIMIT)
        up = jnp.clip(z[:, D_FF:], -SWIGLU_LIMIT, SWIGLU_LIMIT)
        act = g_lin * jax.nn.sigmoid(SWIGLU_ALPHA * g_lin) * (up + 1.0)
        y = y + gates[:, e:e + 1] * (act @ p['w2'][e] + p['b2'][e])
    return y.reshape(b, t, d)


def ffn_sublayer(x, sh2, sc2, gt2, p):
    h = rms_norm(x, p['norm2_g']) * (1.0 + sc2) + sh2
    return x + gt2 * moe_ffn(h, p)


def context_layer(x, c_ctx, p, lb):
    b = x.shape[0]
    sh1, sc1, gt1, sh2, sc2, gt2 = modulation(c_ctx, p['w_mod'], p['b_mod'])
    h = rms_norm(x, p['norm1_g']) * (1.0 + sc1) + sh1
    q, k, v, hq, hff, hfb, hi, hg, su, ga, gh, gs = split_in(h @ p['w_in'])
    q, k, v = attn_heads(q, k, v, p['q_norm_g'], p['k_norm_g'])
    a_out = context_attention(q, k, v, p['sink'])
    z_hg = jnp.zeros((b, 2, HG_HEADS, HG_DK, HG_DV), jnp.float32)
    h_out, s_hg = hgrn2_bidir(hq, hff, hfb, hi, hg, lb, p['hg_norm_g'], z_hg)
    z_s5 = jnp.zeros((b, 2, S5_GROUPS, S5_STATE), jnp.float32)
    s_out, s5_re, s5_im = s5_bidir(su, p, z_s5, z_s5)
    x = x + gt1 * merge_branches(a_out, h_out, s_out, ga, gh, gs, p)
    x = ffn_sublayer(x, sh2, sc2, gt2, p)
    return x, k, v, s_hg, s5_re, s5_im


def latent_layer(x, c, ck, cv, s_hg0, s5_re0, s5_im0, p, lb):
    sh1, sc1, gt1, sh2, sc2, gt2 = modulation(c, p['w_mod'], p['b_mod'])
    h = rms_norm(x, p['norm1_g']) * (1.0 + sc1) + sh1
    q, k, v, hq, hff, hfb, hi, hg, su, ga, gh, gs = split_in(h @ p['w_in'])
    q, k, v = attn_heads(q, k, v, p['q_norm_g'], p['k_norm_g'])
    a_out = window_attention(axial_rope(q), axial_rope(k), v, ck, cv, p['sink'])
    h_out, _ = hgrn2_bidir(hq, hff, hfb, hi, hg, lb, p['hg_norm_g'], s_hg0)
    s_out, _, _ = s5_bidir(su, p, s5_re0, s5_im0)
    x = x + gt1 * merge_branches(a_out, h_out, s_out, ga, gh, gs, p)
    return ffn_sublayer(x, sh2, sc2, gt2, p)


def setup_inputs(seed: int = 0) -> dict:
    key = jax.random.key(seed)
    ks = iter(jax.random.split(key, 64))
    f32 = jnp.float32

    def nrm(shape, scale):
        return jax.random.normal(next(ks), shape, f32) * scale

    def gain(shape):
        return 1.0 + nrm(shape, 0.01)

    s5_shape = (DEPTH, 2, S5_GROUPS, S5_STATE)
    n = jnp.arange(S5_STATE, dtype=f32)
    return {
        'x_prompt': nrm((BATCH, SEQ, D_MODEL), 1.0),
        'x_sample': nrm((DEC_BATCH, DEC_SEQ, D_MODEL), 1.0),
        'cache_k': nrm((DEC_BATCH, DEPTH, PAST_LEN, N_KV_HEADS, HEAD_DIM), 1.0),
        'cache_v': nrm((DEC_BATCH, DEPTH, PAST_LEN, N_KV_HEADS, HEAD_DIM), 1.0),
        'state_hgrn': nrm((DEC_BATCH, DEPTH, 2, HG_HEADS, HG_DK, HG_DV), 0.3),
        'state_s5_re': nrm((DEC_BATCH, DEPTH, 2, S5_GROUPS, S5_STATE), 0.1),
        'state_s5_im': nrm((DEC_BATCH, DEPTH, 2, S5_GROUPS, S5_STATE), 0.1),
        'c': nrm((DEC_BATCH, D_MODEL), 1.0),
        'c_ctx': nrm((D_MODEL,), 1.0),
        'w_mod': nrm((DEPTH, D_MODEL, N_MOD * D_MODEL), D_MODEL ** -0.5),
        'b_mod': nrm((DEPTH, N_MOD * D_MODEL), 0.02),
        'norm1_g': gain((DEPTH, D_MODEL)),
        'norm2_g': gain((DEPTH, D_MODEL)),
        'w_in': nrm((DEPTH, D_MODEL, IN_COLS), D_MODEL ** -0.5),
        'q_norm_g': gain((DEPTH, HEAD_DIM)),
        'k_norm_g': gain((DEPTH, HEAD_DIM)),
        'attn_sink': nrm((DEPTH, N_HEADS), 0.5),
        'hg_lb_logits': nrm((DEPTH, 2, HG_W), 0.5),
        'hg_norm_g': gain((DEPTH, HG_DV)),
        's5_a_re': -0.5 + nrm(s5_shape, 0.01),
        's5_a_im': jnp.pi * n + nrm(s5_shape, 0.01),
        's5_log_dt': jax.random.uniform(next(ks), (DEPTH, 2, S5_GROUPS), f32,
                                        math.log(S5_DT_MIN), math.log(S5_DT_MAX)),
        's5_b_re': nrm((DEPTH, S5_GROUPS, S5_STATE, S5_CH), (2 * S5_CH) ** -0.5),
        's5_b_im': nrm((DEPTH, S5_GROUPS, S5_STATE, S5_CH), (2 * S5_CH) ** -0.5),
        's5_c_re': nrm((DEPTH, S5_GROUPS, S5_CH, S5_STATE), (2 * S5_STATE) ** -0.5),
        's5_c_im': nrm((DEPTH, S5_GROUPS, S5_CH, S5_STATE), (2 * S5_STATE) ** -0.5),
        's5_d': nrm((DEPTH, S5_W), 0.5),
        's5_w_glu': nrm((DEPTH, S5_W, S5_W), S5_W ** -0.5),
        's5_b_glu': nrm((DEPTH, S5_W), 0.02),
        'w_br_attn': nrm((DEPTH, ATT_Q_W, D_MODEL), ATT_Q_W ** -0.5),
        'w_br_hg': nrm((DEPTH, HG_W, D_MODEL), HG_W ** -0.5),
        'w_br_s5': nrm((DEPTH, S5_W, D_MODEL), S5_W ** -0.5),
        'w_out': nrm((DEPTH, D_MODEL, D_MODEL), D_MODEL ** -0.5),
        'router_w': nrm((DEPTH, D_MODEL, N_EXPERTS), D_MODEL ** -0.5),
        'router_b': nrm((DEPTH, N_EXPERTS), 0.01),
        'exp_w1': nrm((DEPTH, N_EXPERTS, D_MODEL, 2 * D_FF), D_MODEL ** -0.5),
        'exp_b1': nrm((DEPTH, N_EXPERTS, 2 * D_FF), 0.01),
        'exp_w2': nrm((DEPTH, N_EXPERTS, D_FF, D_MODEL), D_FF ** -0.5),
        'exp_b2': nrm((DEPTH, N_EXPERTS, D_MODEL), 0.01),
    }


def reference(x_prompt, x_sample, cache_k, cache_v, state_hgrn, state_s5_re, state_s5_im, c, c_ctx,
              w_mod, b_mod, norm1_g, norm2_g, w_in, q_norm_g, k_norm_g, attn_sink, hg_lb_logits,
              hg_norm_g, s5_a_re, s5_a_im, s5_log_dt, s5_b_re, s5_b_im, s5_c_re, s5_c_im, s5_d,
              s5_w_glu, s5_b_glu, w_br_attn, w_br_hg, w_br_s5, w_out, router_w, router_b,
              exp_w1, exp_b1, exp_w2, exp_b2):
    lb_all = jnp.cumsum(jax.nn.softmax(hg_lb_logits.astype(jnp.float32), axis=0), axis=0)
    lb_all = lb_all - lb_all[:1]
    y_p, y_s = x_prompt, x_sample
    new_k, new_v, new_hg, new_re, new_im = [], [], [], [], []
    for l in range(DEPTH):
        p = {
            'w_mod': w_mod[l], 'b_mod': b_mod[l], 'norm1_g': norm1_g[l], 'norm2_g': norm2_g[l],
            'w_in': w_in[l], 'q_norm_g': q_norm_g[l], 'k_norm_g': k_norm_g[l], 'sink': attn_sink[l],
            'hg_norm_g': hg_norm_g[l], 's5_a_re': s5_a_re[l], 's5_a_im': s5_a_im[l],
            's5_log_dt': s5_log_dt[l], 's5_b_re': s5_b_re[l], 's5_b_im': s5_b_im[l],
            's5_c_re': s5_c_re[l], 's5_c_im': s5_c_im[l], 's5_d': s5_d[l], 's5_w_glu': s5_w_glu[l],
            's5_b_glu': s5_b_glu[l], 'w_br_attn': w_br_attn[l], 'w_br_hg': w_br_hg[l],
            'w_br_s5': w_br_s5[l], 'w_out': w_out[l], 'router_w': router_w[l], 'router_b': router_b[l],
            'w1': exp_w1[l], 'b1': exp_b1[l], 'w2': exp_w2[l], 'b2': exp_b2[l],
        }
        y_p, k_l, v_l, hg_l, re_l, im_l = context_layer(y_p, c_ctx, p, lb_all[l])
        new_k.append(k_l)
        new_v.append(v_l)
        new_hg.append(hg_l)
        new_re.append(re_l)
        new_im.append(im_l)
        y_s = latent_layer(y_s, c, cache_k[:, l], cache_v[:, l], state_hgrn[:, l],
                           state_s5_re[:, l], state_s5_im[:, l], p, lb_all[l])
    return (y_p, y_s, jnp.stack(new_k, axis=1), jnp.stack(new_v, axis=1), jnp.stack(new_hg, axis=1),
            jnp.stack(new_re, axis=1), jnp.stack(new_im, axis=1))
```

```python
import functools
import math

import jax
import jax.numpy as jnp
import numpy as np
from jax import lax
from jax.experimental import pallas as pl
from jax.experimental.pallas import tpu as pltpu

D_MODEL = 2048
BATCH = 16
SEQ = 256
DEPTH = 2
DEC_BATCH = 2
DEC_SEQ = 2048
GRID_W = 64
N_HEADS = 8
N_KV_HEADS = 2
HEAD_DIM = 128
Q_GROUP = N_HEADS // N_KV_HEADS
WINDOW = 128
ATTN_BLOCK = 128
ROPE_BASE = 10000.0
ATT_Q_W = N_HEADS * HEAD_DIM
ATT_KV_W = N_KV_HEADS * HEAD_DIM
HG_HEADS = 8
HG_DK = 128
HG_DV = 128
HG_W = HG_HEADS * HG_DK
HG_CHUNK = 16
S5_W = 1024
S5_CH = 16
S5_GROUPS = S5_W // S5_CH
S5_STATE = 64
N_EXPERTS = 32
TOP_K = 4
D_FF = 2048
SWIGLU_ALPHA = 1.702
SWIGLU_LIMIT = 7.0
N_MOD = 6
EPS = 1e-6
IN_SIZES = (ATT_Q_W, ATT_KV_W, ATT_KV_W, HG_W, HG_W, HG_W, HG_W, HG_W, S5_W, D_MODEL, D_MODEL, D_MODEL)
IN_COLS = sum(IN_SIZES)

N_CTX = BATCH * SEQ
N_LAT = DEC_BATCH * DEC_SEQ
N_TOK = N_CTX + N_LAT
N_GROUPS = 1 + DEC_BATCH

LANES = 128
ROW_TILE = 256
VMEM_LIMIT = 56 * 1024 * 1024

MOE_SUPER = 1024
MOE_SUB = 256
MOE_FC = 256
MOE_NC = D_FF // MOE_FC
MOE_NS = (N_TOK * TOP_K) // MOE_SUPER + N_EXPERTS

_bf16 = jnp.bfloat16
_f32 = jnp.float32


def _group_of_tile(i):
    ctx_tiles = N_CTX // ROW_TILE
    lat_tiles = DEC_SEQ // ROW_TILE
    return jnp.where(i < ctx_tiles, 0, 1 + (i - ctx_tiles) // lat_tiles)


def _split_bf16(a):
    hi = a.astype(_bf16)
    lo = (a - hi.astype(_f32)).astype(_bf16)
    return hi, lo


def _router_kernel(x_ref, g_ref, sc_ref, sh_ref, rw_ref, rb_ref, h_ref, idx_ref, w_ref):
    x = x_ref[...]
    y = x * lax.rsqrt(jnp.mean(x * x, axis=-1, keepdims=True) + EPS) * g_ref[...]
    h = y * (1.0 + sc_ref[...]) + sh_ref[...]
    h_ref[...] = h.astype(_bf16)
    h_hi, h_lo = _split_bf16(h)
    r_hi, r_lo = _split_bf16(rw_ref[...])
    logits = (jnp.dot(h_hi, r_hi, preferred_element_type=_f32)
              + jnp.dot(h_hi, r_lo, preferred_element_type=_f32)
              + jnp.dot(h_lo, r_hi, preferred_element_type=_f32)) + rb_ref[...]
    lane = lax.broadcasted_iota(jnp.int32, logits.shape, 1)
    work = logits
    vals, idxs = [], []
    for _ in range(TOP_K):
        m = jnp.max(work, axis=-1, keepdims=True)
        i = jnp.min(jnp.where(work == m, lane, N_EXPERTS), axis=-1, keepdims=True)
        vals.append(m)
        idxs.append(i)
        work = jnp.where(lane == i, -jnp.inf, work)
    es = [jnp.exp(v - vals[0]) for v in vals]
    den = es[0] + es[1] + es[2] + es[3]
    out_lane = lax.broadcasted_iota(jnp.int32, idx_ref.shape, 1)
    idx_out = jnp.zeros(idx_ref.shape, jnp.int32)
    w_out = jnp.zeros(w_ref.shape, _f32)
    for k in range(TOP_K):
        idx_out = jnp.where(out_lane == k, idxs[k], idx_out)
        w_out = jnp.where(out_lane == k, es[k] / den, w_out)
    idx_ref[...] = idx_out
    w_ref[...] = w_out


def _router(x, norm_g, sc, sh, router_w, router_b, layer):
    nt = N_TOK // ROW_TILE
    row = lambda i: (i, 0)
    grp = lambda i: (_group_of_tile(i), 0, 0)
    lay = lambda i: (layer, 0, 0)
    h, idx, w = pl.pallas_call(
        _router_kernel,
        grid=(nt,),
        in_specs=[
            pl.BlockSpec((ROW_TILE, D_MODEL), row),
            pl.BlockSpec((None, 1, D_MODEL), lay),
            pl.BlockSpec((None, 1, D_MODEL), grp),
            pl.BlockSpec((None, 1, D_MODEL), grp),
            pl.BlockSpec((None, D_MODEL, N_EXPERTS), lay),
            pl.BlockSpec((None, 1, N_EXPERTS), lay),
        ],
        out_specs=[
            pl.BlockSpec((ROW_TILE, D_MODEL), row),
            pl.BlockSpec((ROW_TILE, LANES), row),
            pl.BlockSpec((ROW_TILE, LANES), row),
        ],
        out_shape=[
            jax.ShapeDtypeStruct((N_TOK, D_MODEL), _bf16),
            jax.ShapeDtypeStruct((N_TOK, LANES), jnp.int32),
            jax.ShapeDtypeStruct((N_TOK, LANES), _f32),
        ],
        compiler_params=pltpu.CompilerParams(dimension_semantics=("arbitrary",), vmem_limit_bytes=VMEM_LIMIT),
        name="router",
    )(x, norm_g.reshape(DEPTH, 1, D_MODEL), sc, sh, router_w, router_b.reshape(DEPTH, 1, N_EXPERTS))
    return h, idx[:, :TOP_K], w[:, :TOP_K]


def _moe_kernel(exp_ref, rows_ref, used_ref, x_ref, rw_ref, w1g_ref, w1u_ref, b1g_ref, b1u_ref, w2_ref, b2_ref,
                o_ref, w1g_s, w1u_s, w2_s):
    s = pl.program_id(0)
    c = pl.program_id(1)
    nvalid = rows_ref[s]

    @pl.when(c == 0)
    def _():
        o_ref[...] = jnp.zeros(o_ref.shape, _f32)

    @pl.when(nvalid > 0)
    def _():
        w1g_s[...] = w1g_ref[...].astype(_bf16)
        w1u_s[...] = w1u_ref[...].astype(_bf16)
        w2_s[...] = w2_ref[...].astype(_bf16)
        for r in range(MOE_SUPER // MOE_SUB):
            @pl.when(r * MOE_SUB < nvalid)
            def _():
                rows = pl.ds(r * MOE_SUB, MOE_SUB)
                xs = x_ref[rows, :]
                zg = jnp.dot(xs, w1g_s[...], preferred_element_type=_f32) + b1g_ref[...]
                zu = jnp.dot(xs, w1u_s[...], preferred_element_type=_f32) + b1u_ref[...]
                g = jnp.minimum(zg, SWIGLU_LIMIT)
                u = jnp.clip(zu, -SWIGLU_LIMIT, SWIGLU_LIMIT)
                act = g * jax.nn.sigmoid(SWIGLU_ALPHA * g) * (u + 1.0)
                o_ref[rows, :] += jnp.dot(act.astype(_bf16), w2_s[...], preferred_element_type=_f32)

        @pl.when(c == MOE_NC - 1)
        def _():
            o_ref[...] = rw_ref[...] * (o_ref[...] + b2_ref[...])


def _moe_ffn(x_sorted, row_w, st_expert, st_rows, n_used, w1, b1, w2, b2, layer):
    def xmap(s, c, e_ref, r_ref, u_ref):
        return (jnp.minimum(s, u_ref[0] - 1), 0)

    def chunk(s, c, u_ref):
        return jnp.where(s < u_ref[0], c, MOE_NC - 1)

    def w1g_map(s, c, e_ref, r_ref, u_ref):
        return (layer, e_ref[s], 0, chunk(s, c, u_ref))

    def w1u_map(s, c, e_ref, r_ref, u_ref):
        return (layer, e_ref[s], 0, MOE_NC + chunk(s, c, u_ref))

    def w2_map(s, c, e_ref, r_ref, u_ref):
        return (layer, e_ref[s], chunk(s, c, u_ref), 0)

    def b2_map(s, c, e_ref, r_ref, u_ref):
        return (layer, e_ref[s], 0, 0)

    grid_spec = pltpu.PrefetchScalarGridSpec(
        num_scalar_prefetch=3,
        grid=(MOE_NS, MOE_NC),
        in_specs=[
            pl.BlockSpec((MOE_SUPER, D_MODEL), xmap),
            pl.BlockSpec((MOE_SUPER, 1), xmap),
            pl.BlockSpec((None, None, D_MODEL, MOE_FC), w1g_map),
            pl.BlockSpec((None, None, D_MODEL, MOE_FC), w1u_map),
            pl.BlockSpec((None, None, 1, MOE_FC), w1g_map),
            pl.BlockSpec((None, None, 1, MOE_FC), w1u_map),
            pl.BlockSpec((None, None, MOE_FC, D_MODEL), w2_map),
            pl.BlockSpec((None, None, 1, D_MODEL), b2_map),
        ],
        out_specs=pl.BlockSpec((MOE_SUPER, D_MODEL), lambda s, c, e_ref, r_ref, u_ref: (s, 0)),
        scratch_shapes=[
            pltpu.VMEM((D_MODEL, MOE_FC), _bf16),
            pltpu.VMEM((D_MODEL, MOE_FC), _bf16),
            pltpu.VMEM((MOE_FC, D_MODEL), _bf16),
        ],
    )
    return pl.pallas_call(
        _moe_kernel,
        grid_spec=grid_spec,
        out_shape=jax.ShapeDtypeStruct((MOE_NS * MOE_SUPER, D_MODEL), _f32),
        compiler_params=pltpu.CompilerParams(dimension_semantics=("arbitrary", "arbitrary"),
                                             vmem_limit_bytes=VMEM_LIMIT),
        name="moe_ffn",
    )(st_expert, st_rows, n_used, x_sorted, row_w, w1, w1,
      b1.reshape(DEPTH, N_EXPERTS, 1, 2 * D_FF), b1.reshape(DEPTH, N_EXPERTS, 1, 2 * D_FF),
      w2, b2.reshape(DEPTH, N_EXPERTS, 1, D_MODEL))


def _route_metadata(top_idx, top_w):
    n = N_TOK * TOP_K
    e_flat = top_idx.reshape(n)
    order = jnp.argsort(e_flat, stable=True).astype(jnp.int32)
    e_sorted = e_flat[order]
    counts = jnp.zeros((N_EXPERTS,), jnp.int32).at[e_flat].add(1)
    n_st = (counts + MOE_SUPER - 1) // MOE_SUPER
    st_end = jnp.cumsum(n_st)
    st_start = st_end - n_st
    grp_start = jnp.cumsum(counts) - counts
    rank = jnp.arange(n, dtype=jnp.int32) - grp_start[e_sorted]
    dest = st_start[e_sorted] * MOE_SUPER + rank
    n_used = st_end[-1]
    s_ids = jnp.arange(MOE_NS, dtype=jnp.int32)
    s_clamped = jnp.minimum(s_ids, n_used - 1)
    st_expert = jnp.searchsorted(st_end, s_clamped, side="right").astype(jnp.int32)
    st_rows = jnp.clip(counts[st_expert] - (s_clamped - st_start[st_expert]) * MOE_SUPER, 0, MOE_SUPER)
    st_rows = jnp.where(s_ids < n_used, st_rows, 0).astype(jnp.int32)
    src_tok = jnp.zeros((MOE_NS * MOE_SUPER,), jnp.int32).at[dest].set(order // TOP_K)
    row_w = jnp.zeros((MOE_NS * MOE_SUPER,), _f32).at[dest].set(top_w.reshape(n)[order])
    pos = jnp.zeros((n,), jnp.int32).at[order].set(dest)
    return src_tok, row_w.reshape(-1, 1), st_expert, st_rows, n_used.reshape(1).astype(jnp.int32), pos.reshape(N_TOK, TOP_K)


def _moe_sublayer(x, mods, p, layer):
    sh2, sc2, gt2 = mods[3], mods[4], mods[5]
    h, top_idx, top_w = _router(x, p["norm2_g"], sc2, sh2, p["router_w"], p["router_b"], layer)
    src_tok, row_w, st_expert, st_rows, n_used, pos = _route_metadata(top_idx, top_w)
    x_sorted = jnp.take(h, src_tok, axis=0)
    y_sorted = _moe_ffn(x_sorted, row_w, st_expert, st_rows, n_used, p["exp_w1"], p["exp_b1"], p["exp_w2"],
                        p["exp_b2"], layer)
    y = jnp.take(y_sorted, pos.reshape(-1), axis=0).reshape(N_TOK, TOP_K, D_MODEL).sum(axis=1)
    gt_rows = jnp.repeat(gt2[:, 0, :], jnp.array([N_CTX, DEC_SEQ, DEC_SEQ]), axis=0, total_repeat_length=N_TOK)
    return x + gt_rows * y


def _rms_norm(x, g):
    xf = x.astype(_f32)
    y = xf * lax.rsqrt(jnp.mean(xf * xf, axis=-1, keepdims=True) + EPS)
    return (y * g.astype(_f32)).astype(x.dtype)


def _split_in(z):
    return jnp.split(z, np.cumsum(IN_SIZES)[:-1].tolist(), axis=-1)


def _attn_heads(q, k, v, q_g, k_g):
    b, t = q.shape[:2]
    q = _rms_norm(q.reshape(b, t, N_HEADS, HEAD_DIM), q_g)
    k = _rms_norm(k.reshape(b, t, N_KV_HEADS, HEAD_DIM), k_g)
    return q, k, v.reshape(b, t, N_KV_HEADS, HEAD_DIM)


def _rotate(x, pos):
    nf = x.shape[-1] // 2
    inv = ROPE_BASE ** (-jnp.arange(nf, dtype=_f32) / nf)
    ang = pos.astype(_f32)[:, None] * inv[None, :]
    cos = jnp.cos(ang)[None, :, None, :]
    sin = jnp.sin(ang)[None, :, None, :]
    x1, x2 = x[..., :nf], x[..., nf:]
    return jnp.concatenate([x1 * cos - x2 * sin, x1 * sin + x2 * cos], axis=-1)


def _axial_rope(x):
    t = x.shape[1]
    rows = t // GRID_W
    row = jnp.repeat(jnp.arange(rows), GRID_W)
    col = jnp.tile(jnp.arange(GRID_W), rows)
    half = HEAD_DIM // 2
    return jnp.concatenate([_rotate(x[..., :half], row), _rotate(x[..., half:], col)], axis=-1)


def _sink_column(sink, lead_shape):
    s = sink.astype(_f32).reshape(N_KV_HEADS, Q_GROUP, 1, 1)
    return jnp.broadcast_to(s, lead_shape + (1,))


def _context_attention(q, k, v, sink):
    b, l = q.shape[:2]
    scale = HEAD_DIM ** -0.5
    qb = q.reshape(b, l, N_KV_HEADS, Q_GROUP, HEAD_DIM)
    s = jnp.einsum('bqhgd,bshd->bhgqs', qb, k).astype(_f32) * scale
    s = jnp.concatenate([s, _sink_column(sink, s.shape[:-1])], axis=-1)
    pr = jax.nn.softmax(s, axis=-1)[..., :l]
    o = jnp.einsum('bhgqs,bshd->bqhgd', pr, v)
    return o.reshape(b, l, N_HEADS * HEAD_DIM)


def _window_attention(q, k, v, k_ctx, v_ctx, sink):
    b, t = q.shape[:2]
    nb = t // ATTN_BLOCK
    lc = k_ctx.shape[1]
    scale = HEAD_DIM ** -0.5
    qb = q.reshape(b, nb, ATTN_BLOCK, N_KV_HEADS, Q_GROUP, HEAD_DIM)
    pad = ((0, 0), (ATTN_BLOCK, ATTN_BLOCK), (0, 0), (0, 0))

    def bands(a):
        ap = jnp.pad(a, pad).reshape(b, nb + 2, ATTN_BLOCK, N_KV_HEADS, HEAD_DIM)
        return jnp.concatenate([ap[:, :-2], ap[:, 1:-1], ap[:, 2:]], axis=2)

    kb, vb = bands(k), bands(v)
    blk = jnp.arange(nb)[:, None] * ATTN_BLOCK
    qpos = blk + jnp.arange(ATTN_BLOCK)[None, :]
    kpos = blk - ATTN_BLOCK + jnp.arange(3 * ATTN_BLOCK)[None, :]
    rel = kpos[:, None, :] - qpos[:, :, None]
    valid = (jnp.abs(rel) <= WINDOW) & (kpos[:, None, :] >= 0) & (kpos[:, None, :] < t)
    s_loc = jnp.einsum('bnqhgd,bnshd->bnhgqs', qb, kb).astype(_f32) * scale
    s_loc = jnp.where(valid[None, :, None, None], s_loc, -jnp.inf)
    s_ctx = jnp.einsum('bnqhgd,bshd->bnhgqs', qb, k_ctx).astype(_f32) * scale
    s = jnp.concatenate([s_loc, s_ctx, _sink_column(sink, s_loc.shape[:-1])], axis=-1)
    pr = jax.nn.softmax(s, axis=-1)
    w = 3 * ATTN_BLOCK
    o = (jnp.einsum('bnhgqs,bnshd->bnqhgd', pr[..., :w], vb)
         + jnp.einsum('bnhgqs,bshd->bnqhgd', pr[..., w:w + lc], v_ctx))
    return o.reshape(b, t, N_HEADS * HEAD_DIM)


def _hgrn2_forget(f_pre, lb):
    x = f_pre.astype(_f32)
    log_f = jnp.logaddexp(jnp.log(lb), jnp.log1p(-lb) + jax.nn.log_sigmoid(x))
    return 1.0 - jnp.exp(log_f), log_f


def _gla_chunkwise(q, k, v, log_f, s0):
    b, t, h, dk = q.shape
    dv = v.shape[-1]
    nc = t // HG_CHUNK
    q, k, log_f = (a.reshape(b, nc, HG_CHUNK, h, dk) for a in (q, k, log_f))
    v = v.reshape(b, nc, HG_CHUNK, h, dv)
    cum = jnp.cumsum(log_f, axis=2)
    last = cum[:, :, -1]
    causal = jnp.tril(jnp.ones((HG_CHUNK, HG_CHUNK), dtype=bool))[None, None, :, :, None, None]
    diff = cum[:, :, :, None] - cum[:, :, None, :]
    decay = jnp.exp(jnp.where(causal, diff, -jnp.inf))
    scores = jnp.einsum('bctshd,bcthd,bcshd->bchts', decay, q, k)
    o_intra = jnp.einsum('bchts,bcshe->bcthe', scores, v)
    kv = jnp.einsum('bcshd,bcshe->bchde', k * jnp.exp(last[:, :, None] - cum), v)

    def step(state, inp):
        g_c, kv_c = inp
        return jnp.exp(g_c)[..., None] * state + kv_c, state

    s_final, s_prev = lax.scan(step, s0, (jnp.moveaxis(last, 1, 0), jnp.moveaxis(kv, 1, 0)))
    s_prev = jnp.moveaxis(s_prev, 0, 1)
    o_inter = jnp.einsum('bcthd,bchde->bcthe', q * jnp.exp(cum), s_prev)
    return (o_intra + o_inter).reshape(b, t, h, dv), s_final


def _hgrn2_bidir(hq, hff, hfb, hi, hg, lb, norm_g, s0):
    b, t = hq.shape[:2]

    def heads(a):
        return a.astype(_f32).reshape(b, t, HG_HEADS, HG_DK)

    def flip(a):
        return jnp.flip(a, axis=1)

    q = heads(jax.nn.silu(hq.astype(_f32)) * HG_DK ** -0.5)
    v = heads(hi)
    k_f, lf_f = _hgrn2_forget(hff, lb[0])
    k_b, lf_b = _hgrn2_forget(hfb, lb[1])
    s0 = s0.astype(_f32)
    o_f, s_f = _gla_chunkwise(q, heads(k_f), v, heads(lf_f), s0[:, 0])
    o_b, s_b = _gla_chunkwise(flip(q), flip(heads(k_b)), flip(v), flip(heads(lf_b)), s0[:, 1])
    o = _rms_norm(o_f + flip(o_b), norm_g) * jax.nn.silu(heads(hg))
    return o.reshape(b, t, HG_W), jnp.stack([s_f, s_b], axis=1)


def _lin_combine(c1, c2):
    a1, b1 = c1
    a2, b2 = c2
    return a1 * a2, a2 * b1 + b2


def _s5_bidir(u, p, x0_re, x0_im):
    b, t, _ = u.shape
    uf = u.astype(_f32).reshape(b, t, S5_GROUPS, S5_CH)
    b_mat = lax.complex(p['s5_b_re'], p['s5_b_im'])
    c_mat = lax.complex(p['s5_c_re'], p['s5_c_im'])
    lam = lax.complex(p['s5_a_re'], p['s5_a_im'])
    dt = jnp.exp(p['s5_log_dt'])[..., None]
    a_bar = jnp.exp(lam * dt)
    gamma = (a_bar - 1.0) / lam
    x0 = lax.complex(x0_re.astype(_f32), x0_im.astype(_f32))
    bu = jnp.einsum('gpc,btgc->btgp', b_mat, uf.astype(jnp.complex64))

    def run(d, seq):
        elems_b = (gamma[d] * seq).at[:, 0].add(a_bar[d] * x0[:, d])
        elems_a = jnp.broadcast_to(a_bar[d], elems_b.shape)
        _, xs = lax.associative_scan(_lin_combine, (elems_a, elems_b), axis=1)
        return xs

    xs_f = run(0, bu)
    xs_b = run(1, jnp.flip(bu, axis=1))
    y = jnp.real(jnp.einsum('gcp,btgp->btgc', c_mat, xs_f + jnp.flip(xs_b, axis=1)))
    y = y + p['s5_d'].reshape(S5_GROUPS, S5_CH) * uf
    y = jax.nn.gelu(y.reshape(b, t, S5_W))
    out = y * jax.nn.sigmoid(y @ p['s5_w_glu'] + p['s5_b_glu'])
    final = jnp.stack([xs_f[:, -1], xs_b[:, -1]], axis=1)
    return out, jnp.real(final), jnp.imag(final)


def _merge_branches(a_out, h_out, s_out, ga, gh, gs, p):
    y = (jax.nn.sigmoid(ga) * (a_out @ p['w_br_attn'])
         + jax.nn.sigmoid(gh) * (h_out @ p['w_br_hg'])
         + jax.nn.sigmoid(gs) * (s_out @ p['w_br_s5']))
    return y @ p['w_out']


def _mixer_sublayer_jax(xc, xl, mods, pl_, lb, ck, cv, s_hg0, s5_re0, s5_im0):
    sh1, sc1, gt1 = mods[0], mods[1], mods[2]

    def pre(x, sh, sc):
        h = _rms_norm(x, pl_['norm1_g']) * (1.0 + sc) + sh
        return _split_in(h @ pl_['w_in'])

    q, k, v, hq, hff, hfb, hi, hg, su, ga, gh, gs = pre(xc, sh1[0], sc1[0])
    q, k, v = _attn_heads(q, k, v, pl_['q_norm_g'], pl_['k_norm_g'])
    a_out = _context_attention(q, k, v, pl_['sink'])
    z_hg = jnp.zeros((BATCH, 2, HG_HEADS, HG_DK, HG_DV), _f32)
    h_out, s_hg = _hgrn2_bidir(hq, hff, hfb, hi, hg, lb, pl_['hg_norm_g'], z_hg)
    z_s5 = jnp.zeros((BATCH, 2, S5_GROUPS, S5_STATE), _f32)
    s_out, s5_re, s5_im = _s5_bidir(su, pl_, z_s5, z_s5)
    xc = xc + gt1[0] * _merge_branches(a_out, h_out, s_out, ga, gh, gs, pl_)
    new = (k, v, s_hg, s5_re, s5_im)
    sh, sc, gt = sh1[1:, None], sc1[1:, None], gt1[1:, None]
    q, k, v, hq, hff, hfb, hi, hg, su, ga, gh, gs = pre(xl, sh[:, 0], sc[:, 0])
    q, k, v = _attn_heads(q, k, v, pl_['q_norm_g'], pl_['k_norm_g'])
    a_out = _window_attention(_axial_rope(q), _axial_rope(k), v, ck, cv, pl_['sink'])
    h_out, _ = _hgrn2_bidir(hq, hff, hfb, hi, hg, lb, pl_['hg_norm_g'], s_hg0)
    s_out, _, _ = _s5_bidir(su, pl_, s5_re0, s5_im0)
    xl = xl + gt[:, 0] * _merge_branches(a_out, h_out, s_out, ga, gh, gs, pl_)
    return xc, xl, new


def kernel(x_prompt, x_sample, cache_k, cache_v, state_hgrn, state_s5_re, state_s5_im, c, c_ctx, w_mod, b_mod, norm1_g, norm2_g, w_in, q_norm_g, k_norm_g, attn_sink, hg_lb_logits, hg_norm_g, s5_a_re, s5_a_im, s5_log_dt, s5_b_re, s5_b_im, s5_c_re, s5_c_im, s5_d, s5_w_glu, s5_b_glu, w_br_attn, w_br_hg, w_br_s5, w_out, router_w, router_b, exp_w1, exp_b1, exp_w2, exp_b2):
    lb_all = jnp.cumsum(jax.nn.softmax(hg_lb_logits.astype(_f32), axis=0), axis=0)
    lb_all = lb_all - lb_all[:1]
    cond = jnp.concatenate([c_ctx[None, :], c], axis=0)
    xc, xl = x_prompt, x_sample
    new_k, new_v, new_hg, new_re, new_im = [], [], [], [], []
    moe_p = {"norm2_g": norm2_g, "router_w": router_w, "router_b": router_b, "exp_w1": exp_w1, "exp_b1": exp_b1,
             "exp_w2": exp_w2, "exp_b2": exp_b2}
    for l in range(DEPTH):
        p = {
            'norm1_g': norm1_g[l], 'w_in': w_in[l], 'q_norm_g': q_norm_g[l], 'k_norm_g': k_norm_g[l],
            'sink': attn_sink[l], 'hg_norm_g': hg_norm_g[l], 's5_a_re': s5_a_re[l], 's5_a_im': s5_a_im[l],
            's5_log_dt': s5_log_dt[l], 's5_b_re': s5_b_re[l], 's5_b_im': s5_b_im[l], 's5_c_re': s5_c_re[l],
            's5_c_im': s5_c_im[l], 's5_d': s5_d[l], 's5_w_glu': s5_w_glu[l], 's5_b_glu': s5_b_glu[l],
            'w_br_attn': w_br_attn[l], 'w_br_hg': w_br_hg[l], 'w_br_s5': w_br_s5[l], 'w_out': w_out[l],
        }
        m = jax.nn.silu(cond) @ w_mod[l] + b_mod[l]
        mods = [a[:, None, :] for a in jnp.split(m, N_MOD, axis=-1)]
        xc, xl, (k_l, v_l, hg_l, re_l, im_l) = _mixer_sublayer_jax(
            xc, xl, mods, p, lb_all[l], cache_k[:, l], cache_v[:, l], state_hgrn[:, l], state_s5_re[:, l],
            state_s5_im[:, l])
        new_k.append(k_l)
        new_v.append(v_l)
        new_hg.append(hg_l)
        new_re.append(re_l)
        new_im.append(im_l)
        x = jnp.concatenate([xc.reshape(N_CTX, D_MODEL), xl.reshape(N_LAT, D_MODEL)], axis=0)
        x = _moe_sublayer(x, mods, moe_p, l)
        xc = x[:N_CTX].reshape(BATCH, SEQ, D_MODEL)
        xl = x[N_CTX:].reshape(DEC_BATCH, DEC_SEQ, D_MODEL)
    return (xc, xl, jnp.stack(new_k, axis=1), jnp.stack(new_v, axis=1), jnp.stack(new_hg, axis=1),
            jnp.stack(new_re, axis=1), jnp.stack(new_im, axis=1))
```

```python
import functools
import math

import jax
import jax.numpy as jnp
import numpy as np
from jax import lax
from jax.experimental import pallas as pl
from jax.experimental.pallas import tpu as pltpu

D_MODEL = 2048
BATCH = 16
SEQ = 256
DEPTH = 2
DEC_BATCH = 2
DEC_SEQ = 2048
GRID_W = 64
N_HEADS = 8
N_KV_HEADS = 2
HEAD_DIM = 128
Q_GROUP = N_HEADS // N_KV_HEADS
WINDOW = 128
ATTN_BLOCK = 128
ROPE_BASE = 10000.0
ATT_Q_W = N_HEADS * HEAD_DIM
ATT_KV_W = N_KV_HEADS * HEAD_DIM
HG_HEADS = 8
HG_DK = 128
HG_DV = 128
HG_W = HG_HEADS * HG_DK
HG_CHUNK = 16
S5_W = 1024
S5_CH = 16
S5_GROUPS = S5_W // S5_CH
S5_STATE = 64
N_EXPERTS = 32
TOP_K = 4
D_FF = 2048
SWIGLU_ALPHA = 1.702
SWIGLU_LIMIT = 7.0
N_MOD = 6
EPS = 1e-6
IN_SIZES = (ATT_Q_W, ATT_KV_W, ATT_KV_W, HG_W, HG_W, HG_W, HG_W, HG_W, S5_W, D_MODEL, D_MODEL, D_MODEL)
IN_COLS = sum(IN_SIZES)

N_CTX = BATCH * SEQ
N_LAT = DEC_BATCH * DEC_SEQ
N_TOK = N_CTX + N_LAT
N_GROUPS = 1 + DEC_BATCH

LANES = 128
ROW_TILE = 256
VMEM_LIMIT = 56 * 1024 * 1024

MOE_SUPER = 1024
MOE_SUB = 256
MOE_FC = 256
MOE_NC = D_FF // MOE_FC
MOE_NS = (N_TOK * TOP_K) // MOE_SUPER + N_EXPERTS
DISPATCH_TILE = 256
COMBINE_TILE = 64

S5_SLAB_GROUPS = 8
S5_NSLAB = S5_GROUPS // S5_SLAB_GROUPS
S5_SLAB_CH = S5_SLAB_GROUPS * S5_CH
S5_SLAB_ST = S5_SLAB_GROUPS * S5_STATE
S5_BLOCK_ROWS = 1024
SUBLANES = 8

_bf16 = jnp.bfloat16
_f32 = jnp.float32


def _group_of_tile(i):
    ctx_tiles = N_CTX // ROW_TILE
    lat_tiles = DEC_SEQ // ROW_TILE
    return jnp.where(i < ctx_tiles, 0, 1 + (i - ctx_tiles) // lat_tiles)


def _split_bf16(a):
    hi = a.astype(_bf16)
    lo = (a - hi.astype(_f32)).astype(_bf16)
    return hi, lo


def _router_kernel(x_ref, g_ref, sc_ref, sh_ref, rw_ref, rb_ref, h_ref, idx_ref, w_ref):
    x = x_ref[...]
    y = x * lax.rsqrt(jnp.mean(x * x, axis=-1, keepdims=True) + EPS) * g_ref[...]
    h = y * (1.0 + sc_ref[...]) + sh_ref[...]
    h_ref[...] = h.astype(_bf16)
    h_hi, h_lo = _split_bf16(h)
    r_hi, r_lo = _split_bf16(rw_ref[...])
    logits = (jnp.dot(h_hi, r_hi, preferred_element_type=_f32)
              + jnp.dot(h_hi, r_lo, preferred_element_type=_f32)
              + jnp.dot(h_lo, r_hi, preferred_element_type=_f32)) + rb_ref[...]
    lane = lax.broadcasted_iota(jnp.int32, logits.shape, 1)
    work = logits
    vals, idxs = [], []
    for _ in range(TOP_K):
        m = jnp.max(work, axis=-1, keepdims=True)
        i = jnp.min(jnp.where(work == m, lane, N_EXPERTS), axis=-1, keepdims=True)
        vals.append(m)
        idxs.append(i)
        work = jnp.where(lane == i, -jnp.inf, work)
    es = [jnp.exp(v - vals[0]) for v in vals]
    den = es[0] + es[1] + es[2] + es[3]
    out_lane = lax.broadcasted_iota(jnp.int32, idx_ref.shape, 1)
    idx_out = jnp.zeros(idx_ref.shape, jnp.int32)
    w_out = jnp.zeros(w_ref.shape, _f32)
    for k in range(TOP_K):
        idx_out = jnp.where(out_lane == k, idxs[k], idx_out)
        w_out = jnp.where(out_lane == k, es[k] / den, w_out)
    idx_ref[...] = idx_out
    w_ref[...] = w_out


def _router(x, norm_g, sc, sh, router_w, router_b, layer):
    nt = N_TOK // ROW_TILE
    row = lambda i: (i, 0)
    grp = lambda i: (_group_of_tile(i), 0, 0)
    lay = lambda i: (layer, 0, 0)
    h, idx, w = pl.pallas_call(
        _router_kernel,
        grid=(nt,),
        in_specs=[
            pl.BlockSpec((ROW_TILE, D_MODEL), row),
            pl.BlockSpec((None, 1, D_MODEL), lay),
            pl.BlockSpec((None, 1, D_MODEL), grp),
            pl.BlockSpec((None, 1, D_MODEL), grp),
            pl.BlockSpec((None, D_MODEL, N_EXPERTS), lay),
            pl.BlockSpec((None, 1, N_EXPERTS), lay),
        ],
        out_specs=[
            pl.BlockSpec((ROW_TILE, D_MODEL), row),
            pl.BlockSpec((ROW_TILE, LANES), row),
            pl.BlockSpec((ROW_TILE, LANES), row),
        ],
        out_shape=[
            jax.ShapeDtypeStruct((N_TOK, D_MODEL), _bf16),
            jax.ShapeDtypeStruct((N_TOK, LANES), jnp.int32),
            jax.ShapeDtypeStruct((N_TOK, LANES), _f32),
        ],
        compiler_params=pltpu.CompilerParams(dimension_semantics=("arbitrary",), vmem_limit_bytes=VMEM_LIMIT),
        name="router",
    )(x, norm_g.reshape(DEPTH, 1, D_MODEL), sc, sh, router_w, router_b.reshape(DEPTH, 1, N_EXPERTS))
    return h, idx[:, :TOP_K], w[:, :TOP_K]


def _moe_kernel(exp_ref, rows_ref, used_ref, x_ref, rw_ref, w1g_ref, w1u_ref, b1g_ref, b1u_ref, w2_ref, b2_ref,
                o_ref, w1g_s, w1u_s, w2_s):
    s = pl.program_id(0)
    c = pl.program_id(1)
    nvalid = rows_ref[s]

    @pl.when(c == 0)
    def _():
        o_ref[...] = jnp.zeros(o_ref.shape, _f32)

    @pl.when(nvalid > 0)
    def _():
        w1g_s[...] = w1g_ref[...].astype(_bf16)
        w1u_s[...] = w1u_ref[...].astype(_bf16)
        w2_s[...] = w2_ref[...].astype(_bf16)
        for r in range(MOE_SUPER // MOE_SUB):
            @pl.when(r * MOE_SUB < nvalid)
            def _():
                rows = pl.ds(r * MOE_SUB, MOE_SUB)
                xs = x_ref[rows, :]
                zg = jnp.dot(xs, w1g_s[...], preferred_element_type=_f32) + b1g_ref[...]
                zu = jnp.dot(xs, w1u_s[...], preferred_element_type=_f32) + b1u_ref[...]
                g = jnp.minimum(zg, SWIGLU_LIMIT)
                u = jnp.clip(zu, -SWIGLU_LIMIT, SWIGLU_LIMIT)
                act = g * jax.nn.sigmoid(SWIGLU_ALPHA * g) * (u + 1.0)
                o_ref[rows, :] += jnp.dot(act.astype(_bf16), w2_s[...], preferred_element_type=_f32)

        @pl.when(c == MOE_NC - 1)
        def _():
            o_ref[...] = rw_ref[...] * (o_ref[...] + b2_ref[...])


def _moe_ffn(x_sorted, row_w, st_expert, st_rows, n_used, w1, b1, w2, b2, layer):
    def xmap(s, c, e_ref, r_ref, u_ref):
        return (jnp.minimum(s, u_ref[0] - 1), 0)

    def chunk(s, c, u_ref):
        return jnp.where(s < u_ref[0], c, MOE_NC - 1)

    def w1g_map(s, c, e_ref, r_ref, u_ref):
        return (layer, e_ref[s], 0, chunk(s, c, u_ref))

    def w1u_map(s, c, e_ref, r_ref, u_ref):
        return (layer, e_ref[s], 0, MOE_NC + chunk(s, c, u_ref))

    def w2_map(s, c, e_ref, r_ref, u_ref):
        return (layer, e_ref[s], chunk(s, c, u_ref), 0)

    def b2_map(s, c, e_ref, r_ref, u_ref):
        return (layer, e_ref[s], 0, 0)

    grid_spec = pltpu.PrefetchScalarGridSpec(
        num_scalar_prefetch=3,
        grid=(MOE_NS, MOE_NC),
        in_specs=[
            pl.BlockSpec((MOE_SUPER, D_MODEL), xmap),
            pl.BlockSpec((MOE_SUPER, 1), xmap),
            pl.BlockSpec((None, None, D_MODEL, MOE_FC), w1g_map),
            pl.BlockSpec((None, None, D_MODEL, MOE_FC), w1u_map),
            pl.BlockSpec((None, None, 1, MOE_FC), w1g_map),
            pl.BlockSpec((None, None, 1, MOE_FC), w1u_map),
            pl.BlockSpec((None, None, MOE_FC, D_MODEL), w2_map),
            pl.BlockSpec((None, None, 1, D_MODEL), b2_map),
        ],
        out_specs=pl.BlockSpec((MOE_SUPER, D_MODEL), lambda s, c, e_ref, r_ref, u_ref: (s, 0)),
        scratch_shapes=[
            pltpu.VMEM((D_MODEL, MOE_FC), _bf16),
            pltpu.VMEM((D_MODEL, MOE_FC), _bf16),
            pltpu.VMEM((MOE_FC, D_MODEL), _bf16),
        ],
    )
    return pl.pallas_call(
        _moe_kernel,
        grid_spec=grid_spec,
        out_shape=jax.ShapeDtypeStruct((MOE_NS * MOE_SUPER, D_MODEL), _f32),
        compiler_params=pltpu.CompilerParams(dimension_semantics=("arbitrary", "arbitrary"),
                                             vmem_limit_bytes=VMEM_LIMIT),
        name="moe_ffn",
    )(st_expert, st_rows, n_used, x_sorted, row_w, w1, w1,
      b1.reshape(DEPTH, N_EXPERTS, 1, 2 * D_FF), b1.reshape(DEPTH, N_EXPERTS, 1, 2 * D_FF),
      w2, b2.reshape(DEPTH, N_EXPERTS, 1, D_MODEL))


def _route_metadata(top_idx, top_w):
    n = N_TOK * TOP_K
    e_flat = top_idx.reshape(n)
    order = jnp.argsort(e_flat, stable=True).astype(jnp.int32)
    e_sorted = e_flat[order]
    counts = jnp.zeros((N_EXPERTS,), jnp.int32).at[e_flat].add(1)
    n_st = (counts + MOE_SUPER - 1) // MOE_SUPER
    st_end = jnp.cumsum(n_st)
    st_start = st_end - n_st
    grp_start = jnp.cumsum(counts) - counts
    rank = jnp.arange(n, dtype=jnp.int32) - grp_start[e_sorted]
    dest = st_start[e_sorted] * MOE_SUPER + rank
    n_used = st_end[-1]
    s_ids = jnp.arange(MOE_NS, dtype=jnp.int32)
    s_clamped = jnp.minimum(s_ids, n_used - 1)
    st_expert = jnp.searchsorted(st_end, s_clamped, side="right").astype(jnp.int32)
    st_rows = jnp.clip(counts[st_expert] - (s_clamped - st_start[st_expert]) * MOE_SUPER, 0, MOE_SUPER)
    st_rows = jnp.where(s_ids < n_used, st_rows, 0).astype(jnp.int32)
    src_tok = jnp.zeros((MOE_NS * MOE_SUPER,), jnp.int32).at[dest].set(order // TOP_K)
    row_w = jnp.zeros((MOE_NS * MOE_SUPER,), _f32).at[dest].set(top_w.reshape(n)[order])
    pos = jnp.zeros((n,), jnp.int32).at[order].set(dest)
    return src_tok, row_w.reshape(-1, 1), st_expert, st_rows, n_used.reshape(1).astype(jnp.int32), pos.reshape(N_TOK, TOP_K)


def _issue_row_copies(idx_ref, src_hbm, dst_ref, sem, n):
    def start(r, carry):
        pltpu.make_async_copy(src_hbm.at[idx_ref[0, r]], dst_ref.at[r], sem).start()
        return carry

    def wait(r, carry):
        pltpu.make_async_copy(src_hbm.at[0], dst_ref.at[r], sem).wait()
        return carry

    lax.fori_loop(0, n, start, 0)
    lax.fori_loop(0, n, wait, 0)


def _dispatch_kernel(used_ref, idx_ref, src_hbm, o_ref, sem):
    i = pl.program_id(0)

    @pl.when(i < used_ref[0])
    def _():
        _issue_row_copies(idx_ref, src_hbm, o_ref, sem, DISPATCH_TILE)

    @pl.when(i >= used_ref[0])
    def _():
        o_ref[...] = jnp.zeros(o_ref.shape, o_ref.dtype)


def _dispatch(h_words, src_tok, n_used_tiles):
    n_rows = src_tok.shape[0]
    nt = n_rows // DISPATCH_TILE
    width = h_words.shape[1]
    grid_spec = pltpu.PrefetchScalarGridSpec(
        num_scalar_prefetch=1,
        grid=(nt,),
        in_specs=[
            pl.BlockSpec((None, 1, DISPATCH_TILE), lambda i, u: (i, 0, 0), memory_space=pltpu.SMEM),
            pl.BlockSpec(memory_space=pl.ANY),
        ],
        out_specs=pl.BlockSpec((DISPATCH_TILE, width), lambda i, u: (i, 0)),
        scratch_shapes=[pltpu.SemaphoreType.DMA(())],
    )
    return pl.pallas_call(
        _dispatch_kernel,
        grid_spec=grid_spec,
        out_shape=jax.ShapeDtypeStruct((n_rows, width), h_words.dtype),
        compiler_params=pltpu.CompilerParams(dimension_semantics=("arbitrary",), vmem_limit_bytes=VMEM_LIMIT),
        name="moe_dispatch",
    )(n_used_tiles, src_tok.reshape(nt, 1, DISPATCH_TILE), h_words)


def _combine_kernel(pos_ref, y_hbm, x_ref, gt_ref, o_ref, buf, sem):
    _issue_row_copies(pos_ref, y_hbm, buf, sem, TOP_K * COMBINE_TILE)
    y = buf[0:COMBINE_TILE, :]
    for k in range(1, TOP_K):
        y = y + buf[k * COMBINE_TILE:(k + 1) * COMBINE_TILE, :]
    o_ref[...] = x_ref[...] + gt_ref[...] * y


def _combine(y_sorted, pos, x, gt):
    nt = N_TOK // COMBINE_TILE
    tiles_per_row_tile = ROW_TILE // COMBINE_TILE
    pos_tiles = pos.reshape(nt, COMBINE_TILE, TOP_K).transpose(0, 2, 1).reshape(nt, 1, TOP_K * COMBINE_TILE)
    return pl.pallas_call(
        _combine_kernel,
        grid=(nt,),
        in_specs=[
            pl.BlockSpec((None, 1, TOP_K * COMBINE_TILE), lambda i: (i, 0, 0), memory_space=pltpu.SMEM),
            pl.BlockSpec(memory_space=pl.ANY),
            pl.BlockSpec((COMBINE_TILE, D_MODEL), lambda i: (i, 0)),
            pl.BlockSpec((None, 1, D_MODEL), lambda i: (_group_of_tile(i // tiles_per_row_tile), 0, 0)),
        ],
        out_specs=pl.BlockSpec((COMBINE_TILE, D_MODEL), lambda i: (i, 0)),
        out_shape=jax.ShapeDtypeStruct((N_TOK, D_MODEL), _f32),
        scratch_shapes=[pltpu.VMEM((TOP_K * COMBINE_TILE, D_MODEL), _f32), pltpu.SemaphoreType.DMA(())],
        compiler_params=pltpu.CompilerParams(dimension_semantics=("arbitrary",), vmem_limit_bytes=VMEM_LIMIT),
        name="moe_combine",
    )(pos_tiles, y_sorted, x, gt)


def _moe_sublayer(x, mods, p, layer):
    sh2, sc2, gt2 = mods[3], mods[4], mods[5]
    h, top_idx, top_w = _router(x, p["norm2_g"], sc2, sh2, p["router_w"], p["router_b"], layer)
    src_tok, row_w, st_expert, st_rows, n_used, pos = _route_metadata(top_idx, top_w)
    h_words = lax.bitcast_convert_type(h.reshape(N_TOK, D_MODEL // 2, 2), jnp.uint32)
    x_words = _dispatch(h_words, src_tok, n_used * (MOE_SUPER // DISPATCH_TILE))
    x_sorted = lax.bitcast_convert_type(x_words, _bf16).reshape(MOE_NS * MOE_SUPER, D_MODEL)
    y_sorted = _moe_ffn(x_sorted, row_w, st_expert, st_rows, n_used, p["exp_w1"], p["exp_b1"], p["exp_w2"],
                        p["exp_b2"], layer)
    return _combine(y_sorted, pos, x, gt2)


def _rms_norm(x, g):
    xf = x.astype(_f32)
    y = xf * lax.rsqrt(jnp.mean(xf * xf, axis=-1, keepdims=True) + EPS)
    return (y * g.astype(_f32)).astype(x.dtype)


def _split_in(z):
    return jnp.split(z, np.cumsum(IN_SIZES)[:-1].tolist(), axis=-1)


def _attn_heads(q, k, v, q_g, k_g):
    b, t = q.shape[:2]
    q = _rms_norm(q.reshape(b, t, N_HEADS, HEAD_DIM), q_g)
    k = _rms_norm(k.reshape(b, t, N_KV_HEADS, HEAD_DIM), k_g)
    return q, k, v.reshape(b, t, N_KV_HEADS, HEAD_DIM)


def _rotate(x, pos):
    nf = x.shape[-1] // 2
    inv = ROPE_BASE ** (-jnp.arange(nf, dtype=_f32) / nf)
    ang = pos.astype(_f32)[:, None] * inv[None, :]
    cos = jnp.cos(ang)[None, :, None, :]
    sin = jnp.sin(ang)[None, :, None, :]
    x1, x2 = x[..., :nf], x[..., nf:]
    return jnp.concatenate([x1 * cos - x2 * sin, x1 * sin + x2 * cos], axis=-1)


def _axial_rope(x):
    t = x.shape[1]
    rows = t // GRID_W
    row = jnp.repeat(jnp.arange(rows), GRID_W)
    col = jnp.tile(jnp.arange(GRID_W), rows)
    half = HEAD_DIM // 2
    return jnp.concatenate([_rotate(x[..., :half], row), _rotate(x[..., half:], col)], axis=-1)


def _sink_column(sink, lead_shape):
    s = sink.astype(_f32).reshape(N_KV_HEADS, Q_GROUP, 1, 1)
    return jnp.broadcast_to(s, lead_shape + (1,))


def _context_attention(q, k, v, sink):
    b, l = q.shape[:2]
    scale = HEAD_DIM ** -0.5
    qb = q.reshape(b, l, N_KV_HEADS, Q_GROUP, HEAD_DIM)
    s = jnp.einsum('bqhgd,bshd->bhgqs', qb, k).astype(_f32) * scale
    s = jnp.concatenate([s, _sink_column(sink, s.shape[:-1])], axis=-1)
    pr = jax.nn.softmax(s, axis=-1)[..., :l]
    o = jnp.einsum('bhgqs,bshd->bqhgd', pr, v)
    return o.reshape(b, l, N_HEADS * HEAD_DIM)


def _window_attention(q, k, v, k_ctx, v_ctx, sink):
    b, t = q.shape[:2]
    nb = t // ATTN_BLOCK
    lc = k_ctx.shape[1]
    scale = HEAD_DIM ** -0.5
    qb = q.reshape(b, nb, ATTN_BLOCK, N_KV_HEADS, Q_GROUP, HEAD_DIM)
    pad = ((0, 0), (ATTN_BLOCK, ATTN_BLOCK), (0, 0), (0, 0))

    def bands(a):
        ap = jnp.pad(a, pad).reshape(b, nb + 2, ATTN_BLOCK, N_KV_HEADS, HEAD_DIM)
        return jnp.concatenate([ap[:, :-2], ap[:, 1:-1], ap[:, 2:]], axis=2)

    kb, vb = bands(k), bands(v)
    blk = jnp.arange(nb)[:, None] * ATTN_BLOCK
    qpos = blk + jnp.arange(ATTN_BLOCK)[None, :]
    kpos = blk - ATTN_BLOCK + jnp.arange(3 * ATTN_BLOCK)[None, :]
    rel = kpos[:, None, :] - qpos[:, :, None]
    valid = (jnp.abs(rel) <= WINDOW) & (kpos[:, None, :] >= 0) & (kpos[:, None, :] < t)
    s_loc = jnp.einsum('bnqhgd,bnshd->bnhgqs', qb, kb).astype(_f32) * scale
    s_loc = jnp.where(valid[None, :, None, None], s_loc, -jnp.inf)
    s_ctx = jnp.einsum('bnqhgd,bshd->bnhgqs', qb, k_ctx).astype(_f32) * scale
    s = jnp.concatenate([s_loc, s_ctx, _sink_column(sink, s_loc.shape[:-1])], axis=-1)
    pr = jax.nn.softmax(s, axis=-1)
    w = 3 * ATTN_BLOCK
    o = (jnp.einsum('bnhgqs,bnshd->bnqhgd', pr[..., :w], vb)
         + jnp.einsum('bnhgqs,bshd->bnqhgd', pr[..., w:w + lc], v_ctx))
    return o.reshape(b, t, N_HEADS * HEAD_DIM)


def _hgrn2_forget(f_pre, lb):
    x = f_pre.astype(_f32)
    log_f = jnp.logaddexp(jnp.log(lb), jnp.log1p(-lb) + jax.nn.log_sigmoid(x))
    return 1.0 - jnp.exp(log_f), log_f


def _gla_chunkwise(q, k, v, log_f, s0):
    b, t, h, dk = q.shape
    dv = v.shape[-1]
    nc = t // HG_CHUNK
    q, k, log_f = (a.reshape(b, nc, HG_CHUNK, h, dk) for a in (q, k, log_f))
    v = v.reshape(b, nc, HG_CHUNK, h, dv)
    cum = jnp.cumsum(log_f, axis=2)
    last = cum[:, :, -1]
    causal = jnp.tril(jnp.ones((HG_CHUNK, HG_CHUNK), dtype=bool))[None, None, :, :, None, None]
    diff = cum[:, :, :, None] - cum[:, :, None, :]
    decay = jnp.exp(jnp.where(causal, diff, -jnp.inf))
    scores = jnp.einsum('bctshd,bcthd,bcshd->bchts', decay, q, k)
    o_intra = jnp.einsum('bchts,bcshe->bcthe', scores, v)
    kv = jnp.einsum('bcshd,bcshe->bchde', k * jnp.exp(last[:, :, None] - cum), v)

    def step(state, inp):
        g_c, kv_c = inp
        return jnp.exp(g_c)[..., None] * state + kv_c, state

    s_final, s_prev = lax.scan(step, s0, (jnp.moveaxis(last, 1, 0), jnp.moveaxis(kv, 1, 0)))
    s_prev = jnp.moveaxis(s_prev, 0, 1)
    o_inter = jnp.einsum('bcthd,bchde->bcthe', q * jnp.exp(cum), s_prev)
    return (o_intra + o_inter).reshape(b, t, h, dv), s_final


def _hgrn2_bidir(hq, hff, hfb, hi, hg, lb, norm_g, s0):
    b, t = hq.shape[:2]

    def heads(a):
        return a.astype(_f32).reshape(b, t, HG_HEADS, HG_DK)

    def flip(a):
        return jnp.flip(a, axis=1)

    q = heads(jax.nn.silu(hq.astype(_f32)) * HG_DK ** -0.5)
    v = heads(hi)
    k_f, lf_f = _hgrn2_forget(hff, lb[0])
    k_b, lf_b = _hgrn2_forget(hfb, lb[1])
    s0 = s0.astype(_f32)
    o_f, s_f = _gla_chunkwise(q, heads(k_f), v, heads(lf_f), s0[:, 0])
    o_b, s_b = _gla_chunkwise(flip(q), flip(heads(k_b)), flip(v), flip(heads(lf_b)), s0[:, 1])
    o = _rms_norm(o_f + flip(o_b), norm_g) * jax.nn.silu(heads(hg))
    return o.reshape(b, t, HG_W), jnp.stack([s_f, s_b], axis=1)


def _s5_param_kernel(are_ref, aim_ref, ldt_ref, bre_ref, bim_ref, abre_ref, abim_ref, bpre_ref, bpim_ref):
    bre = bre_ref[...]
    bim = bim_ref[...]
    for d in range(2):
        lr = are_ref[d]
        li = aim_ref[d]
        dt = jnp.exp(ldt_ref[d])
        mag = jnp.exp(lr * dt)
        ar = mag * jnp.cos(li * dt)
        ai = mag * jnp.sin(li * dt)
        abre_ref[d] = ar
        abim_ref[d] = ai
        nr = ar - 1.0
        den = lr * lr + li * li
        gr = (nr * lr + ai * li) / den
        gi = (ai * lr - nr * li) / den
        gr3 = gr[:, None, :]
        gi3 = gi[:, None, :]
        bpre_ref[d] = gr3 * bre - gi3 * bim
        bpim_ref[d] = gr3 * bim + gi3 * bre


def _s5_params(a_re, a_im, log_dt, b_re, b_im, c_re, c_im):
    g, pst, ch = S5_GROUPS, S5_STATE, S5_CH
    abre, abim, bpre, bpim = pl.pallas_call(
        _s5_param_kernel,
        out_shape=[
            jax.ShapeDtypeStruct((2, g, pst), _f32),
            jax.ShapeDtypeStruct((2, g, pst), _f32),
            jax.ShapeDtypeStruct((2, g, ch, pst), _f32),
            jax.ShapeDtypeStruct((2, g, ch, pst), _f32),
        ],
        name="s5_params",
    )(a_re, a_im, log_dt.reshape(2, g, 1), b_re.transpose(0, 2, 1), b_im.transpose(0, 2, 1))
    eye = jnp.eye(S5_SLAB_GROUPS, dtype=_f32)

    def b_slabs(bp):
        bp = bp.reshape(2, S5_NSLAB, S5_SLAB_GROUPS, ch, pst)
        w = bp[:, :, :, :, None, :] * eye[None, None, :, None, :, None]
        return w.reshape(2, S5_NSLAB, S5_SLAB_CH, S5_SLAB_ST)

    def c_slabs(cm):
        cm = cm.reshape(S5_NSLAB, S5_SLAB_GROUPS, ch, pst).transpose(0, 1, 3, 2)
        w = cm[:, :, :, None, :] * eye[None, :, None, :, None]
        return w.reshape(S5_NSLAB, S5_SLAB_ST, S5_SLAB_CH)

    b_blk = jnp.concatenate([b_slabs(bpre), b_slabs(bpim)], axis=-1).astype(_bf16)
    return (abre.reshape(2, 1, g * pst), abim.reshape(2, 1, g * pst), b_blk,
            c_slabs(c_re).astype(_bf16), c_slabs(c_im).astype(_bf16))


def _s5_scan_kernel(u_ref, b_ref, cre_ref, cim_ref, are_ref, aim_ref, x0re_ref, x0im_ref,
                    y_ref, xfre_ref, xfim_ref, bu_s, zre_s, zim_s, *, tb_steps, rd):
    tb = pl.program_id(2)
    n_rg = rd // SUBLANES
    st = S5_SLAB_ST

    @pl.when(tb == 0)
    def _():
        zre_s[...] = x0re_ref[...]
        zim_s[...] = x0im_ref[...]

    u = u_ref[...].reshape(tb_steps * rd, S5_SLAB_CH).astype(_bf16)
    bu_s[...] = jnp.dot(u, b_ref[...], preferred_element_type=_f32).reshape(tb_steps, rd, 2 * st)
    are = jnp.broadcast_to(are_ref[...], (SUBLANES, st))
    aim = jnp.broadcast_to(aim_ref[...], (SUBLANES, st))

    def body(t, carry):
        new = []
        for rg in range(n_rg):
            zre, zim = carry[rg]
            rows = slice(rg * SUBLANES, (rg + 1) * SUBLANES)
            nre = are * zre - aim * zim + bu_s[t, rows, 0:st]
            nim = are * zim + aim * zre + bu_s[t, rows, st:2 * st]
            bu_s[t, rows, 0:st] = nre
            bu_s[t, rows, st:2 * st] = nim
            new.append((nre, nim))
        return tuple(new)

    init = tuple((zre_s[rg * SUBLANES:(rg + 1) * SUBLANES, :], zim_s[rg * SUBLANES:(rg + 1) * SUBLANES, :])
                 for rg in range(n_rg))
    fin = lax.fori_loop(0, tb_steps, body, init, unroll=4)
    for rg in range(n_rg):
        zre_s[rg * SUBLANES:(rg + 1) * SUBLANES, :] = fin[rg][0]
        zim_s[rg * SUBLANES:(rg + 1) * SUBLANES, :] = fin[rg][1]

    xs = bu_s[...].reshape(tb_steps * rd, 2 * st)
    y = (jnp.dot(xs[:, 0:st].astype(_bf16), cre_ref[...], preferred_element_type=_f32)
         - jnp.dot(xs[:, st:2 * st].astype(_bf16), cim_ref[...], preferred_element_type=_f32))
    y_ref[...] = y.reshape(tb_steps, rd, S5_SLAB_CH)

    @pl.when(tb == pl.num_programs(2) - 1)
    def _():
        xfre_ref[...] = zre_s[...]
        xfim_ref[...] = zim_s[...]


def _s5_scan(u, params, x0_re, x0_im):
    abre, abim, b_blk, c_re, c_im = params
    _, t_len, rd, _ = u.shape
    tb_steps = S5_BLOCK_ROWS // rd
    n_tb = t_len // tb_steps
    st = S5_SLAB_ST
    grid = (2, S5_NSLAB, n_tb)
    kern = functools.partial(_s5_scan_kernel, tb_steps=tb_steps, rd=rd)
    return pl.pallas_call(
        kern,
        grid=grid,
        in_specs=[
            pl.BlockSpec((None, tb_steps, rd, S5_SLAB_CH), lambda d, s, t: (d, t, 0, s)),
            pl.BlockSpec((None, None, S5_SLAB_CH, 2 * st), lambda d, s, t: (d, s, 0, 0)),
            pl.BlockSpec((None, st, S5_SLAB_CH), lambda d, s, t: (s, 0, 0)),
            pl.BlockSpec((None, st, S5_SLAB_CH), lambda d, s, t: (s, 0, 0)),
            pl.BlockSpec((None, 1, st), lambda d, s, t: (d, 0, s)),
            pl.BlockSpec((None, 1, st), lambda d, s, t: (d, 0, s)),
            pl.BlockSpec((None, rd, st), lambda d, s, t: (d, 0, s)),
            pl.BlockSpec((None, rd, st), lambda d, s, t: (d, 0, s)),
        ],
        out_specs=[
            pl.BlockSpec((None, tb_steps, rd, S5_SLAB_CH), lambda d, s, t: (d, t, 0, s)),
            pl.BlockSpec((None, rd, st), lambda d, s, t: (d, 0, s)),
            pl.BlockSpec((None, rd, st), lambda d, s, t: (d, 0, s)),
        ],
        out_shape=[
            jax.ShapeDtypeStruct((2, t_len, rd, S5_W), _f32),
            jax.ShapeDtypeStruct((2, rd, S5_GROUPS * S5_STATE), _f32),
            jax.ShapeDtypeStruct((2, rd, S5_GROUPS * S5_STATE), _f32),
        ],
        scratch_shapes=[
            pltpu.VMEM((tb_steps, rd, 2 * st), _f32),
            pltpu.VMEM((rd, st), _f32),
            pltpu.VMEM((rd, st), _f32),
        ],
        compiler_params=pltpu.CompilerParams(dimension_semantics=("arbitrary", "arbitrary", "arbitrary"),
                                             vmem_limit_bytes=VMEM_LIMIT),
        name="s5_scan",
    )(u, b_blk, c_re, c_im, abre, abim, x0_re, x0_im)


def _gelu_tanh(x):
    return 0.5 * x * (1.0 + jnp.tanh(math.sqrt(2.0 / math.pi) * (x + 0.044715 * (x * x * x))))


def _s5_out_kernel(yf_ref, yb_ref, u_ref, d_ref, w_ref, b_ref, o_ref, w_s):
    @pl.when(pl.program_id(0) == 0)
    def _():
        w_s[...] = w_ref[...].astype(_bf16)

    y = _gelu_tanh(yf_ref[...] + yb_ref[...] + d_ref[...] * u_ref[...])
    z = jnp.dot(y.astype(_bf16), w_s[...], preferred_element_type=_f32) + b_ref[...]
    o_ref[...] = y * jax.nn.sigmoid(z)


def _s5_out(yf, yb, u, d_vec, w_glu, b_glu, layer):
    nt = N_TOK // ROW_TILE
    row = lambda i: (i, 0)
    lay = lambda i: (layer, 0, 0)
    return pl.pallas_call(
        _s5_out_kernel,
        grid=(nt,),
        in_specs=[
            pl.BlockSpec((ROW_TILE, S5_W), row),
            pl.BlockSpec((ROW_TILE, S5_W), row),
            pl.BlockSpec((ROW_TILE, S5_W), row),
            pl.BlockSpec((None, 1, S5_W), lay),
            pl.BlockSpec((None, S5_W, S5_W), lay),
            pl.BlockSpec((None, 1, S5_W), lay),
        ],
        out_specs=pl.BlockSpec((ROW_TILE, S5_W), row),
        out_shape=jax.ShapeDtypeStruct((N_TOK, S5_W), _f32),
        scratch_shapes=[pltpu.VMEM((S5_W, S5_W), _bf16)],
        compiler_params=pltpu.CompilerParams(dimension_semantics=("arbitrary",), vmem_limit_bytes=VMEM_LIMIT),
        name="s5_out",
    )(yf, yb, u, d_vec.reshape(DEPTH, 1, S5_W), w_glu, b_glu.reshape(DEPTH, 1, S5_W))


def _s5_branch(su_c, su_l, sp, s5_re0, s5_im0, layer):
    params = _s5_params(sp['s5_a_re'][layer], sp['s5_a_im'][layer], sp['s5_log_dt'][layer], sp['s5_b_re'][layer],
                        sp['s5_b_im'][layer], sp['s5_c_re'][layer], sp['s5_c_im'][layer])
    nst = S5_GROUPS * S5_STATE

    def both_dirs(u_tm):
        return jnp.stack([u_tm, u_tm[::-1]], axis=0)

    def merge_dirs(y):
        return y[0].transpose(1, 0, 2), y[1, ::-1].transpose(1, 0, 2)

    zeros = jnp.zeros((2, BATCH, nst), _f32)
    y_c, xf_re, xf_im = _s5_scan(both_dirs(su_c.transpose(1, 0, 2)), params, zeros, zeros)
    yf_c, yb_c = merge_dirs(y_c)
    pad = ((0, 0), (0, SUBLANES - DEC_BATCH), (0, 0))
    u_l = both_dirs(jnp.pad(su_l.transpose(1, 0, 2), pad))

    def lat_state(x0):
        return jnp.pad(x0.transpose(1, 0, 2, 3).reshape(2, DEC_BATCH, nst), pad)

    y_l, _, _ = _s5_scan(u_l, params, lat_state(s5_re0), lat_state(s5_im0))
    yf_l, yb_l = merge_dirs(y_l[:, :, :DEC_BATCH])
    yf = jnp.concatenate([yf_c.reshape(N_CTX, S5_W), yf_l.reshape(N_LAT, S5_W)], axis=0)
    yb = jnp.concatenate([yb_c.reshape(N_CTX, S5_W), yb_l.reshape(N_LAT, S5_W)], axis=0)
    u = jnp.concatenate([su_c.reshape(N_CTX, S5_W), su_l.reshape(N_LAT, S5_W)], axis=0)
    s_out = _s5_out(yf, yb, u, sp['s5_d'], sp['s5_w_glu'], sp['s5_b_glu'], layer)

    def ctx_state(xf):
        return xf.reshape(2, BATCH, S5_GROUPS, S5_STATE).transpose(1, 0, 2, 3)

    return s_out, ctx_state(xf_re), ctx_state(xf_im)


def _merge_branches(a_out, h_out, s_out, ga, gh, gs, p):
    y = (jax.nn.sigmoid(ga) * (a_out @ p['w_br_attn'])
         + jax.nn.sigmoid(gh) * (h_out @ p['w_br_hg'])
         + jax.nn.sigmoid(gs) * (s_out @ p['w_br_s5']))
    return y @ p['w_out']


def _mixer_sublayer(xc, xl, mods, pl_, sp, lb, ck, cv, s_hg0, s5_re0, s5_im0, layer):
    sh1, sc1, gt1 = mods[0], mods[1], mods[2]

    def pre(x, sh, sc):
        h = _rms_norm(x, pl_['norm1_g']) * (1.0 + sc) + sh
        return _split_in(h @ pl_['w_in'])

    zc = pre(xc, sh1[0], sc1[0])
    zl = pre(xl, sh1[1:], sc1[1:])
    s_out, s5_re, s5_im = _s5_branch(zc[8], zl[8], sp, s5_re0, s5_im0, layer)
    s_out_c = s_out[:N_CTX].reshape(BATCH, SEQ, S5_W)
    s_out_l = s_out[N_CTX:].reshape(DEC_BATCH, DEC_SEQ, S5_W)
    q, k, v, hq, hff, hfb, hi, hg, _, ga, gh, gs = zc
    q, k, v = _attn_heads(q, k, v, pl_['q_norm_g'], pl_['k_norm_g'])
    a_out = _context_attention(q, k, v, pl_['sink'])
    z_hg = jnp.zeros((BATCH, 2, HG_HEADS, HG_DK, HG_DV), _f32)
    h_out, s_hg = _hgrn2_bidir(hq, hff, hfb, hi, hg, lb, pl_['hg_norm_g'], z_hg)
    xc = xc + gt1[0] * _merge_branches(a_out, h_out, s_out_c, ga, gh, gs, pl_)
    new = (k, v, s_hg, s5_re, s5_im)
    q, k, v, hq, hff, hfb, hi, hg, _, ga, gh, gs = zl
    q, k, v = _attn_heads(q, k, v, pl_['q_norm_g'], pl_['k_norm_g'])
    a_out = _window_attention(_axial_rope(q), _axial_rope(k), v, ck, cv, pl_['sink'])
    h_out, _ = _hgrn2_bidir(hq, hff, hfb, hi, hg, lb, pl_['hg_norm_g'], s_hg0)
    xl = xl + gt1[1:] * _merge_branches(a_out, h_out, s_out_l, ga, gh, gs, pl_)
    return xc, xl, new


def kernel(x_prompt, x_sample, cache_k, cache_v, state_hgrn, state_s5_re, state_s5_im, c, c_ctx, w_mod, b_mod, norm1_g, norm2_g, w_in, q_norm_g, k_norm_g, attn_sink, hg_lb_logits, hg_norm_g, s5_a_re, s5_a_im, s5_log_dt, s5_b_re, s5_b_im, s5_c_re, s5_c_im, s5_d, s5_w_glu, s5_b_glu, w_br_attn, w_br_hg, w_br_s5, w_out, router_w, router_b, exp_w1, exp_b1, exp_w2, exp_b2):
    lb_all = jnp.cumsum(jax.nn.softmax(hg_lb_logits.astype(_f32), axis=0), axis=0)
    lb_all = lb_all - lb_all[:1]
    cond = jnp.concatenate([c_ctx[None, :], c], axis=0)
    xc, xl = x_prompt, x_sample
    new_k, new_v, new_hg, new_re, new_im = [], [], [], [], []
    moe_p = {"norm2_g": norm2_g, "router_w": router_w, "router_b": router_b, "exp_w1": exp_w1, "exp_b1": exp_b1,
             "exp_w2": exp_w2, "exp_b2": exp_b2}
    s5_p = {'s5_a_re': s5_a_re, 's5_a_im': s5_a_im, 's5_log_dt': s5_log_dt, 's5_b_re': s5_b_re, 's5_b_im': s5_b_im,
            's5_c_re': s5_c_re, 's5_c_im': s5_c_im, 's5_d': s5_d, 's5_w_glu': s5_w_glu, 's5_b_glu': s5_b_glu}
    for l in range(DEPTH):
        p = {
            'norm1_g': norm1_g[l], 'w_in': w_in[l], 'q_norm_g': q_norm_g[l], 'k_norm_g': k_norm_g[l],
            'sink': attn_sink[l], 'hg_norm_g': hg_norm_g[l],
            'w_br_attn': w_br_attn[l], 'w_br_hg': w_br_hg[l], 'w_br_s5': w_br_s5[l], 'w_out': w_out[l],
        }
        m = jax.nn.silu(cond) @ w_mod[l] + b_mod[l]
        mods = [a[:, None, :] for a in jnp.split(m, N_MOD, axis=-1)]
        xc, xl, (k_l, v_l, hg_l, re_l, im_l) = _mixer_sublayer(
            xc, xl, mods, p, s5_p, lb_all[l], cache_k[:, l], cache_v[:, l], state_hgrn[:, l], state_s5_re[:, l],
            state_s5_im[:, l], l)
        new_k.append(k_l)
        new_v.append(v_l)
        new_hg.append(hg_l)
        new_re.append(re_l)
        new_im.append(im_l)
        x = jnp.concatenate([xc.reshape(N_CTX, D_MODEL), xl.reshape(N_LAT, D_MODEL)], axis=0)
        x = _moe_sublayer(x, mods, moe_p, l)
        xc = x[:N_CTX].reshape(BATCH, SEQ, D_MODEL)
        xl = x[N_CTX:].reshape(DEC_BATCH, DEC_SEQ, D_MODEL)
    return (xc, xl, jnp.stack(new_k, axis=1), jnp.stack(new_v, axis=1), jnp.stack(new_hg, axis=1),
            jnp.stack(new_re, axis=1), jnp.stack(new_im, axis=1))
```

```python
import functools
import math

import jax
import jax.numpy as jnp
import numpy as np
from jax import lax
from jax.experimental import pallas as pl
from jax.experimental.pallas import tpu as pltpu

D_MODEL = 2048
BATCH = 16
SEQ = 256
DEPTH = 2
DEC_BATCH = 2
DEC_SEQ = 2048
GRID_W = 64
N_HEADS = 8
N_KV_HEADS = 2
HEAD_DIM = 128
Q_GROUP = N_HEADS // N_KV_HEADS
WINDOW = 128
ATTN_BLOCK = 128
ROPE_BASE = 10000.0
ATT_Q_W = N_HEADS * HEAD_DIM
ATT_KV_W = N_KV_HEADS * HEAD_DIM
HG_HEADS = 8
HG_DK = 128
HG_DV = 128
HG_W = HG_HEADS * HG_DK
HG_CHUNK = 16
S5_W = 1024
S5_CH = 16
S5_GROUPS = S5_W // S5_CH
S5_STATE = 64
N_EXPERTS = 32
TOP_K = 4
D_FF = 2048
SWIGLU_ALPHA = 1.702
SWIGLU_LIMIT = 7.0
N_MOD = 6
EPS = 1e-6
IN_SIZES = (ATT_Q_W, ATT_KV_W, ATT_KV_W, HG_W, HG_W, HG_W, HG_W, HG_W, S5_W, D_MODEL, D_MODEL, D_MODEL)
IN_COLS = sum(IN_SIZES)

N_CTX = BATCH * SEQ
N_LAT = DEC_BATCH * DEC_SEQ
N_TOK = N_CTX + N_LAT
N_GROUPS = 1 + DEC_BATCH

LANES = 128
ROW_TILE = 256
VMEM_LIMIT = 56 * 1024 * 1024

MOE_SUPER = 1024
MOE_SUB = 256
MOE_FC = 256
MOE_NC = D_FF // MOE_FC
MOE_NS = (N_TOK * TOP_K) // MOE_SUPER + N_EXPERTS
MOD_TN = 1536
PROJ_TN = 1536
PROJ_TM = 512
MERGE_TN = 512
HG_OUT_W = 512
DISPATCH_TILE = 256
COMBINE_TILE = 64

S5_SLAB_GROUPS = 8
S5_NSLAB = S5_GROUPS // S5_SLAB_GROUPS
S5_SLAB_CH = S5_SLAB_GROUPS * S5_CH
S5_SLAB_ST = S5_SLAB_GROUPS * S5_STATE
S5_BLOCK_ROWS = 1024
SUBLANES = 8

_bf16 = jnp.bfloat16
_f32 = jnp.float32


def _group_of_tile(i):
    ctx_tiles = N_CTX // ROW_TILE
    lat_tiles = DEC_SEQ // ROW_TILE
    return jnp.where(i < ctx_tiles, 0, 1 + (i - ctx_tiles) // lat_tiles)


def _split_bf16(a):
    hi = a.astype(_bf16)
    lo = (a - hi.astype(_f32)).astype(_bf16)
    return hi, lo


def _router_kernel(x_ref, g_ref, sc_ref, sh_ref, rw_ref, rb_ref, h_ref, idx_ref, w_ref):
    x = x_ref[...]
    y = x * lax.rsqrt(jnp.mean(x * x, axis=-1, keepdims=True) + EPS) * g_ref[...]
    h = y * (1.0 + sc_ref[...]) + sh_ref[...]
    h_ref[...] = h.astype(_bf16)
    h_hi, h_lo = _split_bf16(h)
    r_hi, r_lo = _split_bf16(rw_ref[...])
    logits = (jnp.dot(h_hi, r_hi, preferred_element_type=_f32)
              + jnp.dot(h_hi, r_lo, preferred_element_type=_f32)
              + jnp.dot(h_lo, r_hi, preferred_element_type=_f32)) + rb_ref[...]
    lane = lax.broadcasted_iota(jnp.int32, logits.shape, 1)
    work = logits
    vals, idxs = [], []
    for _ in range(TOP_K):
        m = jnp.max(work, axis=-1, keepdims=True)
        i = jnp.min(jnp.where(work == m, lane, N_EXPERTS), axis=-1, keepdims=True)
        vals.append(m)
        idxs.append(i)
        work = jnp.where(lane == i, -jnp.inf, work)
    es = [jnp.exp(v - vals[0]) for v in vals]
    den = es[0] + es[1] + es[2] + es[3]
    out_lane = lax.broadcasted_iota(jnp.int32, idx_ref.shape, 1)
    idx_out = jnp.zeros(idx_ref.shape, jnp.int32)
    w_out = jnp.zeros(w_ref.shape, _f32)
    for k in range(TOP_K):
        idx_out = jnp.where(out_lane == k, idxs[k], idx_out)
        w_out = jnp.where(out_lane == k, es[k] / den, w_out)
    idx_ref[...] = idx_out
    w_ref[...] = w_out


def _router(x, norm_g, sc, sh, router_w, router_b, layer):
    nt = N_TOK // ROW_TILE
    row = lambda i: (i, 0)
    grp = lambda i: (_group_of_tile(i), 0, 0)
    lay = lambda i: (layer, 0, 0)
    h, idx, w = pl.pallas_call(
        _router_kernel,
        grid=(nt,),
        in_specs=[
            pl.BlockSpec((ROW_TILE, D_MODEL), row),
            pl.BlockSpec((None, 1, D_MODEL), lay),
            pl.BlockSpec((None, 1, D_MODEL), grp),
            pl.BlockSpec((None, 1, D_MODEL), grp),
            pl.BlockSpec((None, D_MODEL, N_EXPERTS), lay),
            pl.BlockSpec((None, 1, N_EXPERTS), lay),
        ],
        out_specs=[
            pl.BlockSpec((ROW_TILE, D_MODEL), row),
            pl.BlockSpec((ROW_TILE, LANES), row),
            pl.BlockSpec((ROW_TILE, LANES), row),
        ],
        out_shape=[
            jax.ShapeDtypeStruct((N_TOK, D_MODEL), _bf16),
            jax.ShapeDtypeStruct((N_TOK, LANES), jnp.int32),
            jax.ShapeDtypeStruct((N_TOK, LANES), _f32),
        ],
        compiler_params=pltpu.CompilerParams(dimension_semantics=("arbitrary",), vmem_limit_bytes=VMEM_LIMIT),
        name="router",
    )(x, norm_g.reshape(DEPTH, 1, D_MODEL), sc, sh, router_w, router_b.reshape(DEPTH, 1, N_EXPERTS))
    return h, idx[:, :TOP_K], w[:, :TOP_K]


def _moe_kernel(exp_ref, rows_ref, used_ref, x_ref, rw_ref, w1g_ref, w1u_ref, b1g_ref, b1u_ref, w2_ref, b2_ref,
                o_ref, w1g_s, w1u_s, w2_s):
    s = pl.program_id(0)
    c = pl.program_id(1)
    nvalid = rows_ref[s]

    @pl.when(c == 0)
    def _():
        o_ref[...] = jnp.zeros(o_ref.shape, _f32)

    @pl.when(nvalid > 0)
    def _():
        w1g_s[...] = w1g_ref[...].astype(_bf16)
        w1u_s[...] = w1u_ref[...].astype(_bf16)
        w2_s[...] = w2_ref[...].astype(_bf16)
        for r in range(MOE_SUPER // MOE_SUB):
            @pl.when(r * MOE_SUB < nvalid)
            def _():
                rows = pl.ds(r * MOE_SUB, MOE_SUB)
                xs = x_ref[rows, :]
                zg = jnp.dot(xs, w1g_s[...], preferred_element_type=_f32) + b1g_ref[...]
                zu = jnp.dot(xs, w1u_s[...], preferred_element_type=_f32) + b1u_ref[...]
                g = jnp.minimum(zg, SWIGLU_LIMIT)
                u = jnp.clip(zu, -SWIGLU_LIMIT, SWIGLU_LIMIT)
                act = g * jax.nn.sigmoid(SWIGLU_ALPHA * g) * (u + 1.0)
                o_ref[rows, :] += jnp.dot(act.astype(_bf16), w2_s[...], preferred_element_type=_f32)

        @pl.when(c == MOE_NC - 1)
        def _():
            o_ref[...] = rw_ref[...] * (o_ref[...] + b2_ref[...])


def _moe_ffn(x_sorted, row_w, st_expert, st_rows, n_used, w1, b1, w2, b2, layer):
    def xmap(s, c, e_ref, r_ref, u_ref):
        return (jnp.minimum(s, u_ref[0] - 1), 0)

    def chunk(s, c, u_ref):
        return jnp.where(s < u_ref[0], c, MOE_NC - 1)

    def w1g_map(s, c, e_ref, r_ref, u_ref):
        return (layer, e_ref[s], 0, chunk(s, c, u_ref))

    def w1u_map(s, c, e_ref, r_ref, u_ref):
        return (layer, e_ref[s], 0, MOE_NC + chunk(s, c, u_ref))

    def w2_map(s, c, e_ref, r_ref, u_ref):
        return (layer, e_ref[s], chunk(s, c, u_ref), 0)

    def b2_map(s, c, e_ref, r_ref, u_ref):
        return (layer, e_ref[s], 0, 0)

    grid_spec = pltpu.PrefetchScalarGridSpec(
        num_scalar_prefetch=3,
        grid=(MOE_NS, MOE_NC),
        in_specs=[
            pl.BlockSpec((MOE_SUPER, D_MODEL), xmap),
            pl.BlockSpec((MOE_SUPER, 1), xmap),
            pl.BlockSpec((None, None, D_MODEL, MOE_FC), w1g_map),
            pl.BlockSpec((None, None, D_MODEL, MOE_FC), w1u_map),
            pl.BlockSpec((None, None, 1, MOE_FC), w1g_map),
            pl.BlockSpec((None, None, 1, MOE_FC), w1u_map),
            pl.BlockSpec((None, None, MOE_FC, D_MODEL), w2_map),
            pl.BlockSpec((None, None, 1, D_MODEL), b2_map),
        ],
        out_specs=pl.BlockSpec((MOE_SUPER, D_MODEL), lambda s, c, e_ref, r_ref, u_ref: (s, 0)),
        scratch_shapes=[
            pltpu.VMEM((D_MODEL, MOE_FC), _bf16),
            pltpu.VMEM((D_MODEL, MOE_FC), _bf16),
            pltpu.VMEM((MOE_FC, D_MODEL), _bf16),
        ],
    )
    return pl.pallas_call(
        _moe_kernel,
        grid_spec=grid_spec,
        out_shape=jax.ShapeDtypeStruct((MOE_NS * MOE_SUPER, D_MODEL), _f32),
        compiler_params=pltpu.CompilerParams(dimension_semantics=("arbitrary", "arbitrary"),
                                             vmem_limit_bytes=VMEM_LIMIT),
        name="moe_ffn",
    )(st_expert, st_rows, n_used, x_sorted, row_w, w1, w1,
      b1.reshape(DEPTH, N_EXPERTS, 1, 2 * D_FF), b1.reshape(DEPTH, N_EXPERTS, 1, 2 * D_FF),
      w2, b2.reshape(DEPTH, N_EXPERTS, 1, D_MODEL))


def _route_metadata(top_idx, top_w):
    n = N_TOK * TOP_K
    e_flat = top_idx.reshape(n)
    order = jnp.argsort(e_flat, stable=True).astype(jnp.int32)
    e_sorted = e_flat[order]
    counts = jnp.zeros((N_EXPERTS,), jnp.int32).at[e_flat].add(1)
    n_st = (counts + MOE_SUPER - 1) // MOE_SUPER
    st_end = jnp.cumsum(n_st)
    st_start = st_end - n_st
    grp_start = jnp.cumsum(counts) - counts
    rank = jnp.arange(n, dtype=jnp.int32) - grp_start[e_sorted]
    dest = st_start[e_sorted] * MOE_SUPER + rank
    n_used = st_end[-1]
    s_ids = jnp.arange(MOE_NS, dtype=jnp.int32)
    s_clamped = jnp.minimum(s_ids, n_used - 1)
    st_expert = jnp.searchsorted(st_end, s_clamped, side="right").astype(jnp.int32)
    st_rows = jnp.clip(counts[st_expert] - (s_clamped - st_start[st_expert]) * MOE_SUPER, 0, MOE_SUPER)
    st_rows = jnp.where(s_ids < n_used, st_rows, 0).astype(jnp.int32)
    src_tok = jnp.zeros((MOE_NS * MOE_SUPER,), jnp.int32).at[dest].set(order // TOP_K)
    row_w = jnp.zeros((MOE_NS * MOE_SUPER,), _f32).at[dest].set(top_w.reshape(n)[order])
    pos = jnp.zeros((n,), jnp.int32).at[order].set(dest)
    return src_tok, row_w.reshape(-1, 1), st_expert, st_rows, n_used.reshape(1).astype(jnp.int32), pos.reshape(N_TOK, TOP_K)


def _issue_row_copies(idx_ref, src_hbm, dst_ref, sem, n):
    def start(r, carry):
        pltpu.make_async_copy(src_hbm.at[idx_ref[0, r]], dst_ref.at[r], sem).start()
        return carry

    def wait(r, carry):
        pltpu.make_async_copy(src_hbm.at[0], dst_ref.at[r], sem).wait()
        return carry

    lax.fori_loop(0, n, start, 0)
    lax.fori_loop(0, n, wait, 0)


def _dispatch_kernel(used_ref, idx_ref, src_hbm, o_ref, sem):
    i = pl.program_id(0)

    @pl.when(i < used_ref[0])
    def _():
        _issue_row_copies(idx_ref, src_hbm, o_ref, sem, DISPATCH_TILE)

    @pl.when(i >= used_ref[0])
    def _():
        o_ref[...] = jnp.zeros(o_ref.shape, o_ref.dtype)


def _dispatch(h_words, src_tok, n_used_tiles):
    n_rows = src_tok.shape[0]
    nt = n_rows // DISPATCH_TILE
    width = h_words.shape[1]
    grid_spec = pltpu.PrefetchScalarGridSpec(
        num_scalar_prefetch=1,
        grid=(nt,),
        in_specs=[
            pl.BlockSpec((None, 1, DISPATCH_TILE), lambda i, u: (i, 0, 0), memory_space=pltpu.SMEM),
            pl.BlockSpec(memory_space=pl.ANY),
        ],
        out_specs=pl.BlockSpec((DISPATCH_TILE, width), lambda i, u: (i, 0)),
        scratch_shapes=[pltpu.SemaphoreType.DMA(())],
    )
    return pl.pallas_call(
        _dispatch_kernel,
        grid_spec=grid_spec,
        out_shape=jax.ShapeDtypeStruct((n_rows, width), h_words.dtype),
        compiler_params=pltpu.CompilerParams(dimension_semantics=("arbitrary",), vmem_limit_bytes=VMEM_LIMIT),
        name="moe_dispatch",
    )(n_used_tiles, src_tok.reshape(nt, 1, DISPATCH_TILE), h_words)


def _combine_kernel(pos_ref, y_hbm, x_ref, gt_ref, o_ref, buf, sem):
    _issue_row_copies(pos_ref, y_hbm, buf, sem, TOP_K * COMBINE_TILE)
    y = buf[0:COMBINE_TILE, :]
    for k in range(1, TOP_K):
        y = y + buf[k * COMBINE_TILE:(k + 1) * COMBINE_TILE, :]
    o_ref[...] = x_ref[...] + gt_ref[...] * y


def _combine(y_sorted, pos, x, gt):
    nt = N_TOK // COMBINE_TILE
    tiles_per_row_tile = ROW_TILE // COMBINE_TILE
    pos_tiles = pos.reshape(nt, COMBINE_TILE, TOP_K).transpose(0, 2, 1).reshape(nt, 1, TOP_K * COMBINE_TILE)
    return pl.pallas_call(
        _combine_kernel,
        grid=(nt,),
        in_specs=[
            pl.BlockSpec((None, 1, TOP_K * COMBINE_TILE), lambda i: (i, 0, 0), memory_space=pltpu.SMEM),
            pl.BlockSpec(memory_space=pl.ANY),
            pl.BlockSpec((COMBINE_TILE, D_MODEL), lambda i: (i, 0)),
            pl.BlockSpec((None, 1, D_MODEL), lambda i: (_group_of_tile(i // tiles_per_row_tile), 0, 0)),
        ],
        out_specs=pl.BlockSpec((COMBINE_TILE, D_MODEL), lambda i: (i, 0)),
        out_shape=jax.ShapeDtypeStruct((N_TOK, D_MODEL), _f32),
        scratch_shapes=[pltpu.VMEM((TOP_K * COMBINE_TILE, D_MODEL), _f32), pltpu.SemaphoreType.DMA(())],
        compiler_params=pltpu.CompilerParams(dimension_semantics=("arbitrary",), vmem_limit_bytes=VMEM_LIMIT),
        name="moe_combine",
    )(pos_tiles, y_sorted, x, gt)


def _moe_sublayer(x, mods, p, layer):
    sh2, sc2, gt2 = mods[3], mods[4], mods[5]
    h, top_idx, top_w = _router(x, p["norm2_g"], sc2, sh2, p["router_w"], p["router_b"], layer)
    src_tok, row_w, st_expert, st_rows, n_used, pos = _route_metadata(top_idx, top_w)
    h_words = lax.bitcast_convert_type(h.reshape(N_TOK, D_MODEL // 2, 2), jnp.uint32)
    x_words = _dispatch(h_words, src_tok, n_used * (MOE_SUPER // DISPATCH_TILE))
    x_sorted = lax.bitcast_convert_type(x_words, _bf16).reshape(MOE_NS * MOE_SUPER, D_MODEL)
    y_sorted = _moe_ffn(x_sorted, row_w, st_expert, st_rows, n_used, p["exp_w1"], p["exp_b1"], p["exp_w2"],
                        p["exp_b2"], layer)
    return _combine(y_sorted, pos, x, gt2)


def _group_of_rows(i, tile):
    return jnp.where(i < N_CTX // tile, 0, 1 + (i - N_CTX // tile) // (DEC_SEQ // tile))


def _mod_kernel(c_ref, w_ref, b_ref, o_ref):
    c = c_ref[...]
    a = (c * jax.nn.sigmoid(c)).astype(_bf16)
    o_ref[...] = jnp.dot(a, w_ref[...].astype(_bf16), preferred_element_type=_f32) + b_ref[...]


def _modulation(cond_rows, w_mod, b_mod, layer):
    n = N_MOD * D_MODEL
    return pl.pallas_call(
        _mod_kernel,
        grid=(n // MOD_TN,),
        in_specs=[
            pl.BlockSpec((SUBLANES, D_MODEL), lambda j: (0, 0)),
            pl.BlockSpec((None, D_MODEL, MOD_TN), lambda j: (layer, 0, j)),
            pl.BlockSpec((None, 1, MOD_TN), lambda j: (layer, 0, j)),
        ],
        out_specs=pl.BlockSpec((SUBLANES, MOD_TN), lambda j: (0, j)),
        out_shape=jax.ShapeDtypeStruct((SUBLANES, n), _f32),
        compiler_params=pltpu.CompilerParams(dimension_semantics=("arbitrary",), vmem_limit_bytes=VMEM_LIMIT),
        name="modulation",
    )(cond_rows, w_mod, b_mod.reshape(DEPTH, 1, n))


def _norm_mod_kernel(x_ref, g_ref, sc_ref, sh_ref, h_ref):
    x = x_ref[...]
    y = x * lax.rsqrt(jnp.mean(x * x, axis=-1, keepdims=True) + EPS) * g_ref[...]
    h_ref[...] = (y * (1.0 + sc_ref[...]) + sh_ref[...]).astype(_bf16)


def _norm_mod(x, norm_g, sc, sh, layer):
    row = lambda i: (i, 0)
    grp = lambda i: (_group_of_tile(i), 0, 0)
    lay = lambda i: (layer, 0, 0)
    return pl.pallas_call(
        _norm_mod_kernel,
        grid=(N_TOK // ROW_TILE,),
        in_specs=[
            pl.BlockSpec((ROW_TILE, D_MODEL), row),
            pl.BlockSpec((None, 1, D_MODEL), lay),
            pl.BlockSpec((None, 1, D_MODEL), grp),
            pl.BlockSpec((None, 1, D_MODEL), grp),
        ],
        out_specs=pl.BlockSpec((ROW_TILE, D_MODEL), row),
        out_shape=jax.ShapeDtypeStruct((N_TOK, D_MODEL), _bf16),
        compiler_params=pltpu.CompilerParams(dimension_semantics=("arbitrary",), vmem_limit_bytes=VMEM_LIMIT),
        name="norm_mod",
    )(x, norm_g.reshape(DEPTH, 1, D_MODEL), sc, sh)


def _in_proj_kernel(h_ref, w_ref, o_ref, w_s):
    @pl.when(pl.program_id(1) == 0)
    def _():
        w_s[...] = w_ref[...].astype(_bf16)

    o_ref[...] = jnp.dot(h_ref[...], w_s[...], preferred_element_type=_f32)


def _in_proj(h, w_in, layer):
    return pl.pallas_call(
        _in_proj_kernel,
        grid=(IN_COLS // PROJ_TN, N_TOK // PROJ_TM),
        in_specs=[
            pl.BlockSpec((PROJ_TM, D_MODEL), lambda j, i: (i, 0)),
            pl.BlockSpec((None, D_MODEL, PROJ_TN), lambda j, i: (layer, 0, j)),
        ],
        out_specs=pl.BlockSpec((PROJ_TM, PROJ_TN), lambda j, i: (i, j)),
        out_shape=jax.ShapeDtypeStruct((N_TOK, IN_COLS), _f32),
        scratch_shapes=[pltpu.VMEM((D_MODEL, PROJ_TN), _bf16)],
        compiler_params=pltpu.CompilerParams(dimension_semantics=("arbitrary", "arbitrary"),
                                             vmem_limit_bytes=VMEM_LIMIT),
        name="in_proj",
    )(h, w_in)


def _merge_kernel(a_ref, h_ref, s_ref, ga_ref, gh_ref, gs_ref, wa_ref, wh_ref, ws_ref, o_ref, wa_s, wh_s, ws_s):
    @pl.when(pl.program_id(1) == 0)
    def _():
        wa_s[...] = wa_ref[...].astype(_bf16)
        wh_s[...] = wh_ref[...].astype(_bf16)
        ws_s[...] = ws_ref[...].astype(_bf16)

    y = (jax.nn.sigmoid(ga_ref[...]) * jnp.dot(a_ref[...], wa_s[...], preferred_element_type=_f32)
         + jax.nn.sigmoid(gh_ref[...]) * jnp.dot(h_ref[...], wh_s[...], preferred_element_type=_f32)
         + jax.nn.sigmoid(gs_ref[...]) * jnp.dot(s_ref[...], ws_s[...], preferred_element_type=_f32))
    o_ref[...] = y.astype(_bf16)


def _merge(a_out, h_out, s_out, z, w_a, w_h, w_s, layer):
    ga0, gh0, gs0 = (sum(IN_SIZES[:k]) // MERGE_TN for k in (9, 10, 11))
    act = lambda j, i: (i, 0)
    wmap = lambda j, i: (layer, 0, j)
    return pl.pallas_call(
        _merge_kernel,
        grid=(D_MODEL // MERGE_TN, N_TOK // ROW_TILE),
        in_specs=[
            pl.BlockSpec((ROW_TILE, ATT_Q_W), act),
            pl.BlockSpec((ROW_TILE, HG_W), act),
            pl.BlockSpec((ROW_TILE, S5_W), act),
            pl.BlockSpec((ROW_TILE, MERGE_TN), lambda j, i: (i, ga0 + j)),
            pl.BlockSpec((ROW_TILE, MERGE_TN), lambda j, i: (i, gh0 + j)),
            pl.BlockSpec((ROW_TILE, MERGE_TN), lambda j, i: (i, gs0 + j)),
            pl.BlockSpec((None, ATT_Q_W, MERGE_TN), wmap),
            pl.BlockSpec((None, HG_W, MERGE_TN), wmap),
            pl.BlockSpec((None, S5_W, MERGE_TN), wmap),
        ],
        out_specs=pl.BlockSpec((ROW_TILE, MERGE_TN), lambda j, i: (i, j)),
        out_shape=jax.ShapeDtypeStruct((N_TOK, D_MODEL), _bf16),
        scratch_shapes=[pltpu.VMEM((ATT_Q_W, MERGE_TN), _bf16), pltpu.VMEM((HG_W, MERGE_TN), _bf16),
                        pltpu.VMEM((S5_W, MERGE_TN), _bf16)],
        compiler_params=pltpu.CompilerParams(dimension_semantics=("arbitrary", "arbitrary"),
                                             vmem_limit_bytes=VMEM_LIMIT),
        name="merge",
    )(a_out, h_out, s_out, z, z, z, w_a, w_h, w_s)


def _out_proj_kernel(y_ref, w_ref, x_ref, gt_ref, o_ref, w_s):
    @pl.when(pl.program_id(1) == 0)
    def _():
        w_s[...] = w_ref[...].astype(_bf16)

    o_ref[...] = x_ref[...] + gt_ref[...] * jnp.dot(y_ref[...], w_s[...], preferred_element_type=_f32)


def _out_proj(y, w_out, x, gt, layer):
    return pl.pallas_call(
        _out_proj_kernel,
        grid=(D_MODEL // MERGE_TN, N_TOK // ROW_TILE),
        in_specs=[
            pl.BlockSpec((ROW_TILE, D_MODEL), lambda j, i: (i, 0)),
            pl.BlockSpec((None, D_MODEL, MERGE_TN), lambda j, i: (layer, 0, j)),
            pl.BlockSpec((ROW_TILE, MERGE_TN), lambda j, i: (i, j)),
            pl.BlockSpec((None, 1, MERGE_TN), lambda j, i: (_group_of_tile(i), 0, j)),
        ],
        out_specs=pl.BlockSpec((ROW_TILE, MERGE_TN), lambda j, i: (i, j)),
        out_shape=jax.ShapeDtypeStruct((N_TOK, D_MODEL), _f32),
        scratch_shapes=[pltpu.VMEM((D_MODEL, MERGE_TN), _bf16)],
        compiler_params=pltpu.CompilerParams(dimension_semantics=("arbitrary", "arbitrary"),
                                             vmem_limit_bytes=VMEM_LIMIT),
        name="out_proj",
    )(y, w_out, x, gt)


COL_Q, COL_K, COL_V, COL_HQ, COL_HFF, COL_HFB, COL_HI, COL_HG, COL_SU = (sum(IN_SIZES[:k]) for k in range(9))
KV_GROUP_W = Q_GROUP * HEAD_DIM


def _head_norm(x, g):
    return x * lax.rsqrt(jnp.mean(x * x, axis=-1, keepdims=True) + EPS) * g


def _dot_nt(a, b):
    return lax.dot_general(a, b, (((1,), (1,)), ((), ())), preferred_element_type=_f32)


def _softmax_with_sink(s, sink):
    m = jnp.maximum(jnp.max(s, axis=-1, keepdims=True), sink)
    p = jnp.exp(s - m)
    den = jnp.sum(p, axis=-1, keepdims=True) + jnp.exp(sink - m)
    return p / den


def _ctx_attn_kernel(q_ref, k_ref, v_ref, qg_ref, kg_ref, sink_ref, o_ref, ko_ref, vo_ref):
    kn = _head_norm(k_ref[...], kg_ref[...])
    v = v_ref[...]
    ko_ref[...] = kn
    vo_ref[...] = v
    knb = kn.astype(_bf16)
    vb = v.astype(_bf16)
    for g in range(Q_GROUP):
        cols = slice(g * HEAD_DIM, (g + 1) * HEAD_DIM)
        qn = _head_norm(q_ref[:, cols], qg_ref[...])
        s = _dot_nt(qn.astype(_bf16), knb) * HEAD_DIM ** -0.5
        p = _softmax_with_sink(s, sink_ref[0:1, g:g + 1])
        o_ref[:, cols] = jnp.dot(p.astype(_bf16), vb, preferred_element_type=_f32).astype(_bf16)


def _ctx_attention(z, q_g, k_g, sink, layer):
    lay = lambda b, j: (layer, 0, 0)
    return pl.pallas_call(
        _ctx_attn_kernel,
        grid=(BATCH, N_KV_HEADS),
        in_specs=[
            pl.BlockSpec((SEQ, KV_GROUP_W), lambda b, j: (b, COL_Q // KV_GROUP_W + j)),
            pl.BlockSpec((SEQ, HEAD_DIM), lambda b, j: (b, COL_K // HEAD_DIM + j)),
            pl.BlockSpec((SEQ, HEAD_DIM), lambda b, j: (b, COL_V // HEAD_DIM + j)),
            pl.BlockSpec((None, 1, HEAD_DIM), lay),
            pl.BlockSpec((None, 1, HEAD_DIM), lay),
            pl.BlockSpec((None, None, 1, Q_GROUP), lambda b, j: (layer, j, 0, 0)),
        ],
        out_specs=[
            pl.BlockSpec((SEQ, KV_GROUP_W), lambda b, j: (b, j)),
            pl.BlockSpec((SEQ, HEAD_DIM), lambda b, j: (b, j)),
            pl.BlockSpec((SEQ, HEAD_DIM), lambda b, j: (b, j)),
        ],
        out_shape=[
            jax.ShapeDtypeStruct((N_CTX, ATT_Q_W), _bf16),
            jax.ShapeDtypeStruct((N_CTX, ATT_KV_W), _f32),
            jax.ShapeDtypeStruct((N_CTX, ATT_KV_W), _f32),
        ],
        compiler_params=pltpu.CompilerParams(dimension_semantics=("arbitrary", "arbitrary"),
                                             vmem_limit_bytes=VMEM_LIMIT),
        name="ctx_attention",
    )(z, z, z, q_g.reshape(DEPTH, 1, HEAD_DIM), k_g.reshape(DEPTH, 1, HEAD_DIM),
      sink.reshape(DEPTH, N_KV_HEADS, 1, Q_GROUP))


def _rope_tables():
    half = HEAD_DIM // 2
    nf = half // 2
    t = np.arange(DEC_SEQ)
    pos = np.stack([t // GRID_W, t % GRID_W], axis=1).astype(np.float64)
    inv = ROPE_BASE ** (-np.arange(nf, dtype=np.float64) / nf)
    dim = np.arange(HEAD_DIM)
    ang = pos[:, dim // half] * inv[dim % nf][None, :]
    sign = np.where((dim % half) < nf, -1.0, 1.0)
    return jnp.asarray(np.cos(ang), _f32), jnp.asarray(np.sin(ang) * sign, _f32)


def _rope(x, cos, sin_signed):
    nf = HEAD_DIM // 4
    lane = lax.broadcasted_iota(jnp.int32, x.shape, 1)
    partner = jnp.where((lane % (2 * nf)) < nf, pltpu.roll(x, HEAD_DIM - nf, 1), pltpu.roll(x, nf, 1))
    return x * cos + partner * sin_signed


def _win_attn_kernel(q_ref, kp_ref, kc_ref, kn_ref, vp_ref, vc_ref, vn_ref, cq_ref, sq_ref, cp_ref, sp_ref,
                     cn_ref, sn_ref, ck_ref, cv_ref, qg_ref, kg_ref, sink_ref, o_ref):
    n = pl.program_id(1)
    nb = pl.num_programs(1)
    kg = kg_ref[...]
    k_all = jnp.concatenate([
        _rope(_head_norm(kp_ref[...], kg), cp_ref[...], sp_ref[...]),
        _rope(_head_norm(kc_ref[...], kg), cq_ref[...], sq_ref[...]),
        _rope(_head_norm(kn_ref[...], kg), cn_ref[...], sn_ref[...]),
        ck_ref[...]], axis=0).astype(_bf16)
    v_all = jnp.concatenate([vp_ref[...], vc_ref[...], vn_ref[...], cv_ref[...]], axis=0).astype(_bf16)
    r = lax.broadcasted_iota(jnp.int32, (ATTN_BLOCK, ATTN_BLOCK), 0)
    c = lax.broadcasted_iota(jnp.int32, (ATTN_BLOCK, ATTN_BLOCK), 1)
    neg = jnp.full((ATTN_BLOCK, ATTN_BLOCK), -jnp.inf, _f32)
    zero = jnp.zeros((ATTN_BLOCK, ATTN_BLOCK), _f32)
    bias = jnp.concatenate([
        jnp.where(n > 0, jnp.where(c >= r, zero, neg), neg),
        zero,
        jnp.where(n < nb - 1, jnp.where(c <= r, zero, neg), neg),
        jnp.zeros((ATTN_BLOCK, ck_ref.shape[0]), _f32)], axis=1)
    for g in range(Q_GROUP):
        cols = slice(g * HEAD_DIM, (g + 1) * HEAD_DIM)
        qn = _rope(_head_norm(q_ref[:, cols], qg_ref[...]), cq_ref[...], sq_ref[...])
        s = _dot_nt(qn.astype(_bf16), k_all) * HEAD_DIM ** -0.5 + bias
        p = _softmax_with_sink(s, sink_ref[0:1, g:g + 1])
        o_ref[:, cols] = jnp.dot(p.astype(_bf16), v_all, preferred_element_type=_f32).astype(_bf16)


def _win_attention(z, cache_k, cache_v, q_g, k_g, sink, layer):
    cos, sin = _rope_tables()
    nb = DEC_SEQ // ATTN_BLOCK
    base = N_CTX // ATTN_BLOCK
    prev = lambda n: jnp.maximum(n - 1, 0)
    nxt = lambda n: jnp.minimum(n + 1, nb - 1)
    past = cache_k.shape[3]

    def zrow(sel, col0):
        return pl.BlockSpec((ATTN_BLOCK, HEAD_DIM), lambda b, n, j: (base + b * nb + sel(n), col0 // HEAD_DIM + j))

    def tab(sel):
        return pl.BlockSpec((ATTN_BLOCK, HEAD_DIM), lambda b, n, j: (sel(n), 0))

    same = lambda n: n
    lay = lambda b, n, j: (layer, 0, 0)
    cache = pl.BlockSpec((None, None, None, past, HEAD_DIM), lambda b, n, j: (b, layer, j, 0, 0))
    return pl.pallas_call(
        _win_attn_kernel,
        grid=(DEC_BATCH, nb, N_KV_HEADS),
        in_specs=[
            pl.BlockSpec((ATTN_BLOCK, KV_GROUP_W), lambda b, n, j: (base + b * nb + n, COL_Q // KV_GROUP_W + j)),
            zrow(prev, COL_K), zrow(same, COL_K), zrow(nxt, COL_K),
            zrow(prev, COL_V), zrow(same, COL_V), zrow(nxt, COL_V),
            tab(same), tab(same), tab(prev), tab(prev), tab(nxt), tab(nxt),
            cache, cache,
            pl.BlockSpec((None, 1, HEAD_DIM), lay),
            pl.BlockSpec((None, 1, HEAD_DIM), lay),
            pl.BlockSpec((None, None, 1, Q_GROUP), lambda b, n, j: (layer, j, 0, 0)),
        ],
        out_specs=pl.BlockSpec((ATTN_BLOCK, KV_GROUP_W), lambda b, n, j: (b * nb + n, j)),
        out_shape=jax.ShapeDtypeStruct((N_LAT, ATT_Q_W), _bf16),
        compiler_params=pltpu.CompilerParams(dimension_semantics=("arbitrary", "arbitrary", "arbitrary"),
                                             vmem_limit_bytes=VMEM_LIMIT),
        name="win_attention",
    )(z, z, z, z, z, z, z, cos, sin, cos, sin, cos, sin, cache_k, cache_v,
      q_g.reshape(DEPTH, 1, HEAD_DIM), k_g.reshape(DEPTH, 1, HEAD_DIM), sink.reshape(DEPTH, N_KV_HEADS, 1, Q_GROUP))


HG_BLOCK = 128
HG_LEVELS = 7
HG_NSEG = HG_LEVELS + 2


def _hgrn_constants():
    c = HG_BLOCK
    i = np.arange(c)
    seg = np.zeros((HG_NSEG, c, c), np.float32)
    mask = np.zeros((HG_LEVELS + 1, c, c), np.float32)
    mask[0] = np.eye(c)
    for lv in range(HG_LEVELS):
        b = 2 << lv
        h = b // 2
        mid = (i // b) * b + h
        second = (i % b) >= h
        for r in range(c):
            if second[r]:
                seg[lv, r, mid[r]:r + 1] = 1.0
            else:
                seg[lv, r, r + 1:mid[r]] = 1.0
        same = (i[:, None] // b) == (i[None, :] // b)
        mask[lv + 1] = same & second[:, None] & ~second[None, :]
    seg[HG_LEVELS] = np.tril(np.ones((c, c)))
    seg[HG_LEVELS + 1] = np.triu(np.ones((c, c)), 1)
    seg2 = np.stack([seg, seg[:, ::-1, ::-1]]).reshape(2, HG_NSEG * c, c)
    mask2 = np.stack([mask, mask[:, ::-1, ::-1]])
    return jnp.asarray(seg2, _bf16), jnp.asarray(mask2, _f32)


def _hgrn_schedule():
    rows, first, last, seq = [[], []], [], [], []
    for s in range(BATCH + DEC_BATCH):
        n = (SEQ if s < BATCH else DEC_SEQ) // HG_BLOCK
        base = s * (SEQ // HG_BLOCK) if s < BATCH else N_CTX // HG_BLOCK + (s - BATCH) * (DEC_SEQ // HG_BLOCK)
        for cpos in range(n):
            rows[0].append(base + cpos)
            rows[1].append(base + n - 1 - cpos)
            first.append(int(cpos == 0))
            last.append(int(cpos == n - 1))
            seq.append(s)
    as_i32 = lambda a: jnp.asarray(np.asarray(a, np.int32).reshape(-1))
    return as_i32(rows), as_i32(first), as_i32(last), as_i32(seq), len(seq)


def _hgrn_kernel(rows_ref, first_ref, last_ref, seq_ref, q_ref, f_ref, i_ref, lb_ref, seg_ref, mask_ref, s0_ref,
                 o_ref, sf_ref, st_s):
    job = pl.program_id(2)

    @pl.when(first_ref[job] == 1)
    def _():
        st_s[...] = s0_ref[...].T

    xq = q_ref[...]
    q = xq * jax.nn.sigmoid(xq) * HG_DK ** -0.5
    xf = f_ref[...]
    log_sig = jnp.minimum(xf, 0.0) - jnp.log(1.0 + jnp.exp(-jnp.abs(xf)))
    lb = lb_ref[...]
    f = lb + (1.0 - lb) * jnp.exp(log_sig)
    log_f = jnp.where(lb > 0.0, jnp.log(f), log_sig)
    k = 1.0 - f
    v = i_ref[...]
    vb = v.astype(_bf16)
    lf_hi, lf_lo = _split_bf16(log_f)
    seg = seg_ref[...]
    decay = jnp.exp(jnp.dot(seg, lf_hi, preferred_element_type=_f32)
                    + jnp.dot(seg, lf_lo, preferred_element_type=_f32))
    scores = _dot_nt(q.astype(_bf16), k.astype(_bf16)) * mask_ref[0]
    for lv in range(HG_LEVELS):
        e = decay[lv * HG_BLOCK:(lv + 1) * HG_BLOCK]
        scores = scores + _dot_nt((q * e).astype(_bf16), (k * e).astype(_bf16)) * mask_ref[lv + 1]
    e_q = decay[HG_LEVELS * HG_BLOCK:(HG_LEVELS + 1) * HG_BLOCK]
    e_k = decay[(HG_LEVELS + 1) * HG_BLOCK:(HG_LEVELS + 2) * HG_BLOCK]
    st = st_s[...]
    o_ref[...] = (jnp.dot(scores.astype(_bf16), vb, preferred_element_type=_f32)
                  + _dot_nt((q * e_q).astype(_bf16), st.astype(_bf16)))
    total = jnp.sum(log_f, axis=0, keepdims=True)
    st_new = jnp.exp(total) * st + jnp.dot(v.T.astype(_bf16), (k * e_k).astype(_bf16),
                                           preferred_element_type=_f32)
    st_s[...] = st_new

    @pl.when(last_ref[job] == 1)
    def _():
        sf_ref[...] = st_new.T


def _hgrn(z, lb, s0):
    seg, mask = _hgrn_constants()
    rows, first, last, seq, n_jobs = _hgrn_schedule()

    def zcol(col0):
        return pl.BlockSpec((HG_BLOCK, HG_DK),
                            lambda h, d, j, r, f, l, s: (r[d * n_jobs + j], col0 // HG_DK + h))

    state = pl.BlockSpec((None, None, None, HG_DK, HG_DV), lambda h, d, j, r, f, l, s: (s[j], d, h, 0, 0))
    grid_spec = pltpu.PrefetchScalarGridSpec(
        num_scalar_prefetch=4,
        grid=(HG_HEADS, 2, n_jobs),
        in_specs=[
            zcol(COL_HQ),
            pl.BlockSpec((HG_BLOCK, HG_DK),
                         lambda h, d, j, r, f, l, s: (r[d * n_jobs + j], (COL_HFF + d * HG_W) // HG_DK + h)),
            zcol(COL_HI),
            pl.BlockSpec((None, None, 1, HG_DK), lambda h, d, j, r, f, l, s: (d, h, 0, 0)),
            pl.BlockSpec((None, HG_NSEG * HG_BLOCK, HG_BLOCK), lambda h, d, j, r, f, l, s: (d, 0, 0)),
            pl.BlockSpec((None, HG_LEVELS + 1, HG_BLOCK, HG_BLOCK), lambda h, d, j, r, f, l, s: (d, 0, 0, 0)),
            state,
        ],
        out_specs=[
            pl.BlockSpec((None, HG_BLOCK, HG_DV), lambda h, d, j, r, f, l, s: (d, r[d * n_jobs + j], h)),
            state,
        ],
        scratch_shapes=[pltpu.VMEM((HG_DV, HG_DK), _f32)],
    )
    return pl.pallas_call(
        _hgrn_kernel,
        grid_spec=grid_spec,
        out_shape=[
            jax.ShapeDtypeStruct((2, N_TOK, HG_W), _f32),
            jax.ShapeDtypeStruct(s0.shape, _f32),
        ],
        compiler_params=pltpu.CompilerParams(dimension_semantics=("arbitrary", "arbitrary", "arbitrary"),
                                             vmem_limit_bytes=VMEM_LIMIT),
        name="hgrn",
    )(rows, first, last, seq, z, z, z, lb.reshape(2, HG_HEADS, 1, HG_DK), seg, mask, s0)


def _hgrn_out_kernel(of_ref, ob_ref, g_ref, ng_ref, o_ref):
    ng = ng_ref[...]
    for hh in range(HG_OUT_W // HG_DV):
        cols = slice(hh * HG_DV, (hh + 1) * HG_DV)
        o = _head_norm(of_ref[:, cols] + ob_ref[:, cols], ng)
        xg = g_ref[:, cols]
        o_ref[:, cols] = (o * (xg * jax.nn.sigmoid(xg))).astype(_bf16)


def _hgrn_out(o_dirs, z, norm_g, layer):
    return pl.pallas_call(
        _hgrn_out_kernel,
        grid=(N_TOK // ROW_TILE, HG_W // HG_OUT_W),
        in_specs=[
            pl.BlockSpec((None, ROW_TILE, HG_OUT_W), lambda i, j: (0, i, j)),
            pl.BlockSpec((None, ROW_TILE, HG_OUT_W), lambda i, j: (1, i, j)),
            pl.BlockSpec((ROW_TILE, HG_OUT_W), lambda i, j: (i, COL_HG // HG_OUT_W + j)),
            pl.BlockSpec((None, 1, HG_DV), lambda i, j: (layer, 0, 0)),
        ],
        out_specs=pl.BlockSpec((ROW_TILE, HG_OUT_W), lambda i, j: (i, j)),
        out_shape=jax.ShapeDtypeStruct((N_TOK, HG_W), _bf16),
        compiler_params=pltpu.CompilerParams(dimension_semantics=("arbitrary", "arbitrary"),
                                             vmem_limit_bytes=VMEM_LIMIT),
        name="hgrn_out",
    )(o_dirs, o_dirs, z, norm_g.reshape(DEPTH, 1, HG_DV))


def _rms_norm(x, g):
    xf = x.astype(_f32)
    y = xf * lax.rsqrt(jnp.mean(xf * xf, axis=-1, keepdims=True) + EPS)
    return (y * g.astype(_f32)).astype(x.dtype)


def _split_in(z):
    return jnp.split(z, np.cumsum(IN_SIZES)[:-1].tolist(), axis=-1)


def _attn_heads(q, k, v, q_g, k_g):
    b, t = q.shape[:2]
    q = _rms_norm(q.reshape(b, t, N_HEADS, HEAD_DIM), q_g)
    k = _rms_norm(k.reshape(b, t, N_KV_HEADS, HEAD_DIM), k_g)
    return q, k, v.reshape(b, t, N_KV_HEADS, HEAD_DIM)


def _rotate(x, pos):
    nf = x.shape[-1] // 2
    inv = ROPE_BASE ** (-jnp.arange(nf, dtype=_f32) / nf)
    ang = pos.astype(_f32)[:, None] * inv[None, :]
    cos = jnp.cos(ang)[None, :, None, :]
    sin = jnp.sin(ang)[None, :, None, :]
    x1, x2 = x[..., :nf], x[..., nf:]
    return jnp.concatenate([x1 * cos - x2 * sin, x1 * sin + x2 * cos], axis=-1)


def _axial_rope(x):
    t = x.shape[1]
    rows = t // GRID_W
    row = jnp.repeat(jnp.arange(rows), GRID_W)
    col = jnp.tile(jnp.arange(GRID_W), rows)
    half = HEAD_DIM // 2
    return jnp.concatenate([_rotate(x[..., :half], row), _rotate(x[..., half:], col)], axis=-1)


def _sink_column(sink, lead_shape):
    s = sink.astype(_f32).reshape(N_KV_HEADS, Q_GROUP, 1, 1)
    return jnp.broadcast_to(s, lead_shape + (1,))


def _context_attention(q, k, v, sink):
    b, l = q.shape[:2]
    scale = HEAD_DIM ** -0.5
    qb = q.reshape(b, l, N_KV_HEADS, Q_GROUP, HEAD_DIM)
    s = jnp.einsum('bqhgd,bshd->bhgqs', qb, k).astype(_f32) * scale
    s = jnp.concatenate([s, _sink_column(sink, s.shape[:-1])], axis=-1)
    pr = jax.nn.softmax(s, axis=-1)[..., :l]
    o = jnp.einsum('bhgqs,bshd->bqhgd', pr, v)
    return o.reshape(b, l, N_HEADS * HEAD_DIM)


def _window_attention(q, k, v, k_ctx, v_ctx, sink):
    b, t = q.shape[:2]
    nb = t // ATTN_BLOCK
    lc = k_ctx.shape[1]
    scale = HEAD_DIM ** -0.5
    qb = q.reshape(b, nb, ATTN_BLOCK, N_KV_HEADS, Q_GROUP, HEAD_DIM)
    pad = ((0, 0), (ATTN_BLOCK, ATTN_BLOCK), (0, 0), (0, 0))

    def bands(a):
        ap = jnp.pad(a, pad).reshape(b, nb + 2, ATTN_BLOCK, N_KV_HEADS, HEAD_DIM)
        return jnp.concatenate([ap[:, :-2], ap[:, 1:-1], ap[:, 2:]], axis=2)

    kb, vb = bands(k), bands(v)
    blk = jnp.arange(nb)[:, None] * ATTN_BLOCK
    qpos = blk + jnp.arange(ATTN_BLOCK)[None, :]
    kpos = blk - ATTN_BLOCK + jnp.arange(3 * ATTN_BLOCK)[None, :]
    rel = kpos[:, None, :] - qpos[:, :, None]
    valid = (jnp.abs(rel) <= WINDOW) & (kpos[:, None, :] >= 0) & (kpos[:, None, :] < t)
    s_loc = jnp.einsum('bnqhgd,bnshd->bnhgqs', qb, kb).astype(_f32) * scale
    s_loc = jnp.where(valid[None, :, None, None], s_loc, -jnp.inf)
    s_ctx = jnp.einsum('bnqhgd,bshd->bnhgqs', qb, k_ctx).astype(_f32) * scale
    s = jnp.concatenate([s_loc, s_ctx, _sink_column(sink, s_loc.shape[:-1])], axis=-1)
    pr = jax.nn.softmax(s, axis=-1)
    w = 3 * ATTN_BLOCK
    o = (jnp.einsum('bnhgqs,bnshd->bnqhgd', pr[..., :w], vb)
         + jnp.einsum('bnhgqs,bshd->bnqhgd', pr[..., w:w + lc], v_ctx))
    return o.reshape(b, t, N_HEADS * HEAD_DIM)


def _hgrn2_forget(f_pre, lb):
    x = f_pre.astype(_f32)
    log_f = jnp.logaddexp(jnp.log(lb), jnp.log1p(-lb) + jax.nn.log_sigmoid(x))
    return 1.0 - jnp.exp(log_f), log_f


def _gla_chunkwise(q, k, v, log_f, s0):
    b, t, h, dk = q.shape
    dv = v.shape[-1]
    nc = t // HG_CHUNK
    q, k, log_f = (a.reshape(b, nc, HG_CHUNK, h, dk) for a in (q, k, log_f))
    v = v.reshape(b, nc, HG_CHUNK, h, dv)
    cum = jnp.cumsum(log_f, axis=2)
    last = cum[:, :, -1]
    causal = jnp.tril(jnp.ones((HG_CHUNK, HG_CHUNK), dtype=bool))[None, None, :, :, None, None]
    diff = cum[:, :, :, None] - cum[:, :, None, :]
    decay = jnp.exp(jnp.where(causal, diff, -jnp.inf))
    scores = jnp.einsum('bctshd,bcthd,bcshd->bchts', decay, q, k)
    o_intra = jnp.einsum('bchts,bcshe->bcthe', scores, v)
    kv = jnp.einsum('bcshd,bcshe->bchde', k * jnp.exp(last[:, :, None] - cum), v)

    def step(state, inp):
        g_c, kv_c = inp
        return jnp.exp(g_c)[..., None] * state + kv_c, state

    s_final, s_prev = lax.scan(step, s0, (jnp.moveaxis(last, 1, 0), jnp.moveaxis(kv, 1, 0)))
    s_prev = jnp.moveaxis(s_prev, 0, 1)
    o_inter = jnp.einsum('bcthd,bchde->bcthe', q * jnp.exp(cum), s_prev)
    return (o_intra + o_inter).reshape(b, t, h, dv), s_final


def _hgrn2_bidir(hq, hff, hfb, hi, hg, lb, norm_g, s0):
    b, t = hq.shape[:2]

    def heads(a):
        return a.astype(_f32).reshape(b, t, HG_HEADS, HG_DK)

    def flip(a):
        return jnp.flip(a, axis=1)

    q = heads(jax.nn.silu(hq.astype(_f32)) * HG_DK ** -0.5)
    v = heads(hi)
    k_f, lf_f = _hgrn2_forget(hff, lb[0])
    k_b, lf_b = _hgrn2_forget(hfb, lb[1])
    s0 = s0.astype(_f32)
    o_f, s_f = _gla_chunkwise(q, heads(k_f), v, heads(lf_f), s0[:, 0])
    o_b, s_b = _gla_chunkwise(flip(q), flip(heads(k_b)), flip(v), flip(heads(lf_b)), s0[:, 1])
    o = _rms_norm(o_f + flip(o_b), norm_g) * jax.nn.silu(heads(hg))
    return o.reshape(b, t, HG_W), jnp.stack([s_f, s_b], axis=1)


def _s5_param_kernel(are_ref, aim_ref, ldt_ref, bre_ref, bim_ref, abre_ref, abim_ref, bpre_ref, bpim_ref):
    bre = bre_ref[...]
    bim = bim_ref[...]
    for d in range(2):
        lr = are_ref[d]
        li = aim_ref[d]
        dt = jnp.exp(ldt_ref[d])
        mag = jnp.exp(lr * dt)
        ar = mag * jnp.cos(li * dt)
        ai = mag * jnp.sin(li * dt)
        abre_ref[d] = ar
        abim_ref[d] = ai
        nr = ar - 1.0
        den = lr * lr + li * li
        gr = (nr * lr + ai * li) / den
        gi = (ai * lr - nr * li) / den
        gr3 = gr[:, None, :]
        gi3 = gi[:, None, :]
        bpre_ref[d] = gr3 * bre - gi3 * bim
        bpim_ref[d] = gr3 * bim + gi3 * bre


def _s5_params(a_re, a_im, log_dt, b_re, b_im, c_re, c_im):
    g, pst, ch = S5_GROUPS, S5_STATE, S5_CH
    abre, abim, bpre, bpim = pl.pallas_call(
        _s5_param_kernel,
        out_shape=[
            jax.ShapeDtypeStruct((2, g, pst), _f32),
            jax.ShapeDtypeStruct((2, g, pst), _f32),
            jax.ShapeDtypeStruct((2, g, ch, pst), _f32),
            jax.ShapeDtypeStruct((2, g, ch, pst), _f32),
        ],
        name="s5_params",
    )(a_re, a_im, log_dt.reshape(2, g, 1), b_re.transpose(0, 2, 1), b_im.transpose(0, 2, 1))
    eye = jnp.eye(S5_SLAB_GROUPS, dtype=_f32)

    def b_slabs(bp):
        bp = bp.reshape(2, S5_NSLAB, S5_SLAB_GROUPS, ch, pst)
        w = bp[:, :, :, :, None, :] * eye[None, None, :, None, :, None]
        return w.reshape(2, S5_NSLAB, S5_SLAB_CH, S5_SLAB_ST)

    def c_slabs(cm):
        cm = cm.reshape(S5_NSLAB, S5_SLAB_GROUPS, ch, pst).transpose(0, 1, 3, 2)
        w = cm[:, :, :, None, :] * eye[None, :, None, :, None]
        return w.reshape(S5_NSLAB, S5_SLAB_ST, S5_SLAB_CH)

    b_blk = jnp.concatenate([b_slabs(bpre), b_slabs(bpim)], axis=-1).astype(_bf16)
    return (abre.reshape(2, 1, g * pst), abim.reshape(2, 1, g * pst), b_blk,
            c_slabs(c_re).astype(_bf16), c_slabs(c_im).astype(_bf16))


def _s5_scan_kernel(u_ref, b_ref, cre_ref, cim_ref, are_ref, aim_ref, x0re_ref, x0im_ref,
                    y_ref, xfre_ref, xfim_ref, bu_s, zre_s, zim_s, *, tb_steps, rd):
    tb = pl.program_id(2)
    n_rg = rd // SUBLANES
    st = S5_SLAB_ST

    @pl.when(tb == 0)
    def _():
        zre_s[...] = x0re_ref[...]
        zim_s[...] = x0im_ref[...]

    u = u_ref[...].reshape(tb_steps * rd, S5_SLAB_CH).astype(_bf16)
    bu_s[...] = jnp.dot(u, b_ref[...], preferred_element_type=_f32).reshape(tb_steps, rd, 2 * st)
    are = jnp.broadcast_to(are_ref[...], (SUBLANES, st))
    aim = jnp.broadcast_to(aim_ref[...], (SUBLANES, st))

    def body(t, carry):
        new = []
        for rg in range(n_rg):
            zre, zim = carry[rg]
            rows = slice(rg * SUBLANES, (rg + 1) * SUBLANES)
            nre = are * zre - aim * zim + bu_s[t, rows, 0:st]
            nim = are * zim + aim * zre + bu_s[t, rows, st:2 * st]
            bu_s[t, rows, 0:st] = nre
            bu_s[t, rows, st:2 * st] = nim
            new.append((nre, nim))
        return tuple(new)

    init = tuple((zre_s[rg * SUBLANES:(rg + 1) * SUBLANES, :], zim_s[rg * SUBLANES:(rg + 1) * SUBLANES, :])
                 for rg in range(n_rg))
    fin = lax.fori_loop(0, tb_steps, body, init, unroll=4)
    for rg in range(n_rg):
        zre_s[rg * SUBLANES:(rg + 1) * SUBLANES, :] = fin[rg][0]
        zim_s[rg * SUBLANES:(rg + 1) * SUBLANES, :] = fin[rg][1]

    xs = bu_s[...].reshape(tb_steps * rd, 2 * st)
    y = (jnp.dot(xs[:, 0:st].astype(_bf16), cre_ref[...], preferred_element_type=_f32)
         - jnp.dot(xs[:, st:2 * st].astype(_bf16), cim_ref[...], preferred_element_type=_f32))
    y_ref[...] = y.reshape(tb_steps, rd, S5_SLAB_CH)

    @pl.when(tb == pl.num_programs(2) - 1)
    def _():
        xfre_ref[...] = zre_s[...]
        xfim_ref[...] = zim_s[...]


def _s5_scan(u, params, x0_re, x0_im):
    abre, abim, b_blk, c_re, c_im = params
    _, t_len, rd, _ = u.shape
    tb_steps = S5_BLOCK_ROWS // rd
    n_tb = t_len // tb_steps
    st = S5_SLAB_ST
    grid = (2, S5_NSLAB, n_tb)
    kern = functools.partial(_s5_scan_kernel, tb_steps=tb_steps, rd=rd)
    return pl.pallas_call(
        kern,
        grid=grid,
        in_specs=[
            pl.BlockSpec((None, tb_steps, rd, S5_SLAB_CH), lambda d, s, t: (d, t, 0, s)),
            pl.BlockSpec((None, None, S5_SLAB_CH, 2 * st), lambda d, s, t: (d, s, 0, 0)),
            pl.BlockSpec((None, st, S5_SLAB_CH), lambda d, s, t: (s, 0, 0)),
            pl.BlockSpec((None, st, S5_SLAB_CH), lambda d, s, t: (s, 0, 0)),
            pl.BlockSpec((None, 1, st), lambda d, s, t: (d, 0, s)),
            pl.BlockSpec((None, 1, st), lambda d, s, t: (d, 0, s)),
            pl.BlockSpec((None, rd, st), lambda d, s, t: (d, 0, s)),
            pl.BlockSpec((None, rd, st), lambda d, s, t: (d, 0, s)),
        ],
        out_specs=[
            pl.BlockSpec((None, tb_steps, rd, S5_SLAB_CH), lambda d, s, t: (d, t, 0, s)),
            pl.BlockSpec((None, rd, st), lambda d, s, t: (d, 0, s)),
            pl.BlockSpec((None, rd, st), lambda d, s, t: (d, 0, s)),
        ],
        out_shape=[
            jax.ShapeDtypeStruct((2, t_len, rd, S5_W), _f32),
            jax.ShapeDtypeStruct((2, rd, S5_GROUPS * S5_STATE), _f32),
            jax.ShapeDtypeStruct((2, rd, S5_GROUPS * S5_STATE), _f32),
        ],
        scratch_shapes=[
            pltpu.VMEM((tb_steps, rd, 2 * st), _f32),
            pltpu.VMEM((rd, st), _f32),
            pltpu.VMEM((rd, st), _f32),
        ],
        compiler_params=pltpu.CompilerParams(dimension_semantics=("arbitrary", "arbitrary", "arbitrary"),
                                             vmem_limit_bytes=VMEM_LIMIT),
        name="s5_scan",
    )(u, b_blk, c_re, c_im, abre, abim, x0_re, x0_im)


def _gelu_tanh(x):
    return 0.5 * x * (1.0 + jnp.tanh(math.sqrt(2.0 / math.pi) * (x + 0.044715 * (x * x * x))))


def _s5_out_kernel(yf_ref, yb_ref, u_ref, d_ref, w_ref, b_ref, o_ref, w_s):
    @pl.when(pl.program_id(0) == 0)
    def _():
        w_s[...] = w_ref[...].astype(_bf16)

    y = _gelu_tanh(yf_ref[...] + yb_ref[...] + d_ref[...] * u_ref[...])
    z = jnp.dot(y.astype(_bf16), w_s[...], preferred_element_type=_f32) + b_ref[...]
    o_ref[...] = (y * jax.nn.sigmoid(z)).astype(_bf16)


def _s5_out(yf, yb, u, d_vec, w_glu, b_glu, layer):
    nt = N_TOK // ROW_TILE
    row = lambda i: (i, 0)
    lay = lambda i: (layer, 0, 0)
    return pl.pallas_call(
        _s5_out_kernel,
        grid=(nt,),
        in_specs=[
            pl.BlockSpec((ROW_TILE, S5_W), row),
            pl.BlockSpec((ROW_TILE, S5_W), row),
            pl.BlockSpec((ROW_TILE, S5_W), row),
            pl.BlockSpec((None, 1, S5_W), lay),
            pl.BlockSpec((None, S5_W, S5_W), lay),
            pl.BlockSpec((None, 1, S5_W), lay),
        ],
        out_specs=pl.BlockSpec((ROW_TILE, S5_W), row),
        out_shape=jax.ShapeDtypeStruct((N_TOK, S5_W), _bf16),
        scratch_shapes=[pltpu.VMEM((S5_W, S5_W), _bf16)],
        compiler_params=pltpu.CompilerParams(dimension_semantics=("arbitrary",), vmem_limit_bytes=VMEM_LIMIT),
        name="s5_out",
    )(yf, yb, u, d_vec.reshape(DEPTH, 1, S5_W), w_glu, b_glu.reshape(DEPTH, 1, S5_W))


def _s5_branch(su_c, su_l, sp, s5_re0, s5_im0, layer):
    params = _s5_params(sp['s5_a_re'][layer], sp['s5_a_im'][layer], sp['s5_log_dt'][layer], sp['s5_b_re'][layer],
                        sp['s5_b_im'][layer], sp['s5_c_re'][layer], sp['s5_c_im'][layer])
    nst = S5_GROUPS * S5_STATE

    def both_dirs(u_tm):
        return jnp.stack([u_tm, u_tm[::-1]], axis=0)

    def merge_dirs(y):
        return y[0].transpose(1, 0, 2), y[1, ::-1].transpose(1, 0, 2)

    zeros = jnp.zeros((2, BATCH, nst), _f32)
    y_c, xf_re, xf_im = _s5_scan(both_dirs(su_c.transpose(1, 0, 2)), params, zeros, zeros)
    yf_c, yb_c = merge_dirs(y_c)
    pad = ((0, 0), (0, SUBLANES - DEC_BATCH), (0, 0))
    u_l = both_dirs(jnp.pad(su_l.transpose(1, 0, 2), pad))

    def lat_state(x0):
        return jnp.pad(x0.transpose(1, 0, 2, 3).reshape(2, DEC_BATCH, nst), pad)

    y_l, _, _ = _s5_scan(u_l, params, lat_state(s5_re0), lat_state(s5_im0))
    yf_l, yb_l = merge_dirs(y_l[:, :, :DEC_BATCH])
    yf = jnp.concatenate([yf_c.reshape(N_CTX, S5_W), yf_l.reshape(N_LAT, S5_W)], axis=0)
    yb = jnp.concatenate([yb_c.reshape(N_CTX, S5_W), yb_l.reshape(N_LAT, S5_W)], axis=0)
    u = jnp.concatenate([su_c.reshape(N_CTX, S5_W), su_l.reshape(N_LAT, S5_W)], axis=0)
    s_out = _s5_out(yf, yb, u, sp['s5_d'], sp['s5_w_glu'], sp['s5_b_glu'], layer)

    def ctx_state(xf):
        return xf.reshape(2, BATCH, S5_GROUPS, S5_STATE).transpose(1, 0, 2, 3)

    return s_out, ctx_state(xf_re), ctx_state(xf_im)


def _merge_branches(a_out, h_out, s_out, ga, gh, gs, p):
    y = (jax.nn.sigmoid(ga) * (a_out @ p['w_br_attn'])
         + jax.nn.sigmoid(gh) * (h_out @ p['w_br_hg'])
         + jax.nn.sigmoid(gs) * (s_out @ p['w_br_s5']))
    return y @ p['w_out']


def _legacy_mixer_sublayer(xc, xl, mods, pl_, sp, lb, ck, cv, s_hg0, s5_re0, s5_im0, layer):
    sh1, sc1, gt1 = mods[0], mods[1], mods[2]

    def pre(x, sh, sc):
        h = _rms_norm(x, pl_['norm1_g']) * (1.0 + sc) + sh
        return _split_in(h @ pl_['w_in'])

    zc = pre(xc, sh1[0], sc1[0])
    zl = pre(xl, sh1[1:], sc1[1:])
    s_out, s5_re, s5_im = _s5_branch(zc[8], zl[8], sp, s5_re0, s5_im0, layer)
    s_out_c = s_out[:N_CTX].reshape(BATCH, SEQ, S5_W)
    s_out_l = s_out[N_CTX:].reshape(DEC_BATCH, DEC_SEQ, S5_W)
    q, k, v, hq, hff, hfb, hi, hg, _, ga, gh, gs = zc
    q, k, v = _attn_heads(q, k, v, pl_['q_norm_g'], pl_['k_norm_g'])
    a_out = _context_attention(q, k, v, pl_['sink'])
    z_hg = jnp.zeros((BATCH, 2, HG_HEADS, HG_DK, HG_DV), _f32)
    h_out, s_hg = _hgrn2_bidir(hq, hff, hfb, hi, hg, lb, pl_['hg_norm_g'], z_hg)
    xc = xc + gt1[0] * _merge_branches(a_out, h_out, s_out_c, ga, gh, gs, pl_)
    new = (k, v, s_hg, s5_re, s5_im)
    q, k, v, hq, hff, hfb, hi, hg, _, ga, gh, gs = zl
    q, k, v = _attn_heads(q, k, v, pl_['q_norm_g'], pl_['k_norm_g'])
    a_out = _window_attention(_axial_rope(q), _axial_rope(k), v, ck, cv, pl_['sink'])
    h_out, _ = _hgrn2_bidir(hq, hff, hfb, hi, hg, lb, pl_['hg_norm_g'], s_hg0)
    xl = xl + gt1[1:] * _merge_branches(a_out, h_out, s_out_l, ga, gh, gs, pl_)
    return xc, xl, new


def kernel(x_prompt, x_sample, cache_k, cache_v, state_hgrn, state_s5_re, state_s5_im, c, c_ctx, w_mod, b_mod, norm1_g, norm2_g, w_in, q_norm_g, k_norm_g, attn_sink, hg_lb_logits, hg_norm_g, s5_a_re, s5_a_im, s5_log_dt, s5_b_re, s5_b_im, s5_c_re, s5_c_im, s5_d, s5_w_glu, s5_b_glu, w_br_attn, w_br_hg, w_br_s5, w_out, router_w, router_b, exp_w1, exp_b1, exp_w2, exp_b2):
    lb_all = jnp.cumsum(jax.nn.softmax(hg_lb_logits.astype(_f32), axis=0), axis=0)
    lb_all = lb_all - lb_all[:1]
    cond_rows = jnp.concatenate([c_ctx[None, :], c, jnp.zeros((SUBLANES - N_GROUPS, D_MODEL), _f32)], axis=0)
    x = jnp.concatenate([x_prompt.reshape(N_CTX, D_MODEL), x_sample.reshape(N_LAT, D_MODEL)], axis=0)
    cache_k_t = cache_k.transpose(0, 1, 3, 2, 4)
    cache_v_t = cache_v.transpose(0, 1, 3, 2, 4)
    zero_state = jnp.zeros((BATCH, 2, HG_HEADS, HG_DK, HG_DV), _f32)
    new_k, new_v, new_hg, new_re, new_im = [], [], [], [], []
    moe_p = {"norm2_g": norm2_g, "router_w": router_w, "router_b": router_b, "exp_w1": exp_w1, "exp_b1": exp_b1,
             "exp_w2": exp_w2, "exp_b2": exp_b2}
    s5_p = {'s5_a_re': s5_a_re, 's5_a_im': s5_a_im, 's5_log_dt': s5_log_dt, 's5_b_re': s5_b_re, 's5_b_im': s5_b_im,
            's5_c_re': s5_c_re, 's5_c_im': s5_c_im, 's5_d': s5_d, 's5_w_glu': s5_w_glu, 's5_b_glu': s5_b_glu}
    for l in range(DEPTH):
        m = _modulation(cond_rows, w_mod, b_mod, l)[:N_GROUPS]
        mods = [a[:, None, :] for a in jnp.split(m, N_MOD, axis=-1)]
        sh1, sc1, gt1 = mods[0], mods[1], mods[2]
        z = _in_proj(_norm_mod(x, norm1_g, sc1, sh1, l), w_in, l)
        a_ctx, k_l, v_l = _ctx_attention(z, q_norm_g, k_norm_g, attn_sink, l)
        a_lat = _win_attention(z, cache_k_t, cache_v_t, q_norm_g, k_norm_g, attn_sink, l)
        a_out = jnp.concatenate([a_ctx, a_lat], axis=0)
        o_dirs, hg_fin = _hgrn(z, lb_all[l], jnp.concatenate([zero_state, state_hgrn[:, l]], axis=0))
        h_out = _hgrn_out(o_dirs, z, hg_norm_g, l)
        su = z[:, COL_SU:COL_SU + S5_W]
        s_out, re_l, im_l = _s5_branch(su[:N_CTX].reshape(BATCH, SEQ, S5_W),
                                       su[N_CTX:].reshape(DEC_BATCH, DEC_SEQ, S5_W), s5_p,
                                       state_s5_re[:, l], state_s5_im[:, l], l)
        y = _merge(a_out, h_out, s_out, z, w_br_attn, w_br_hg, w_br_s5, l)
        x = _out_proj(y, w_out, x, gt1, l)
        new_k.append(k_l.reshape(BATCH, SEQ, N_KV_HEADS, HEAD_DIM))
        new_v.append(v_l.reshape(BATCH, SEQ, N_KV_HEADS, HEAD_DIM))
        new_hg.append(hg_fin[:BATCH])
        new_re.append(re_l)
        new_im.append(im_l)
        x = _moe_sublayer(x, mods, moe_p, l)
    return (x[:N_CTX].reshape(BATCH, SEQ, D_MODEL), x[N_CTX:].reshape(DEC_BATCH, DEC_SEQ, D_MODEL),
            jnp.stack(new_k, axis=1), jnp.stack(new_v, axis=1), jnp.stack(new_hg, axis=1),
            jnp.stack(new_re, axis=1), jnp.stack(new_im, axis=1))
```

```python
import functools
import math

import jax
import jax.numpy as jnp
import numpy as np
from jax import lax
from jax.experimental import pallas as pl
from jax.experimental.pallas import tpu as pltpu

D_MODEL = 2048
BATCH = 16
SEQ = 256
DEPTH = 2
DEC_BATCH = 2
DEC_SEQ = 2048
GRID_W = 64
N_HEADS = 8
N_KV_HEADS = 2
HEAD_DIM = 128
Q_GROUP = N_HEADS // N_KV_HEADS
WINDOW = 128
ATTN_BLOCK = 128
ROPE_BASE = 10000.0
ATT_Q_W = N_HEADS * HEAD_DIM
ATT_KV_W = N_KV_HEADS * HEAD_DIM
HG_HEADS = 8
HG_DK = 128
HG_DV = 128
HG_W = HG_HEADS * HG_DK
HG_CHUNK = 16
S5_W = 1024
S5_CH = 16
S5_GROUPS = S5_W // S5_CH
S5_STATE = 64
N_EXPERTS = 32
TOP_K = 4
D_FF = 2048
SWIGLU_ALPHA = 1.702
SWIGLU_LIMIT = 7.0
N_MOD = 6
EPS = 1e-6
IN_SIZES = (ATT_Q_W, ATT_KV_W, ATT_KV_W, HG_W, HG_W, HG_W, HG_W, HG_W, S5_W, D_MODEL, D_MODEL, D_MODEL)
IN_COLS = sum(IN_SIZES)

N_CTX = BATCH * SEQ
N_LAT = DEC_BATCH * DEC_SEQ
N_TOK = N_CTX + N_LAT
N_GROUPS = 1 + DEC_BATCH

LANES = 128
ROW_TILE = 256
VMEM_LIMIT = 56 * 1024 * 1024

MOE_SUPER = 1024
MOE_SUB = 256
MOE_FC = 256
MOE_NC = D_FF // MOE_FC
MOE_NS = (N_TOK * TOP_K) // MOE_SUPER + N_EXPERTS
MOD_TN = 1536
PROJ_TN = 1536
PROJ_TM = 512
MERGE_TN = 512
HG_OUT_W = 512
DISPATCH_TILE = 256
COMBINE_TILE = 64

S5_SLAB_GROUPS = 8
S5_NSLAB = S5_GROUPS // S5_SLAB_GROUPS
S5_SLAB_CH = S5_SLAB_GROUPS * S5_CH
S5_SLAB_ST = S5_SLAB_GROUPS * S5_STATE
S5_BLOCK_ROWS = 1024
SUBLANES = 8

_bf16 = jnp.bfloat16
_f32 = jnp.float32


def _group_of_tile(i):
    ctx_tiles = N_CTX // ROW_TILE
    lat_tiles = DEC_SEQ // ROW_TILE
    return jnp.where(i < ctx_tiles, 0, 1 + (i - ctx_tiles) // lat_tiles)


def _split_bf16(a):
    hi = a.astype(_bf16)
    lo = (a - hi.astype(_f32)).astype(_bf16)
    return hi, lo


_HI16 = 0xFFFF0000


def _pack_bf16_pairs(h):
    half = h.shape[1] // 2
    bits = pltpu.bitcast(h.astype(_bf16).astype(_f32), jnp.uint32)
    return (bits[:, :half] >> 16) | (bits[:, half:] & jnp.uint32(_HI16))


def _unpack_bf16_pairs(words):
    lo = pltpu.bitcast(words << 16, _f32).astype(_bf16)
    hi = pltpu.bitcast(words & jnp.uint32(_HI16), _f32).astype(_bf16)
    return lo, hi


def _router_kernel(x_ref, g_ref, sc_ref, sh_ref, rw_ref, rb_ref, h_ref, idx_ref, w_ref):
    x = x_ref[...]
    y = x * lax.rsqrt(jnp.mean(x * x, axis=-1, keepdims=True) + EPS) * g_ref[...]
    h = y * (1.0 + sc_ref[...]) + sh_ref[...]
    h_ref[...] = _pack_bf16_pairs(h)
    h_hi, h_lo = _split_bf16(h)
    r_hi, r_lo = _split_bf16(rw_ref[...])
    logits = (jnp.dot(h_hi, r_hi, preferred_element_type=_f32)
              + jnp.dot(h_hi, r_lo, preferred_element_type=_f32)
              + jnp.dot(h_lo, r_hi, preferred_element_type=_f32)) + rb_ref[...]
    lane = lax.broadcasted_iota(jnp.int32, logits.shape, 1)
    work = logits
    vals, idxs = [], []
    for _ in range(TOP_K):
        m = jnp.max(work, axis=-1, keepdims=True)
        i = jnp.min(jnp.where(work == m, lane, N_EXPERTS), axis=-1, keepdims=True)
        vals.append(m)
        idxs.append(i)
        work = jnp.where(lane == i, -jnp.inf, work)
    es = [jnp.exp(v - vals[0]) for v in vals]
    den = es[0] + es[1] + es[2] + es[3]
    out_lane = lax.broadcasted_iota(jnp.int32, idx_ref.shape, 1)
    idx_out = jnp.zeros(idx_ref.shape, jnp.int32)
    w_out = jnp.zeros(w_ref.shape, _f32)
    for k in range(TOP_K):
        idx_out = jnp.where(out_lane == k, idxs[k], idx_out)
        w_out = jnp.where(out_lane == k, es[k] / den, w_out)
    idx_ref[...] = idx_out
    w_ref[...] = w_out


def _router(x, norm_g, sc, sh, router_w, router_b, layer):
    nt = N_TOK // ROW_TILE
    row = lambda i: (i, 0)
    grp = lambda i: (_group_of_tile(i), 0, 0)
    lay = lambda i: (layer, 0, 0)
    h, idx, w = pl.pallas_call(
        _router_kernel,
        grid=(nt,),
        in_specs=[
            pl.BlockSpec((ROW_TILE, D_MODEL), row),
            pl.BlockSpec((None, 1, D_MODEL), lay),
            pl.BlockSpec((None, 1, D_MODEL), grp),
            pl.BlockSpec((None, 1, D_MODEL), grp),
            pl.BlockSpec((None, D_MODEL, N_EXPERTS), lay),
            pl.BlockSpec((None, 1, N_EXPERTS), lay),
        ],
        out_specs=[
            pl.BlockSpec((ROW_TILE, D_MODEL // 2), row),
            pl.BlockSpec((ROW_TILE, LANES), row),
            pl.BlockSpec((ROW_TILE, LANES), row),
        ],
        out_shape=[
            jax.ShapeDtypeStruct((N_TOK, D_MODEL // 2), jnp.uint32),
            jax.ShapeDtypeStruct((N_TOK, LANES), jnp.int32),
            jax.ShapeDtypeStruct((N_TOK, LANES), _f32),
        ],
        compiler_params=pltpu.CompilerParams(dimension_semantics=("arbitrary",), vmem_limit_bytes=VMEM_LIMIT),
        name="router",
    )(x, norm_g.reshape(DEPTH, 1, D_MODEL), sc, sh, router_w, router_b.reshape(DEPTH, 1, N_EXPERTS))
    return h, idx[:, :TOP_K], w[:, :TOP_K]


def _moe_kernel(exp_ref, rows_ref, used_ref, x_ref, rw_ref, w1g_ref, w1u_ref, b1g_ref, b1u_ref, w2_ref, b2_ref,
                o_ref, w1g_s, w1u_s, w2_s, x_s):
    s = pl.program_id(0)
    c = pl.program_id(1)
    nvalid = rows_ref[s]
    half = D_MODEL // 2

    @pl.when(c == 0)
    def _():
        o_ref[...] = jnp.zeros(o_ref.shape, _f32)

    @pl.when(nvalid > 0)
    def _():
        @pl.when(c == 0)
        def _():
            lo, hi = _unpack_bf16_pairs(x_ref[...])
            x_s[:, :half] = lo
            x_s[:, half:] = hi

        w1g_s[...] = w1g_ref[...].astype(_bf16)
        w1u_s[...] = w1u_ref[...].astype(_bf16)
        w2_s[...] = w2_ref[...].astype(_bf16)
        for r in range(MOE_SUPER // MOE_SUB):
            @pl.when(r * MOE_SUB < nvalid)
            def _():
                rows = pl.ds(r * MOE_SUB, MOE_SUB)
                xs = x_s[rows, :]
                zg = jnp.dot(xs, w1g_s[...], preferred_element_type=_f32) + b1g_ref[...]
                zu = jnp.dot(xs, w1u_s[...], preferred_element_type=_f32) + b1u_ref[...]
                g = jnp.minimum(zg, SWIGLU_LIMIT)
                u = jnp.clip(zu, -SWIGLU_LIMIT, SWIGLU_LIMIT)
                act = g * jax.nn.sigmoid(SWIGLU_ALPHA * g) * (u + 1.0)
                o_ref[rows, :] += jnp.dot(act.astype(_bf16), w2_s[...], preferred_element_type=_f32)

        @pl.when(c == MOE_NC - 1)
        def _():
            o_ref[...] = rw_ref[...] * (o_ref[...] + b2_ref[...])


def _moe_ffn(x_sorted, row_w, st_expert, st_rows, n_used, w1, b1, w2, b2, layer):
    def xmap(s, c, e_ref, r_ref, u_ref):
        return (jnp.minimum(s, u_ref[0] - 1), 0)

    def chunk(s, c, u_ref):
        return jnp.where(s < u_ref[0], c, MOE_NC - 1)

    def w1g_map(s, c, e_ref, r_ref, u_ref):
        return (layer, e_ref[s], 0, chunk(s, c, u_ref))

    def w1u_map(s, c, e_ref, r_ref, u_ref):
        return (layer, e_ref[s], 0, MOE_NC + chunk(s, c, u_ref))

    def w2_map(s, c, e_ref, r_ref, u_ref):
        return (layer, e_ref[s], chunk(s, c, u_ref), 0)

    def b2_map(s, c, e_ref, r_ref, u_ref):
        return (layer, e_ref[s], 0, 0)

    grid_spec = pltpu.PrefetchScalarGridSpec(
        num_scalar_prefetch=3,
        grid=(MOE_NS, MOE_NC),
        in_specs=[
            pl.BlockSpec((MOE_SUPER, D_MODEL // 2), xmap),
            pl.BlockSpec((MOE_SUPER, 1), xmap),
            pl.BlockSpec((None, None, D_MODEL, MOE_FC), w1g_map),
            pl.BlockSpec((None, None, D_MODEL, MOE_FC), w1u_map),
            pl.BlockSpec((None, None, 1, MOE_FC), w1g_map),
            pl.BlockSpec((None, None, 1, MOE_FC), w1u_map),
            pl.BlockSpec((None, None, MOE_FC, D_MODEL), w2_map),
            pl.BlockSpec((None, None, 1, D_MODEL), b2_map),
        ],
        out_specs=pl.BlockSpec((MOE_SUPER, D_MODEL), lambda s, c, e_ref, r_ref, u_ref: (s, 0)),
        scratch_shapes=[
            pltpu.VMEM((D_MODEL, MOE_FC), _bf16),
            pltpu.VMEM((D_MODEL, MOE_FC), _bf16),
            pltpu.VMEM((MOE_FC, D_MODEL), _bf16),
            pltpu.VMEM((MOE_SUPER, D_MODEL), _bf16),
        ],
    )
    return pl.pallas_call(
        _moe_kernel,
        grid_spec=grid_spec,
        out_shape=jax.ShapeDtypeStruct((MOE_NS * MOE_SUPER, D_MODEL), _f32),
        compiler_params=pltpu.CompilerParams(dimension_semantics=("arbitrary", "arbitrary"),
                                             vmem_limit_bytes=VMEM_LIMIT),
        name="moe_ffn",
    )(st_expert, st_rows, n_used, x_sorted, row_w, w1, w1,
      b1.reshape(DEPTH, N_EXPERTS, 1, 2 * D_FF), b1.reshape(DEPTH, N_EXPERTS, 1, 2 * D_FF),
      w2, b2.reshape(DEPTH, N_EXPERTS, 1, D_MODEL))


def _route_metadata(top_idx, top_w):
    n = N_TOK * TOP_K
    e_flat = top_idx.reshape(n)
    order = jnp.argsort(e_flat, stable=True).astype(jnp.int32)
    e_sorted = e_flat[order]
    grp_end = jnp.searchsorted(e_sorted, jnp.arange(N_EXPERTS, dtype=jnp.int32), side="right").astype(jnp.int32)
    grp_start = jnp.concatenate([jnp.zeros((1,), jnp.int32), grp_end[:-1]])
    counts = grp_end - grp_start
    n_st = (counts + MOE_SUPER - 1) // MOE_SUPER
    st_end = jnp.cumsum(n_st)
    st_start = st_end - n_st
    n_used = st_end[-1]
    s_ids = jnp.arange(MOE_NS, dtype=jnp.int32)
    s_clamped = jnp.minimum(s_ids, n_used - 1)
    st_expert = jnp.searchsorted(st_end, s_clamped, side="right").astype(jnp.int32)
    st_first_rank = (s_clamped - st_start[st_expert]) * MOE_SUPER
    st_rows = jnp.clip(counts[st_expert] - st_first_rank, 0, MOE_SUPER)
    st_rows = jnp.where(s_ids < n_used, st_rows, 0).astype(jnp.int32)
    p_ids = jnp.arange(MOE_NS * MOE_SUPER, dtype=jnp.int32)
    p_tile = p_ids // MOE_SUPER
    p_in_tile = p_ids % MOE_SUPER
    p_valid = p_in_tile < st_rows[p_tile]
    p_entry = jnp.where(p_valid, grp_start[st_expert[p_tile]] + st_first_rank[p_tile] + p_in_tile, 0)
    src_tok = jnp.where(p_valid, order[p_entry] // TOP_K, 0)
    row_w = jnp.where(p_valid, top_w.reshape(n)[order[p_entry]], 0.0)
    inv_order = jnp.argsort(order).astype(jnp.int32)
    rank = inv_order - grp_start[e_flat]
    pos = st_start[e_flat] * MOE_SUPER + rank
    return (src_tok, row_w.reshape(-1, 1), st_expert, st_rows, n_used.reshape(1).astype(jnp.int32),
            pos.reshape(N_TOK, TOP_K))


def _issue_row_copies(idx_ref, src_hbm, dst_ref, sem, n):
    def start(r, carry):
        pltpu.make_async_copy(src_hbm.at[idx_ref[0, r]], dst_ref.at[r], sem).start()
        return carry

    lax.fori_loop(0, n, start, 0, unroll=8)
    pltpu.make_async_copy(src_hbm.at[pl.ds(0, n)], dst_ref, sem).wait()


def _dispatch_kernel(used_ref, idx_ref, src_hbm, o_ref, sem):
    i = pl.program_id(0)

    @pl.when(i < used_ref[0])
    def _():
        _issue_row_copies(idx_ref, src_hbm, o_ref, sem, DISPATCH_TILE)

    @pl.when(i >= used_ref[0])
    def _():
        o_ref[...] = jnp.zeros(o_ref.shape, o_ref.dtype)


def _dispatch(h_words, src_tok, n_used_tiles):
    n_rows = src_tok.shape[0]
    nt = n_rows // DISPATCH_TILE
    width = h_words.shape[1]
    grid_spec = pltpu.PrefetchScalarGridSpec(
        num_scalar_prefetch=1,
        grid=(nt,),
        in_specs=[
            pl.BlockSpec((None, 1, DISPATCH_TILE), lambda i, u: (i, 0, 0), memory_space=pltpu.SMEM),
            pl.BlockSpec(memory_space=pl.ANY),
        ],
        out_specs=pl.BlockSpec((DISPATCH_TILE, width), lambda i, u: (i, 0)),
        scratch_shapes=[pltpu.SemaphoreType.DMA(())],
    )
    return pl.pallas_call(
        _dispatch_kernel,
        grid_spec=grid_spec,
        out_shape=jax.ShapeDtypeStruct((n_rows, width), h_words.dtype),
        compiler_params=pltpu.CompilerParams(dimension_semantics=("arbitrary",), vmem_limit_bytes=VMEM_LIMIT),
        name="moe_dispatch",
    )(n_used_tiles, src_tok.reshape(nt, 1, DISPATCH_TILE), h_words)


def _combine_kernel(pos_ref, y_hbm, x_ref, gt_ref, o_ref, buf, sem):
    _issue_row_copies(pos_ref, y_hbm, buf, sem, TOP_K * COMBINE_TILE)
    y = buf[0:COMBINE_TILE, :]
    for k in range(1, TOP_K):
        y = y + buf[k * COMBINE_TILE:(k + 1) * COMBINE_TILE, :]
    o_ref[...] = x_ref[...] + gt_ref[...] * y


def _combine(y_sorted, pos, x, gt):
    nt = N_TOK // COMBINE_TILE
    tiles_per_row_tile = ROW_TILE // COMBINE_TILE
    pos_tiles = pos.reshape(nt, COMBINE_TILE, TOP_K).transpose(0, 2, 1).reshape(nt, 1, TOP_K * COMBINE_TILE)
    return pl.pallas_call(
        _combine_kernel,
        grid=(nt,),
        in_specs=[
            pl.BlockSpec((None, 1, TOP_K * COMBINE_TILE), lambda i: (i, 0, 0), memory_space=pltpu.SMEM),
            pl.BlockSpec(memory_space=pl.ANY),
            pl.BlockSpec((COMBINE_TILE, D_MODEL), lambda i: (i, 0)),
            pl.BlockSpec((None, 1, D_MODEL), lambda i: (_group_of_tile(i // tiles_per_row_tile), 0, 0)),
        ],
        out_specs=pl.BlockSpec((COMBINE_TILE, D_MODEL), lambda i: (i, 0)),
        out_shape=jax.ShapeDtypeStruct((N_TOK, D_MODEL), _f32),
        scratch_shapes=[pltpu.VMEM((TOP_K * COMBINE_TILE, D_MODEL), _f32), pltpu.SemaphoreType.DMA(())],
        compiler_params=pltpu.CompilerParams(dimension_semantics=("arbitrary",), vmem_limit_bytes=VMEM_LIMIT),
        name="moe_combine",
    )(pos_tiles, y_sorted, x, gt)


def _moe_sublayer(x, mods, p, layer):
    sh2, sc2, gt2 = mods[3], mods[4], mods[5]
    h, top_idx, top_w = _router(x, p["norm2_g"], sc2, sh2, p["router_w"], p["router_b"], layer)
    src_tok, row_w, st_expert, st_rows, n_used, pos = _route_metadata(top_idx, top_w)
    x_sorted = _dispatch(h, src_tok, n_used * (MOE_SUPER // DISPATCH_TILE))
    y_sorted = _moe_ffn(x_sorted, row_w, st_expert, st_rows, n_used, p["exp_w1"], p["exp_b1"], p["exp_w2"],
                        p["exp_b2"], layer)
    return _combine(y_sorted, pos, x, gt2)


def _group_of_rows(i, tile):
    return jnp.where(i < N_CTX // tile, 0, 1 + (i - N_CTX // tile) // (DEC_SEQ // tile))


def _mod_kernel(c_ref, w_ref, b_ref, o_ref):
    c = c_ref[...]
    a = (c * jax.nn.sigmoid(c)).astype(_bf16)
    o_ref[...] = jnp.dot(a, w_ref[...].astype(_bf16), preferred_element_type=_f32) + b_ref[...]


def _modulation(cond_rows, w_mod, b_mod, layer):
    n = N_MOD * D_MODEL
    return pl.pallas_call(
        _mod_kernel,
        grid=(n // MOD_TN,),
        in_specs=[
            pl.BlockSpec((SUBLANES, D_MODEL), lambda j: (0, 0)),
            pl.BlockSpec((None, D_MODEL, MOD_TN), lambda j: (layer, 0, j)),
            pl.BlockSpec((None, 1, MOD_TN), lambda j: (layer, 0, j)),
        ],
        out_specs=pl.BlockSpec((SUBLANES, MOD_TN), lambda j: (0, j)),
        out_shape=jax.ShapeDtypeStruct((SUBLANES, n), _f32),
        compiler_params=pltpu.CompilerParams(dimension_semantics=("arbitrary",), vmem_limit_bytes=VMEM_LIMIT),
        name="modulation",
    )(cond_rows, w_mod, b_mod.reshape(DEPTH, 1, n))


def _norm_mod_kernel(x_ref, g_ref, sc_ref, sh_ref, h_ref):
    x = x_ref[...]
    y = x * lax.rsqrt(jnp.mean(x * x, axis=-1, keepdims=True) + EPS) * g_ref[...]
    h_ref[...] = (y * (1.0 + sc_ref[...]) + sh_ref[...]).astype(_bf16)


def _norm_mod(x, norm_g, sc, sh, layer):
    row = lambda i: (i, 0)
    grp = lambda i: (_group_of_tile(i), 0, 0)
    lay = lambda i: (layer, 0, 0)
    return pl.pallas_call(
        _norm_mod_kernel,
        grid=(N_TOK // ROW_TILE,),
        in_specs=[
            pl.BlockSpec((ROW_TILE, D_MODEL), row),
            pl.BlockSpec((None, 1, D_MODEL), lay),
            pl.BlockSpec((None, 1, D_MODEL), grp),
            pl.BlockSpec((None, 1, D_MODEL), grp),
        ],
        out_specs=pl.BlockSpec((ROW_TILE, D_MODEL), row),
        out_shape=jax.ShapeDtypeStruct((N_TOK, D_MODEL), _bf16),
        compiler_params=pltpu.CompilerParams(dimension_semantics=("arbitrary",), vmem_limit_bytes=VMEM_LIMIT),
        name="norm_mod",
    )(x, norm_g.reshape(DEPTH, 1, D_MODEL), sc, sh)


def _in_proj_kernel(h_ref, w_ref, o_ref, w_s):
    @pl.when(pl.program_id(1) == 0)
    def _():
        w_s[...] = w_ref[...].astype(_bf16)

    o_ref[...] = jnp.dot(h_ref[...], w_s[...], preferred_element_type=_f32)


def _in_proj(h, w_in, layer):
    return pl.pallas_call(
        _in_proj_kernel,
        grid=(IN_COLS // PROJ_TN, N_TOK // PROJ_TM),
        in_specs=[
            pl.BlockSpec((PROJ_TM, D_MODEL), lambda j, i: (i, 0)),
            pl.BlockSpec((None, D_MODEL, PROJ_TN), lambda j, i: (layer, 0, j)),
        ],
        out_specs=pl.BlockSpec((PROJ_TM, PROJ_TN), lambda j, i: (i, j)),
        out_shape=jax.ShapeDtypeStruct((N_TOK, IN_COLS), _f32),
        scratch_shapes=[pltpu.VMEM((D_MODEL, PROJ_TN), _bf16)],
        compiler_params=pltpu.CompilerParams(dimension_semantics=("arbitrary", "arbitrary"),
                                             vmem_limit_bytes=VMEM_LIMIT),
        name="in_proj",
    )(h, w_in)


def _merge_kernel(a_ref, h_ref, s_ref, ga_ref, gh_ref, gs_ref, wa_ref, wh_ref, ws_ref, o_ref, wa_s, wh_s, ws_s):
    @pl.when(pl.program_id(1) == 0)
    def _():
        wa_s[...] = wa_ref[...].astype(_bf16)
        wh_s[...] = wh_ref[...].astype(_bf16)
        ws_s[...] = ws_ref[...].astype(_bf16)

    y = (jax.nn.sigmoid(ga_ref[...]) * jnp.dot(a_ref[...], wa_s[...], preferred_element_type=_f32)
         + jax.nn.sigmoid(gh_ref[...]) * jnp.dot(h_ref[...], wh_s[...], preferred_element_type=_f32)
         + jax.nn.sigmoid(gs_ref[...]) * jnp.dot(s_ref[...], ws_s[...], preferred_element_type=_f32))
    o_ref[...] = y.astype(_bf16)


def _merge(a_out, h_out, s_out, z, w_a, w_h, w_s, layer):
    ga0, gh0, gs0 = (sum(IN_SIZES[:k]) // MERGE_TN for k in (9, 10, 11))
    act = lambda j, i: (i, 0)
    wmap = lambda j, i: (layer, 0, j)
    return pl.pallas_call(
        _merge_kernel,
        grid=(D_MODEL // MERGE_TN, N_TOK // ROW_TILE),
        in_specs=[
            pl.BlockSpec((ROW_TILE, ATT_Q_W), act),
            pl.BlockSpec((ROW_TILE, HG_W), act),
            pl.BlockSpec((ROW_TILE, S5_W), act),
            pl.BlockSpec((ROW_TILE, MERGE_TN), lambda j, i: (i, ga0 + j)),
            pl.BlockSpec((ROW_TILE, MERGE_TN), lambda j, i: (i, gh0 + j)),
            pl.BlockSpec((ROW_TILE, MERGE_TN), lambda j, i: (i, gs0 + j)),
            pl.BlockSpec((None, ATT_Q_W, MERGE_TN), wmap),
            pl.BlockSpec((None, HG_W, MERGE_TN), wmap),
            pl.BlockSpec((None, S5_W, MERGE_TN), wmap),
        ],
        out_specs=pl.BlockSpec((ROW_TILE, MERGE_TN), lambda j, i: (i, j)),
        out_shape=jax.ShapeDtypeStruct((N_TOK, D_MODEL), _bf16),
        scratch_shapes=[pltpu.VMEM((ATT_Q_W, MERGE_TN), _bf16), pltpu.VMEM((HG_W, MERGE_TN), _bf16),
                        pltpu.VMEM((S5_W, MERGE_TN), _bf16)],
        compiler_params=pltpu.CompilerParams(dimension_semantics=("arbitrary", "arbitrary"),
                                             vmem_limit_bytes=VMEM_LIMIT),
        name="merge",
    )(a_out, h_out, s_out, z, z, z, w_a, w_h, w_s)


def _out_proj_kernel(y_ref, w_ref, x_ref, gt_ref, o_ref, w_s):
    @pl.when(pl.program_id(1) == 0)
    def _():
        w_s[...] = w_ref[...].astype(_bf16)

    o_ref[...] = x_ref[...] + gt_ref[...] * jnp.dot(y_ref[...], w_s[...], preferred_element_type=_f32)


def _out_proj(y, w_out, x, gt, layer):
    return pl.pallas_call(
        _out_proj_kernel,
        grid=(D_MODEL // MERGE_TN, N_TOK // ROW_TILE),
        in_specs=[
            pl.BlockSpec((ROW_TILE, D_MODEL), lambda j, i: (i, 0)),
            pl.BlockSpec((None, D_MODEL, MERGE_TN), lambda j, i: (layer, 0, j)),
            pl.BlockSpec((ROW_TILE, MERGE_TN), lambda j, i: (i, j)),
            pl.BlockSpec((None, 1, MERGE_TN), lambda j, i: (_group_of_tile(i), 0, j)),
        ],
        out_specs=pl.BlockSpec((ROW_TILE, MERGE_TN), lambda j, i: (i, j)),
        out_shape=jax.ShapeDtypeStruct((N_TOK, D_MODEL), _f32),
        scratch_shapes=[pltpu.VMEM((D_MODEL, MERGE_TN), _bf16)],
        compiler_params=pltpu.CompilerParams(dimension_semantics=("arbitrary", "arbitrary"),
                                             vmem_limit_bytes=VMEM_LIMIT),
        name="out_proj",
    )(y, w_out, x, gt)


COL_Q, COL_K, COL_V, COL_HQ, COL_HFF, COL_HFB, COL_HI, COL_HG, COL_SU = (sum(IN_SIZES[:k]) for k in range(9))
KV_GROUP_W = Q_GROUP * HEAD_DIM


def _head_norm(x, g):
    return x * lax.rsqrt(jnp.mean(x * x, axis=-1, keepdims=True) + EPS) * g


def _dot_nt(a, b):
    return lax.dot_general(a, b, (((1,), (1,)), ((), ())), preferred_element_type=_f32)


def _softmax_with_sink(s, sink):
    m = jnp.maximum(jnp.max(s, axis=-1, keepdims=True), sink)
    p = jnp.exp(s - m)
    den = jnp.sum(p, axis=-1, keepdims=True) + jnp.exp(sink - m)
    return p / den


def _ctx_attn_kernel(q_ref, k_ref, v_ref, qg_ref, kg_ref, sink_ref, o_ref, ko_ref, vo_ref):
    kn = _head_norm(k_ref[...], kg_ref[...])
    v = v_ref[...]
    ko_ref[...] = kn
    vo_ref[...] = v
    knb = kn.astype(_bf16)
    vb = v.astype(_bf16)
    for g in range(Q_GROUP):
        cols = slice(g * HEAD_DIM, (g + 1) * HEAD_DIM)
        qn = _head_norm(q_ref[:, cols], qg_ref[...])
        s = _dot_nt(qn.astype(_bf16), knb) * HEAD_DIM ** -0.5
        p = _softmax_with_sink(s, sink_ref[0:1, g:g + 1])
        o_ref[:, cols] = jnp.dot(p.astype(_bf16), vb, preferred_element_type=_f32).astype(_bf16)


def _ctx_attention(z, q_g, k_g, sink, layer):
    lay = lambda b, j: (layer, 0, 0)
    return pl.pallas_call(
        _ctx_attn_kernel,
        grid=(BATCH, N_KV_HEADS),
        in_specs=[
            pl.BlockSpec((SEQ, KV_GROUP_W), lambda b, j: (b, COL_Q // KV_GROUP_W + j)),
            pl.BlockSpec((SEQ, HEAD_DIM), lambda b, j: (b, COL_K // HEAD_DIM + j)),
            pl.BlockSpec((SEQ, HEAD_DIM), lambda b, j: (b, COL_V // HEAD_DIM + j)),
            pl.BlockSpec((None, 1, HEAD_DIM), lay),
            pl.BlockSpec((None, 1, HEAD_DIM), lay),
            pl.BlockSpec((None, None, 1, Q_GROUP), lambda b, j: (layer, j, 0, 0)),
        ],
        out_specs=[
            pl.BlockSpec((SEQ, KV_GROUP_W), lambda b, j: (b, j)),
            pl.BlockSpec((SEQ, HEAD_DIM), lambda b, j: (b, j)),
            pl.BlockSpec((SEQ, HEAD_DIM), lambda b, j: (b, j)),
        ],
        out_shape=[
            jax.ShapeDtypeStruct((N_CTX, ATT_Q_W), _bf16),
            jax.ShapeDtypeStruct((N_CTX, ATT_KV_W), _f32),
            jax.ShapeDtypeStruct((N_CTX, ATT_KV_W), _f32),
        ],
        compiler_params=pltpu.CompilerParams(dimension_semantics=("arbitrary", "arbitrary"),
                                             vmem_limit_bytes=VMEM_LIMIT),
        name="ctx_attention",
    )(z, z, z, q_g.reshape(DEPTH, 1, HEAD_DIM), k_g.reshape(DEPTH, 1, HEAD_DIM),
      sink.reshape(DEPTH, N_KV_HEADS, 1, Q_GROUP))


def _rope_tables():
    half = HEAD_DIM // 2
    nf = half // 2
    t = np.arange(DEC_SEQ)
    pos = np.stack([t // GRID_W, t % GRID_W], axis=1).astype(np.float64)
    inv = ROPE_BASE ** (-np.arange(nf, dtype=np.float64) / nf)
    dim = np.arange(HEAD_DIM)
    ang = pos[:, dim // half] * inv[dim % nf][None, :]
    sign = np.where((dim % half) < nf, -1.0, 1.0)
    return jnp.asarray(np.cos(ang), _f32), jnp.asarray(np.sin(ang) * sign, _f32)


def _rope(x, cos, sin_signed):
    nf = HEAD_DIM // 4
    lane = lax.broadcasted_iota(jnp.int32, x.shape, 1)
    partner = jnp.where((lane % (2 * nf)) < nf, pltpu.roll(x, HEAD_DIM - nf, 1), pltpu.roll(x, nf, 1))
    return x * cos + partner * sin_signed


def _win_attn_kernel(q_ref, kp_ref, kc_ref, kn_ref, vp_ref, vc_ref, vn_ref, cq_ref, sq_ref, cp_ref, sp_ref,
                     cn_ref, sn_ref, ck_ref, cv_ref, qg_ref, kg_ref, sink_ref, o_ref):
    n = pl.program_id(1)
    nb = pl.num_programs(1)
    kg = kg_ref[...]
    k_all = jnp.concatenate([
        _rope(_head_norm(kp_ref[...], kg), cp_ref[...], sp_ref[...]),
        _rope(_head_norm(kc_ref[...], kg), cq_ref[...], sq_ref[...]),
        _rope(_head_norm(kn_ref[...], kg), cn_ref[...], sn_ref[...]),
        ck_ref[...]], axis=0).astype(_bf16)
    v_all = jnp.concatenate([vp_ref[...], vc_ref[...], vn_ref[...], cv_ref[...]], axis=0).astype(_bf16)
    r = lax.broadcasted_iota(jnp.int32, (ATTN_BLOCK, ATTN_BLOCK), 0)
    c = lax.broadcasted_iota(jnp.int32, (ATTN_BLOCK, ATTN_BLOCK), 1)
    neg = jnp.full((ATTN_BLOCK, ATTN_BLOCK), -jnp.inf, _f32)
    zero = jnp.zeros((ATTN_BLOCK, ATTN_BLOCK), _f32)
    bias = jnp.concatenate([
        jnp.where(n > 0, jnp.where(c >= r, zero, neg), neg),
        zero,
        jnp.where(n < nb - 1, jnp.where(c <= r, zero, neg), neg),
        jnp.zeros((ATTN_BLOCK, ck_ref.shape[0]), _f32)], axis=1)
    for g in range(Q_GROUP):
        cols = slice(g * HEAD_DIM, (g + 1) * HEAD_DIM)
        qn = _rope(_head_norm(q_ref[:, cols], qg_ref[...]), cq_ref[...], sq_ref[...])
        s = _dot_nt(qn.astype(_bf16), k_all) * HEAD_DIM ** -0.5 + bias
        p = _softmax_with_sink(s, sink_ref[0:1, g:g + 1])
        o_ref[:, cols] = jnp.dot(p.astype(_bf16), v_all, preferred_element_type=_f32).astype(_bf16)


def _win_attention(z, cache_k, cache_v, q_g, k_g, sink, layer):
    cos, sin = _rope_tables()
    nb = DEC_SEQ // ATTN_BLOCK
    base = N_CTX // ATTN_BLOCK
    prev = lambda n: jnp.maximum(n - 1, 0)
    nxt = lambda n: jnp.minimum(n + 1, nb - 1)
    past = cache_k.shape[3]

    def zrow(sel, col0):
        return pl.BlockSpec((ATTN_BLOCK, HEAD_DIM), lambda b, n, j: (base + b * nb + sel(n), col0 // HEAD_DIM + j))

    def tab(sel):
        return pl.BlockSpec((ATTN_BLOCK, HEAD_DIM), lambda b, n, j: (sel(n), 0))

    same = lambda n: n
    lay = lambda b, n, j: (layer, 0, 0)
    cache = pl.BlockSpec((None, None, None, past, HEAD_DIM), lambda b, n, j: (b, layer, j, 0, 0))
    return pl.pallas_call(
        _win_attn_kernel,
        grid=(DEC_BATCH, nb, N_KV_HEADS),
        in_specs=[
            pl.BlockSpec((ATTN_BLOCK, KV_GROUP_W), lambda b, n, j: (base + b * nb + n, COL_Q // KV_GROUP_W + j)),
            zrow(prev, COL_K), zrow(same, COL_K), zrow(nxt, COL_K),
            zrow(prev, COL_V), zrow(same, COL_V), zrow(nxt, COL_V),
            tab(same), tab(same), tab(prev), tab(prev), tab(nxt), tab(nxt),
            cache, cache,
            pl.BlockSpec((None, 1, HEAD_DIM), lay),
            pl.BlockSpec((None, 1, HEAD_DIM), lay),
            pl.BlockSpec((None, None, 1, Q_GROUP), lambda b, n, j: (layer, j, 0, 0)),
        ],
        out_specs=pl.BlockSpec((ATTN_BLOCK, KV_GROUP_W), lambda b, n, j: (b * nb + n, j)),
        out_shape=jax.ShapeDtypeStruct((N_LAT, ATT_Q_W), _bf16),
        compiler_params=pltpu.CompilerParams(dimension_semantics=("arbitrary", "arbitrary", "arbitrary"),
                                             vmem_limit_bytes=VMEM_LIMIT),
        name="win_attention",
    )(z, z, z, z, z, z, z, cos, sin, cos, sin, cos, sin, cache_k, cache_v,
      q_g.reshape(DEPTH, 1, HEAD_DIM), k_g.reshape(DEPTH, 1, HEAD_DIM), sink.reshape(DEPTH, N_KV_HEADS, 1, Q_GROUP))


HG_BLOCK = 128
HG_LEVELS = 7
HG_NSEG = HG_LEVELS + 2
HG_HEADS_PER_STEP = 4


def _hgrn_constants():
    c = HG_BLOCK
    i = np.arange(c)
    seg = np.zeros((HG_NSEG, c, c), np.float32)
    mask = np.zeros((HG_LEVELS + 1, c, c), np.float32)
    mask[0] = np.eye(c)
    for lv in range(HG_LEVELS):
        b = 2 << lv
        h = b // 2
        mid = (i // b) * b + h
        second = (i % b) >= h
        for r in range(c):
            if second[r]:
                seg[lv, r, mid[r]:r + 1] = 1.0
            else:
                seg[lv, r, r + 1:mid[r]] = 1.0
        same = (i[:, None] // b) == (i[None, :] // b)
        mask[lv + 1] = same & second[:, None] & ~second[None, :]
    seg[HG_LEVELS] = np.tril(np.ones((c, c)))
    seg[HG_LEVELS + 1] = np.triu(np.ones((c, c)), 1)
    seg2 = np.stack([seg, seg[:, ::-1, ::-1]]).reshape(2, HG_NSEG * c, c)
    mask2 = np.stack([mask, mask[:, ::-1, ::-1]])
    return jnp.asarray(seg2, _bf16), jnp.asarray(mask2, _f32)


def _hgrn_schedule():
    rows, first, last, seq = [[], []], [], [], []
    for s in range(BATCH + DEC_BATCH):
        n = (SEQ if s < BATCH else DEC_SEQ) // HG_BLOCK
        base = s * (SEQ // HG_BLOCK) if s < BATCH else N_CTX // HG_BLOCK + (s - BATCH) * (DEC_SEQ // HG_BLOCK)
        for cpos in range(n):
            rows[0].append(base + cpos)
            rows[1].append(base + n - 1 - cpos)
            first.append(int(cpos == 0))
            last.append(int(cpos == n - 1))
            seq.append(s)
    as_i32 = lambda a: jnp.asarray(np.asarray(a, np.int32).reshape(-1))
    return as_i32(rows), as_i32(first), as_i32(last), as_i32(seq), len(seq)


def _hgrn_kernel(rows_ref, first_ref, last_ref, seq_ref, q_ref, f_ref, i_ref, lb_ref, seg_ref, mask_ref, s0_ref,
                 o_ref, sf_ref, st_s):
    job = pl.program_id(2)
    seg = seg_ref[...]

    @pl.when(first_ref[job] == 1)
    def _():
        for hh in range(HG_HEADS_PER_STEP):
            st_s[hh] = s0_ref[hh].T

    for hh in range(HG_HEADS_PER_STEP):
        cols = slice(hh * HG_DK, (hh + 1) * HG_DK)
        xq = q_ref[:, cols]
        q = xq * jax.nn.sigmoid(xq) * HG_DK ** -0.5
        xf = f_ref[:, cols]
        log_sig = jnp.minimum(xf, 0.0) - jnp.log(1.0 + jnp.exp(-jnp.abs(xf)))
        lb = lb_ref[hh]
        f = lb + (1.0 - lb) * jnp.exp(log_sig)
        log_f = jnp.where(lb > 0.0, jnp.log(f), log_sig)
        k = 1.0 - f
        v = i_ref[:, cols]
        vb = v.astype(_bf16)
        lf_hi, lf_lo = _split_bf16(log_f)
        decay = jnp.exp(jnp.dot(seg, lf_hi, preferred_element_type=_f32)
                        + jnp.dot(seg, lf_lo, preferred_element_type=_f32))
        scores = _dot_nt(q.astype(_bf16), k.astype(_bf16)) * mask_ref[0]
        for lv in range(HG_LEVELS):
            e = decay[lv * HG_BLOCK:(lv + 1) * HG_BLOCK]
            scores = scores + _dot_nt((q * e).astype(_bf16), (k * e).astype(_bf16)) * mask_ref[lv + 1]
        e_q = decay[HG_LEVELS * HG_BLOCK:(HG_LEVELS + 1) * HG_BLOCK]
        e_k = decay[(HG_LEVELS + 1) * HG_BLOCK:(HG_LEVELS + 2) * HG_BLOCK]
        st = st_s[hh]
        o_ref[:, cols] = (jnp.dot(scores.astype(_bf16), vb, preferred_element_type=_f32)
                          + _dot_nt((q * e_q).astype(_bf16), st.astype(_bf16)))
        total = jnp.sum(log_f, axis=0, keepdims=True)
        st_new = jnp.exp(total) * st + jnp.dot(v.T.astype(_bf16), (k * e_k).astype(_bf16),
                                               preferred_element_type=_f32)
        st_s[hh] = st_new

    @pl.when(last_ref[job] == 1)
    def _():
        for hh in range(HG_HEADS_PER_STEP):
            sf_ref[hh] = st_s[hh].T


def _hgrn(z, lb, s0):
    seg, mask = _hgrn_constants()
    rows, first, last, seq, n_jobs = _hgrn_schedule()

    hps = HG_HEADS_PER_STEP
    step_w = hps * HG_DK

    def zcol(col0):
        return pl.BlockSpec((HG_BLOCK, step_w),
                            lambda h, d, j, r, f, l, s: (r[d * n_jobs + j], col0 // step_w + h))

    state = pl.BlockSpec((None, None, hps, HG_DK, HG_DV), lambda h, d, j, r, f, l, s: (s[j], d, h, 0, 0))
    grid_spec = pltpu.PrefetchScalarGridSpec(
        num_scalar_prefetch=4,
        grid=(HG_HEADS // hps, 2, n_jobs),
        in_specs=[
            zcol(COL_HQ),
            pl.BlockSpec((HG_BLOCK, step_w),
                         lambda h, d, j, r, f, l, s: (r[d * n_jobs + j], (COL_HFF + d * HG_W) // step_w + h)),
            zcol(COL_HI),
            pl.BlockSpec((None, hps, 1, HG_DK), lambda h, d, j, r, f, l, s: (d, h, 0, 0)),
            pl.BlockSpec((None, HG_NSEG * HG_BLOCK, HG_BLOCK), lambda h, d, j, r, f, l, s: (d, 0, 0)),
            pl.BlockSpec((None, HG_LEVELS + 1, HG_BLOCK, HG_BLOCK), lambda h, d, j, r, f, l, s: (d, 0, 0, 0)),
            state,
        ],
        out_specs=[
            pl.BlockSpec((None, HG_BLOCK, step_w), lambda h, d, j, r, f, l, s: (d, r[d * n_jobs + j], h)),
            state,
        ],
        scratch_shapes=[pltpu.VMEM((hps, HG_DV, HG_DK), _f32)],
    )
    return pl.pallas_call(
        _hgrn_kernel,
        grid_spec=grid_spec,
        out_shape=[
            jax.ShapeDtypeStruct((2, N_TOK, HG_W), _f32),
            jax.ShapeDtypeStruct(s0.shape, _f32),
        ],
        compiler_params=pltpu.CompilerParams(dimension_semantics=("arbitrary", "arbitrary", "arbitrary"),
                                             vmem_limit_bytes=VMEM_LIMIT),
        name="hgrn",
    )(rows, first, last, seq, z, z, z, lb.reshape(2, HG_HEADS, 1, HG_DK), seg, mask, s0)


def _hgrn_out_kernel(of_ref, ob_ref, g_ref, ng_ref, o_ref):
    ng = ng_ref[...]
    for hh in range(HG_OUT_W // HG_DV):
        cols = slice(hh * HG_DV, (hh + 1) * HG_DV)
        o = _head_norm(of_ref[:, cols] + ob_ref[:, cols], ng)
        xg = g_ref[:, cols]
        o_ref[:, cols] = (o * (xg * jax.nn.sigmoid(xg))).astype(_bf16)


def _hgrn_out(o_dirs, z, norm_g, layer):
    return pl.pallas_call(
        _hgrn_out_kernel,
        grid=(N_TOK // ROW_TILE, HG_W // HG_OUT_W),
        in_specs=[
            pl.BlockSpec((None, ROW_TILE, HG_OUT_W), lambda i, j: (0, i, j)),
            pl.BlockSpec((None, ROW_TILE, HG_OUT_W), lambda i, j: (1, i, j)),
            pl.BlockSpec((ROW_TILE, HG_OUT_W), lambda i, j: (i, COL_HG // HG_OUT_W + j)),
            pl.BlockSpec((None, 1, HG_DV), lambda i, j: (layer, 0, 0)),
        ],
        out_specs=pl.BlockSpec((ROW_TILE, HG_OUT_W), lambda i, j: (i, j)),
        out_shape=jax.ShapeDtypeStruct((N_TOK, HG_W), _bf16),
        compiler_params=pltpu.CompilerParams(dimension_semantics=("arbitrary", "arbitrary"),
                                             vmem_limit_bytes=VMEM_LIMIT),
        name="hgrn_out",
    )(o_dirs, o_dirs, z, norm_g.reshape(DEPTH, 1, HG_DV))


def _rms_norm(x, g):
    xf = x.astype(_f32)
    y = xf * lax.rsqrt(jnp.mean(xf * xf, axis=-1, keepdims=True) + EPS)
    return (y * g.astype(_f32)).astype(x.dtype)


def _split_in(z):
    return jnp.split(z, np.cumsum(IN_SIZES)[:-1].tolist(), axis=-1)


def _attn_heads(q, k, v, q_g, k_g):
    b, t = q.shape[:2]
    q = _rms_norm(q.reshape(b, t, N_HEADS, HEAD_DIM), q_g)
    k = _rms_norm(k.reshape(b, t, N_KV_HEADS, HEAD_DIM), k_g)
    return q, k, v.reshape(b, t, N_KV_HEADS, HEAD_DIM)


def _rotate(x, pos):
    nf = x.shape[-1] // 2
    inv = ROPE_BASE ** (-jnp.arange(nf, dtype=_f32) / nf)
    ang = pos.astype(_f32)[:, None] * inv[None, :]
    cos = jnp.cos(ang)[None, :, None, :]
    sin = jnp.sin(ang)[None, :, None, :]
    x1, x2 = x[..., :nf], x[..., nf:]
    return jnp.concatenate([x1 * cos - x2 * sin, x1 * sin + x2 * cos], axis=-1)


def _axial_rope(x):
    t = x.shape[1]
    rows = t // GRID_W
    row = jnp.repeat(jnp.arange(rows), GRID_W)
    col = jnp.tile(jnp.arange(GRID_W), rows)
    half = HEAD_DIM // 2
    return jnp.concatenate([_rotate(x[..., :half], row), _rotate(x[..., half:], col)], axis=-1)


def _sink_column(sink, lead_shape):
    s = sink.astype(_f32).reshape(N_KV_HEADS, Q_GROUP, 1, 1)
    return jnp.broadcast_to(s, lead_shape + (1,))


def _context_attention(q, k, v, sink):
    b, l = q.shape[:2]
    scale = HEAD_DIM ** -0.5
    qb = q.reshape(b, l, N_KV_HEADS, Q_GROUP, HEAD_DIM)
    s = jnp.einsum('bqhgd,bshd->bhgqs', qb, k).astype(_f32) * scale
    s = jnp.concatenate([s, _sink_column(sink, s.shape[:-1])], axis=-1)
    pr = jax.nn.softmax(s, axis=-1)[..., :l]
    o = jnp.einsum('bhgqs,bshd->bqhgd', pr, v)
    return o.reshape(b, l, N_HEADS * HEAD_DIM)


def _window_attention(q, k, v, k_ctx, v_ctx, sink):
    b, t = q.shape[:2]
    nb = t // ATTN_BLOCK
    lc = k_ctx.shape[1]
    scale = HEAD_DIM ** -0.5
    qb = q.reshape(b, nb, ATTN_BLOCK, N_KV_HEADS, Q_GROUP, HEAD_DIM)
    pad = ((0, 0), (ATTN_BLOCK, ATTN_BLOCK), (0, 0), (0, 0))

    def bands(a):
        ap = jnp.pad(a, pad).reshape(b, nb + 2, ATTN_BLOCK, N_KV_HEADS, HEAD_DIM)
        return jnp.concatenate([ap[:, :-2], ap[:, 1:-1], ap[:, 2:]], axis=2)

    kb, vb = bands(k), bands(v)
    blk = jnp.arange(nb)[:, None] * ATTN_BLOCK
    qpos = blk + jnp.arange(ATTN_BLOCK)[None, :]
    kpos = blk - ATTN_BLOCK + jnp.arange(3 * ATTN_BLOCK)[None, :]
    rel = kpos[:, None, :] - qpos[:, :, None]
    valid = (jnp.abs(rel) <= WINDOW) & (kpos[:, None, :] >= 0) & (kpos[:, None, :] < t)
    s_loc = jnp.einsum('bnqhgd,bnshd->bnhgqs', qb, kb).astype(_f32) * scale
    s_loc = jnp.where(valid[None, :, None, None], s_loc, -jnp.inf)
    s_ctx = jnp.einsum('bnqhgd,bshd->bnhgqs', qb, k_ctx).astype(_f32) * scale
    s = jnp.concatenate([s_loc, s_ctx, _sink_column(sink, s_loc.shape[:-1])], axis=-1)
    pr = jax.nn.softmax(s, axis=-1)
    w = 3 * ATTN_BLOCK
    o = (jnp.einsum('bnhgqs,bnshd->bnqhgd', pr[..., :w], vb)
         + jnp.einsum('bnhgqs,bshd->bnqhgd', pr[..., w:w + lc], v_ctx))
    return o.reshape(b, t, N_HEADS * HEAD_DIM)


def _hgrn2_forget(f_pre, lb):
    x = f_pre.astype(_f32)
    log_f = jnp.logaddexp(jnp.log(lb), jnp.log1p(-lb) + jax.nn.log_sigmoid(x))
    return 1.0 - jnp.exp(log_f), log_f


def _gla_chunkwise(q, k, v, log_f, s0):
    b, t, h, dk = q.shape
    dv = v.shape[-1]
    nc = t // HG_CHUNK
    q, k, log_f = (a.reshape(b, nc, HG_CHUNK, h, dk) for a in (q, k, log_f))
    v = v.reshape(b, nc, HG_CHUNK, h, dv)
    cum = jnp.cumsum(log_f, axis=2)
    last = cum[:, :, -1]
    causal = jnp.tril(jnp.ones((HG_CHUNK, HG_CHUNK), dtype=bool))[None, None, :, :, None, None]
    diff = cum[:, :, :, None] - cum[:, :, None, :]
    decay = jnp.exp(jnp.where(causal, diff, -jnp.inf))
    scores = jnp.einsum('bctshd,bcthd,bcshd->bchts', decay, q, k)
    o_intra = jnp.einsum('bchts,bcshe->bcthe', scores, v)
    kv = jnp.einsum('bcshd,bcshe->bchde', k * jnp.exp(last[:, :, None] - cum), v)

    def step(state, inp):
        g_c, kv_c = inp
        return jnp.exp(g_c)[..., None] * state + kv_c, state

    s_final, s_prev = lax.scan(step, s0, (jnp.moveaxis(last, 1, 0), jnp.moveaxis(kv, 1, 0)))
    s_prev = jnp.moveaxis(s_prev, 0, 1)
    o_inter = jnp.einsum('bcthd,bchde->bcthe', q * jnp.exp(cum), s_prev)
    return (o_intra + o_inter).reshape(b, t, h, dv), s_final


def _hgrn2_bidir(hq, hff, hfb, hi, hg, lb, norm_g, s0):
    b, t = hq.shape[:2]

    def heads(a):
        return a.astype(_f32).reshape(b, t, HG_HEADS, HG_DK)

    def flip(a):
        return jnp.flip(a, axis=1)

    q = heads(jax.nn.silu(hq.astype(_f32)) * HG_DK ** -0.5)
    v = heads(hi)
    k_f, lf_f = _hgrn2_forget(hff, lb[0])
    k_b, lf_b = _hgrn2_forget(hfb, lb[1])
    s0 = s0.astype(_f32)
    o_f, s_f = _gla_chunkwise(q, heads(k_f), v, heads(lf_f), s0[:, 0])
    o_b, s_b = _gla_chunkwise(flip(q), flip(heads(k_b)), flip(v), flip(heads(lf_b)), s0[:, 1])
    o = _rms_norm(o_f + flip(o_b), norm_g) * jax.nn.silu(heads(hg))
    return o.reshape(b, t, HG_W), jnp.stack([s_f, s_b], axis=1)


def _s5_param_kernel(are_ref, aim_ref, ldt_ref, bre_ref, bim_ref, abre_ref, abim_ref, bpre_ref, bpim_ref):
    bre = bre_ref[...]
    bim = bim_ref[...]
    for d in range(2):
        lr = are_ref[d]
        li = aim_ref[d]
        dt = jnp.exp(ldt_ref[d])
        mag = jnp.exp(lr * dt)
        ar = mag * jnp.cos(li * dt)
        ai = mag * jnp.sin(li * dt)
        abre_ref[d] = ar
        abim_ref[d] = ai
        nr = ar - 1.0
        den = lr * lr + li * li
        gr = (nr * lr + ai * li) / den
        gi = (ai * lr - nr * li) / den
        gr3 = gr[:, None, :]
        gi3 = gi[:, None, :]
        bpre_ref[d] = gr3 * bre - gi3 * bim
        bpim_ref[d] = gr3 * bim + gi3 * bre


def _s5_params(a_re, a_im, log_dt, b_re, b_im, c_re, c_im):
    g, pst, ch = S5_GROUPS, S5_STATE, S5_CH
    abre, abim, bpre, bpim = pl.pallas_call(
        _s5_param_kernel,
        out_shape=[
            jax.ShapeDtypeStruct((2, g, pst), _f32),
            jax.ShapeDtypeStruct((2, g, pst), _f32),
            jax.ShapeDtypeStruct((2, g, ch, pst), _f32),
            jax.ShapeDtypeStruct((2, g, ch, pst), _f32),
        ],
        name="s5_params",
    )(a_re, a_im, log_dt.reshape(2, g, 1), b_re.transpose(0, 2, 1), b_im.transpose(0, 2, 1))
    eye = jnp.eye(S5_SLAB_GROUPS, dtype=_f32)

    def b_slabs(bp):
        bp = bp.reshape(2, S5_NSLAB, S5_SLAB_GROUPS, ch, pst)
        w = bp[:, :, :, :, None, :] * eye[None, None, :, None, :, None]
        return w.reshape(2, S5_NSLAB, S5_SLAB_CH, S5_SLAB_ST)

    def c_slabs(cm):
        cm = cm.reshape(S5_NSLAB, S5_SLAB_GROUPS, ch, pst).transpose(0, 1, 3, 2)
        w = cm[:, :, :, None, :] * eye[None, :, None, :, None]
        return w.reshape(S5_NSLAB, S5_SLAB_ST, S5_SLAB_CH)

    b_blk = jnp.concatenate([b_slabs(bpre), b_slabs(bpim)], axis=-1).astype(_bf16)
    return (abre.reshape(2, 1, g * pst), abim.reshape(2, 1, g * pst), b_blk,
            c_slabs(c_re).astype(_bf16), c_slabs(c_im).astype(_bf16))


def _s5_scan_kernel(u_ref, b_ref, cre_ref, cim_ref, are_ref, aim_ref, x0re_ref, x0im_ref,
                    y_ref, xfre_ref, xfim_ref, bu_s, zre_s, zim_s, *, tb_steps, rd):
    tb = pl.program_id(2)
    n_rg = rd // SUBLANES
    st = S5_SLAB_ST

    @pl.when(tb == 0)
    def _():
        zre_s[...] = x0re_ref[...]
        zim_s[...] = x0im_ref[...]

    u = u_ref[...].reshape(tb_steps * rd, S5_SLAB_CH).astype(_bf16)
    bu_s[...] = jnp.dot(u, b_ref[...], preferred_element_type=_f32).reshape(tb_steps, rd, 2 * st)
    are = jnp.broadcast_to(are_ref[...], (SUBLANES, st))
    aim = jnp.broadcast_to(aim_ref[...], (SUBLANES, st))

    def body(t, carry):
        new = []
        for rg in range(n_rg):
            zre, zim = carry[rg]
            rows = slice(rg * SUBLANES, (rg + 1) * SUBLANES)
            nre = are * zre - aim * zim + bu_s[t, rows, 0:st]
            nim = are * zim + aim * zre + bu_s[t, rows, st:2 * st]
            bu_s[t, rows, 0:st] = nre
            bu_s[t, rows, st:2 * st] = nim
            new.append((nre, nim))
        return tuple(new)

    init = tuple((zre_s[rg * SUBLANES:(rg + 1) * SUBLANES, :], zim_s[rg * SUBLANES:(rg + 1) * SUBLANES, :])
                 for rg in range(n_rg))
    fin = lax.fori_loop(0, tb_steps, body, init, unroll=4)
    for rg in range(n_rg):
        zre_s[rg * SUBLANES:(rg + 1) * SUBLANES, :] = fin[rg][0]
        zim_s[rg * SUBLANES:(rg + 1) * SUBLANES, :] = fin[rg][1]

    xs = bu_s[...].reshape(tb_steps * rd, 2 * st)
    y = (jnp.dot(xs[:, 0:st].astype(_bf16), cre_ref[...], preferred_element_type=_f32)
         - jnp.dot(xs[:, st:2 * st].astype(_bf16), cim_ref[...], preferred_element_type=_f32))
    y_ref[...] = y.reshape(tb_steps, rd, S5_SLAB_CH)

    @pl.when(tb == pl.num_programs(2) - 1)
    def _():
        xfre_ref[...] = zre_s[...]
        xfim_ref[...] = zim_s[...]


def _s5_scan(u, params, x0_re, x0_im):
    abre, abim, b_blk, c_re, c_im = params
    _, t_len, rd, _ = u.shape
    tb_steps = S5_BLOCK_ROWS // rd
    n_tb = t_len // tb_steps
    st = S5_SLAB_ST
    grid = (2, S5_NSLAB, n_tb)
    kern = functools.partial(_s5_scan_kernel, tb_steps=tb_steps, rd=rd)
    return pl.pallas_call(
        kern,
        grid=grid,
        in_specs=[
            pl.BlockSpec((None, tb_steps, rd, S5_SLAB_CH), lambda d, s, t: (d, t, 0, s)),
            pl.BlockSpec((None, None, S5_SLAB_CH, 2 * st), lambda d, s, t: (d, s, 0, 0)),
            pl.BlockSpec((None, st, S5_SLAB_CH), lambda d, s, t: (s, 0, 0)),
            pl.BlockSpec((None, st, S5_SLAB_CH), lambda d, s, t: (s, 0, 0)),
            pl.BlockSpec((None, 1, st), lambda d, s, t: (d, 0, s)),
            pl.BlockSpec((None, 1, st), lambda d, s, t: (d, 0, s)),
            pl.BlockSpec((None, rd, st), lambda d, s, t: (d, 0, s)),
            pl.BlockSpec((None, rd, st), lambda d, s, t: (d, 0, s)),
        ],
        out_specs=[
            pl.BlockSpec((None, tb_steps, rd, S5_SLAB_CH), lambda d, s, t: (d, t, 0, s)),
            pl.BlockSpec((None, rd, st), lambda d, s, t: (d, 0, s)),
            pl.BlockSpec((None, rd, st), lambda d, s, t: (d, 0, s)),
        ],
        out_shape=[
            jax.ShapeDtypeStruct((2, t_len, rd, S5_W), _f32),
            jax.ShapeDtypeStruct((2, rd, S5_GROUPS * S5_STATE), _f32),
            jax.ShapeDtypeStruct((2, rd, S5_GROUPS * S5_STATE), _f32),
        ],
        scratch_shapes=[
            pltpu.VMEM((tb_steps, rd, 2 * st), _f32),
            pltpu.VMEM((rd, st), _f32),
            pltpu.VMEM((rd, st), _f32),
        ],
        compiler_params=pltpu.CompilerParams(dimension_semantics=("arbitrary", "arbitrary", "arbitrary"),
                                             vmem_limit_bytes=VMEM_LIMIT),
        name="s5_scan",
    )(u, b_blk, c_re, c_im, abre, abim, x0_re, x0_im)


def _gelu_tanh(x):
    return 0.5 * x * (1.0 + jnp.tanh(math.sqrt(2.0 / math.pi) * (x + 0.044715 * (x * x * x))))


def _s5_out_kernel(yf_ref, yb_ref, u_ref, d_ref, w_ref, b_ref, o_ref, w_s):
    @pl.when(pl.program_id(0) == 0)
    def _():
        w_s[...] = w_ref[...].astype(_bf16)

    y = _gelu_tanh(yf_ref[...] + yb_ref[...] + d_ref[...] * u_ref[...])
    z = jnp.dot(y.astype(_bf16), w_s[...], preferred_element_type=_f32) + b_ref[...]
    o_ref[...] = (y * jax.nn.sigmoid(z)).astype(_bf16)


def _s5_out(yf, yb, u, d_vec, w_glu, b_glu, layer):
    nt = N_TOK // ROW_TILE
    row = lambda i: (i, 0)
    lay = lambda i: (layer, 0, 0)
    return pl.pallas_call(
        _s5_out_kernel,
        grid=(nt,),
        in_specs=[
            pl.BlockSpec((ROW_TILE, S5_W), row),
            pl.BlockSpec((ROW_TILE, S5_W), row),
            pl.BlockSpec((ROW_TILE, S5_W), row),
            pl.BlockSpec((None, 1, S5_W), lay),
            pl.BlockSpec((None, S5_W, S5_W), lay),
            pl.BlockSpec((None, 1, S5_W), lay),
        ],
        out_specs=pl.BlockSpec((ROW_TILE, S5_W), row),
        out_shape=jax.ShapeDtypeStruct((N_TOK, S5_W), _bf16),
        scratch_shapes=[pltpu.VMEM((S5_W, S5_W), _bf16)],
        compiler_params=pltpu.CompilerParams(dimension_semantics=("arbitrary",), vmem_limit_bytes=VMEM_LIMIT),
        name="s5_out",
    )(yf, yb, u, d_vec.reshape(DEPTH, 1, S5_W), w_glu, b_glu.reshape(DEPTH, 1, S5_W))


def _s5_branch(su_c, su_l, sp, s5_re0, s5_im0, layer):
    params = _s5_params(sp['s5_a_re'][layer], sp['s5_a_im'][layer], sp['s5_log_dt'][layer], sp['s5_b_re'][layer],
                        sp['s5_b_im'][layer], sp['s5_c_re'][layer], sp['s5_c_im'][layer])
    nst = S5_GROUPS * S5_STATE

    def both_dirs(u_tm):
        return jnp.stack([u_tm, u_tm[::-1]], axis=0)

    def merge_dirs(y):
        return y[0].transpose(1, 0, 2), y[1, ::-1].transpose(1, 0, 2)

    zeros = jnp.zeros((2, BATCH, nst), _f32)
    y_c, xf_re, xf_im = _s5_scan(both_dirs(su_c.transpose(1, 0, 2)), params, zeros, zeros)
    yf_c, yb_c = merge_dirs(y_c)
    pad = ((0, 0), (0, SUBLANES - DEC_BATCH), (0, 0))
    u_l = both_dirs(jnp.pad(su_l.transpose(1, 0, 2), pad))

    def lat_state(x0):
        return jnp.pad(x0.transpose(1, 0, 2, 3).reshape(2, DEC_BATCH, nst), pad)

    y_l, _, _ = _s5_scan(u_l, params, lat_state(s5_re0), lat_state(s5_im0))
    yf_l, yb_l = merge_dirs(y_l[:, :, :DEC_BATCH])
    yf = jnp.concatenate([yf_c.reshape(N_CTX, S5_W), yf_l.reshape(N_LAT, S5_W)], axis=0)
    yb = jnp.concatenate([yb_c.reshape(N_CTX, S5_W), yb_l.reshape(N_LAT, S5_W)], axis=0)
    u = jnp.concatenate([su_c.reshape(N_CTX, S5_W), su_l.reshape(N_LAT, S5_W)], axis=0)
    s_out = _s5_out(yf, yb, u, sp['s5_d'], sp['s5_w_glu'], sp['s5_b_glu'], layer)

    def ctx_state(xf):
        return xf.reshape(2, BATCH, S5_GROUPS, S5_STATE).transpose(1, 0, 2, 3)

    return s_out, ctx_state(xf_re), ctx_state(xf_im)


def _merge_branches(a_out, h_out, s_out, ga, gh, gs, p):
    y = (jax.nn.sigmoid(ga) * (a_out @ p['w_br_attn'])
         + jax.nn.sigmoid(gh) * (h_out @ p['w_br_hg'])
         + jax.nn.sigmoid(gs) * (s_out @ p['w_br_s5']))
    return y @ p['w_out']


def _legacy_mixer_sublayer(xc, xl, mods, pl_, sp, lb, ck, cv, s_hg0, s5_re0, s5_im0, layer):
    sh1, sc1, gt1 = mods[0], mods[1], mods[2]

    def pre(x, sh, sc):
        h = _rms_norm(x, pl_['norm1_g']) * (1.0 + sc) + sh
        return _split_in(h @ pl_['w_in'])

    zc = pre(xc, sh1[0], sc1[0])
    zl = pre(xl, sh1[1:], sc1[1:])
    s_out, s5_re, s5_im = _s5_branch(zc[8], zl[8], sp, s5_re0, s5_im0, layer)
    s_out_c = s_out[:N_CTX].reshape(BATCH, SEQ, S5_W)
    s_out_l = s_out[N_CTX:].reshape(DEC_BATCH, DEC_SEQ, S5_W)
    q, k, v, hq, hff, hfb, hi, hg, _, ga, gh, gs = zc
    q, k, v = _attn_heads(q, k, v, pl_['q_norm_g'], pl_['k_norm_g'])
    a_out = _context_attention(q, k, v, pl_['sink'])
    z_hg = jnp.zeros((BATCH, 2, HG_HEADS, HG_DK, HG_DV), _f32)
    h_out, s_hg = _hgrn2_bidir(hq, hff, hfb, hi, hg, lb, pl_['hg_norm_g'], z_hg)
    xc = xc + gt1[0] * _merge_branches(a_out, h_out, s_out_c, ga, gh, gs, pl_)
    new = (k, v, s_hg, s5_re, s5_im)
    q, k, v, hq, hff, hfb, hi, hg, _, ga, gh, gs = zl
    q, k, v = _attn_heads(q, k, v, pl_['q_norm_g'], pl_['k_norm_g'])
    a_out = _window_attention(_axial_rope(q), _axial_rope(k), v, ck, cv, pl_['sink'])
    h_out, _ = _hgrn2_bidir(hq, hff, hfb, hi, hg, lb, pl_['hg_norm_g'], s_hg0)
    xl = xl + gt1[1:] * _merge_branches(a_out, h_out, s_out_l, ga, gh, gs, pl_)
    return xc, xl, new


def kernel(x_prompt, x_sample, cache_k, cache_v, state_hgrn, state_s5_re, state_s5_im, c, c_ctx, w_mod, b_mod, norm1_g, norm2_g, w_in, q_norm_g, k_norm_g, attn_sink, hg_lb_logits, hg_norm_g, s5_a_re, s5_a_im, s5_log_dt, s5_b_re, s5_b_im, s5_c_re, s5_c_im, s5_d, s5_w_glu, s5_b_glu, w_br_attn, w_br_hg, w_br_s5, w_out, router_w, router_b, exp_w1, exp_b1, exp_w2, exp_b2):
    lb_all = jnp.cumsum(jax.nn.softmax(hg_lb_logits.astype(_f32), axis=0), axis=0)
    lb_all = lb_all - lb_all[:1]
    cond_rows = jnp.concatenate([c_ctx[None, :], c, jnp.zeros((SUBLANES - N_GROUPS, D_MODEL), _f32)], axis=0)
    x = jnp.concatenate([x_prompt.reshape(N_CTX, D_MODEL), x_sample.reshape(N_LAT, D_MODEL)], axis=0)
    cache_k_t = cache_k.transpose(0, 1, 3, 2, 4)
    cache_v_t = cache_v.transpose(0, 1, 3, 2, 4)
    zero_state = jnp.zeros((BATCH, 2, HG_HEADS, HG_DK, HG_DV), _f32)
    new_k, new_v, new_hg, new_re, new_im = [], [], [], [], []
    moe_p = {"norm2_g": norm2_g, "router_w": router_w, "router_b": router_b, "exp_w1": exp_w1, "exp_b1": exp_b1,
             "exp_w2": exp_w2, "exp_b2": exp_b2}
    s5_p = {'s5_a_re': s5_a_re, 's5_a_im': s5_a_im, 's5_log_dt': s5_log_dt, 's5_b_re': s5_b_re, 's5_b_im': s5_b_im,
            's5_c_re': s5_c_re, 's5_c_im': s5_c_im, 's5_d': s5_d, 's5_w_glu': s5_w_glu, 's5_b_glu': s5_b_glu}
    for l in range(DEPTH):
        m = _modulation(cond_rows, w_mod, b_mod, l)[:N_GROUPS]
        mods = [a[:, None, :] for a in jnp.split(m, N_MOD, axis=-1)]
        sh1, sc1, gt1 = mods[0], mods[1], mods[2]
        z = _in_proj(_norm_mod(x, norm1_g, sc1, sh1, l), w_in, l)
        a_ctx, k_l, v_l = _ctx_attention(z, q_norm_g, k_norm_g, attn_sink, l)
        a_lat = _win_attention(z, cache_k_t, cache_v_t, q_norm_g, k_norm_g, attn_sink, l)
        a_out = jnp.concatenate([a_ctx, a_lat], axis=0)
        o_dirs, hg_fin = _hgrn(z, lb_all[l], jnp.concatenate([zero_state, state_hgrn[:, l]], axis=0))
        h_out = _hgrn_out(o_dirs, z, hg_norm_g, l)
        su = z[:, COL_SU:COL_SU + S5_W]
        s_out, re_l, im_l = _s5_branch(su[:N_CTX].reshape(BATCH, SEQ, S5_W),
                                       su[N_CTX:].reshape(DEC_BATCH, DEC_SEQ, S5_W), s5_p,
                                       state_s5_re[:, l], state_s5_im[:, l], l)
        y = _merge(a_out, h_out, s_out, z, w_br_attn, w_br_hg, w_br_s5, l)
        x = _out_proj(y, w_out, x, gt1, l)
        new_k.append(k_l.reshape(BATCH, SEQ, N_KV_HEADS, HEAD_DIM))
        new_v.append(v_l.reshape(BATCH, SEQ, N_KV_HEADS, HEAD_DIM))
        new_hg.append(hg_fin[:BATCH])
        new_re.append(re_l)
        new_im.append(im_l)
        x = _moe_sublayer(x, mods, moe_p, l)
    return (x[:N_CTX].reshape(BATCH, SEQ, D_MODEL), x[N_CTX:].reshape(DEC_BATCH, DEC_SEQ, D_MODEL),
            jnp.stack(new_k, axis=1), jnp.stack(new_v, axis=1), jnp.stack(new_hg, axis=1),
            jnp.stack(new_re, axis=1), jnp.stack(new_im, axis=1))
```

```python
import functools
import math

import jax
import jax.numpy as jnp
import numpy as np
from jax import lax
from jax.experimental import pallas as pl
from jax.experimental.pallas import tpu as pltpu

D_MODEL = 2048
BATCH = 16
SEQ = 256
DEPTH = 2
DEC_BATCH = 2
DEC_SEQ = 2048
GRID_W = 64
N_HEADS = 8
N_KV_HEADS = 2
HEAD_DIM = 128
Q_GROUP = N_HEADS // N_KV_HEADS
WINDOW = 128
ATTN_BLOCK = 128
ROPE_BASE = 10000.0
ATT_Q_W = N_HEADS * HEAD_DIM
ATT_KV_W = N_KV_HEADS * HEAD_DIM
HG_HEADS = 8
HG_DK = 128
HG_DV = 128
HG_W = HG_HEADS * HG_DK
HG_CHUNK = 16
S5_W = 1024
S5_CH = 16
S5_GROUPS = S5_W // S5_CH
S5_STATE = 64
N_EXPERTS = 32
TOP_K = 4
D_FF = 2048
SWIGLU_ALPHA = 1.702
SWIGLU_LIMIT = 7.0
N_MOD = 6
EPS = 1e-6
IN_SIZES = (ATT_Q_W, ATT_KV_W, ATT_KV_W, HG_W, HG_W, HG_W, HG_W, HG_W, S5_W, D_MODEL, D_MODEL, D_MODEL)
IN_COLS = sum(IN_SIZES)

N_CTX = BATCH * SEQ
N_LAT = DEC_BATCH * DEC_SEQ
N_TOK = N_CTX + N_LAT
N_GROUPS = 1 + DEC_BATCH

LANES = 128
ROW_TILE = 256
VMEM_LIMIT = 56 * 1024 * 1024

MOE_SUPER = 1024
MOE_SUB = 256
MOE_FC = 512
MOE_NC = D_FF // MOE_FC
MOE_NS = (N_TOK * TOP_K) // MOE_SUPER + N_EXPERTS
MOD_TN = 1536
PROJ_TN = 1536
PROJ_TM = 512
MERGE_TN = 512
HG_OUT_W = 512
DISPATCH_TILE = 256
DISPATCH_TOKENS = 64
COMBINE_TILE = 64

S5_SLAB_GROUPS = 8
S5_NSLAB = S5_GROUPS // S5_SLAB_GROUPS
S5_SLAB_CH = S5_SLAB_GROUPS * S5_CH
S5_SLAB_ST = S5_SLAB_GROUPS * S5_STATE
S5_BLOCK_ROWS = 1024
SUBLANES = 8

_bf16 = jnp.bfloat16
_f32 = jnp.float32


def _group_of_tile(i):
    ctx_tiles = N_CTX // ROW_TILE
    lat_tiles = DEC_SEQ // ROW_TILE
    return jnp.where(i < ctx_tiles, 0, 1 + (i - ctx_tiles) // lat_tiles)


def _split_bf16(a):
    hi = a.astype(_bf16)
    lo = (a - hi.astype(_f32)).astype(_bf16)
    return hi, lo


_HI16 = 0xFFFF0000


def _pack_bf16_pairs(h):
    half = h.shape[1] // 2
    bits = pltpu.bitcast(h.astype(_bf16).astype(_f32), jnp.uint32)
    return (bits[:, :half] >> 16) | (bits[:, half:] & jnp.uint32(_HI16))


def _unpack_bf16_pairs(words):
    lo = pltpu.bitcast(words << 16, _f32).astype(_bf16)
    hi = pltpu.bitcast(words & jnp.uint32(_HI16), _f32).astype(_bf16)
    return lo, hi


def _router_kernel(x_ref, g_ref, sc_ref, sh_ref, rw_ref, rb_ref, h_ref, idx_ref, w_ref, rank_ref, cnt_ref, cnt_s):
    x = x_ref[...]
    y = x * lax.rsqrt(jnp.mean(x * x, axis=-1, keepdims=True) + EPS) * g_ref[...]
    h = y * (1.0 + sc_ref[...]) + sh_ref[...]
    h_ref[...] = _pack_bf16_pairs(h)
    h_hi, h_lo = _split_bf16(h)
    r_hi, r_lo = _split_bf16(rw_ref[...])
    logits = (jnp.dot(h_hi, r_hi, preferred_element_type=_f32)
              + jnp.dot(h_hi, r_lo, preferred_element_type=_f32)
              + jnp.dot(h_lo, r_hi, preferred_element_type=_f32)) + rb_ref[...]
    lane = lax.broadcasted_iota(jnp.int32, logits.shape, 1)
    work = logits
    vals, idxs = [], []
    for _ in range(TOP_K):
        m = jnp.max(work, axis=-1, keepdims=True)
        i = jnp.min(jnp.where(work == m, lane, N_EXPERTS), axis=-1, keepdims=True)
        vals.append(m)
        idxs.append(i)
        work = jnp.where(lane == i, -jnp.inf, work)
    es = [jnp.exp(v - vals[0]) for v in vals]
    den = es[0] + es[1] + es[2] + es[3]
    @pl.when(pl.program_id(0) == 0)
    def _():
        cnt_s[...] = jnp.zeros(cnt_s.shape, _f32)

    out_lane = lax.broadcasted_iota(jnp.int32, idx_ref.shape, 1)
    tr = lax.broadcasted_iota(jnp.int32, (ROW_TILE, ROW_TILE), 0)
    tc = lax.broadcasted_iota(jnp.int32, (ROW_TILE, ROW_TILE), 1)
    earlier = jnp.where(tc < tr, 1.0, 0.0).astype(_bf16)
    running = cnt_s[...]
    idx_out = jnp.zeros(idx_ref.shape, jnp.int32)
    w_out = jnp.zeros(w_ref.shape, _f32)
    rank_out = jnp.zeros(rank_ref.shape, jnp.int32)
    for k in range(TOP_K):
        onehot = jnp.where(out_lane == idxs[k], 1.0, 0.0)
        before = jnp.dot(earlier, onehot.astype(_bf16), preferred_element_type=_f32)
        rank = jnp.sum(onehot * (running + before), axis=-1, keepdims=True)
        running = running + jnp.sum(onehot, axis=0, keepdims=True)
        idx_out = jnp.where(out_lane == k, idxs[k], idx_out)
        w_out = jnp.where(out_lane == k, es[k] / den, w_out)
        rank_out = jnp.where(out_lane == k, rank.astype(jnp.int32), rank_out)
    cnt_s[...] = running
    idx_ref[...] = idx_out
    w_ref[...] = w_out
    rank_ref[...] = rank_out
    cnt_ref[...] = running


def _router(x, norm_g, sc, sh, router_w, router_b, layer):
    nt = N_TOK // ROW_TILE
    row = lambda i: (i, 0)
    grp = lambda i: (_group_of_tile(i), 0, 0)
    lay = lambda i: (layer, 0, 0)
    h, idx, w, rank, cnt = pl.pallas_call(
        _router_kernel,
        grid=(nt,),
        in_specs=[
            pl.BlockSpec((ROW_TILE, D_MODEL), row),
            pl.BlockSpec((None, 1, D_MODEL), lay),
            pl.BlockSpec((None, 1, D_MODEL), grp),
            pl.BlockSpec((None, 1, D_MODEL), grp),
            pl.BlockSpec((None, D_MODEL, N_EXPERTS), lay),
            pl.BlockSpec((None, 1, N_EXPERTS), lay),
        ],
        out_specs=[
            pl.BlockSpec((ROW_TILE, D_MODEL // 2), row),
            pl.BlockSpec((ROW_TILE, LANES), row),
            pl.BlockSpec((ROW_TILE, LANES), row),
            pl.BlockSpec((ROW_TILE, LANES), row),
            pl.BlockSpec((1, LANES), lambda i: (0, 0)),
        ],
        out_shape=[
            jax.ShapeDtypeStruct((N_TOK, D_MODEL // 2), jnp.uint32),
            jax.ShapeDtypeStruct((N_TOK, LANES), jnp.int32),
            jax.ShapeDtypeStruct((N_TOK, LANES), _f32),
            jax.ShapeDtypeStruct((N_TOK, LANES), jnp.int32),
            jax.ShapeDtypeStruct((1, LANES), _f32),
        ],
        scratch_shapes=[pltpu.VMEM((1, LANES), _f32)],
        compiler_params=pltpu.CompilerParams(dimension_semantics=("arbitrary",), vmem_limit_bytes=VMEM_LIMIT),
        name="router",
    )(x, norm_g.reshape(DEPTH, 1, D_MODEL), sc, sh, router_w, router_b.reshape(DEPTH, 1, N_EXPERTS))
    return h, idx[:, :TOP_K], w, rank[:, :TOP_K], cnt[0, :N_EXPERTS].astype(jnp.int32)


def _moe_kernel(exp_ref, rows_ref, used_ref, x_ref, w1g_ref, w1u_ref, b1g_ref, b1u_ref, w2_ref, b2_ref,
                o_hbm, w1g_s, w1u_s, w2_s, x_s, acc_s, sem):
    s = pl.program_id(0)
    c = pl.program_id(1)
    nvalid = rows_ref[s]
    half = D_MODEL // 2

    def write_out():
        out_copy = pltpu.make_async_copy(
            acc_s, o_hbm.at[pl.ds(pl.multiple_of(s * MOE_SUPER, MOE_SUPER), MOE_SUPER)], sem)
        out_copy.start()
        out_copy.wait()

    @pl.when(nvalid > 0)
    def _():
        @pl.when(c == 0)
        def _():
            acc_s[...] = jnp.zeros(acc_s.shape, _f32)
            lo, hi = _unpack_bf16_pairs(x_ref[...])
            x_s[:, :half] = lo
            x_s[:, half:] = hi

        w1g_s[...] = w1g_ref[...].astype(_bf16)
        w1u_s[...] = w1u_ref[...].astype(_bf16)
        w2_s[...] = w2_ref[...].astype(_bf16)
        for r in range(MOE_SUPER // MOE_SUB):
            @pl.when(r * MOE_SUB < nvalid)
            def _():
                rows = pl.ds(r * MOE_SUB, MOE_SUB)
                xs = x_s[rows, :]
                zg = jnp.dot(xs, w1g_s[...], preferred_element_type=_f32) + b1g_ref[...]
                zu = jnp.dot(xs, w1u_s[...], preferred_element_type=_f32) + b1u_ref[...]
                g = jnp.minimum(zg, SWIGLU_LIMIT)
                u = jnp.clip(zu, -SWIGLU_LIMIT, SWIGLU_LIMIT)
                act = g * jax.nn.sigmoid(SWIGLU_ALPHA * g) * (u + 1.0)
                acc_s[rows, :] += jnp.dot(act.astype(_bf16), w2_s[...], preferred_element_type=_f32)

        @pl.when(c == MOE_NC - 1)
        def _():
            acc_s[...] = acc_s[...] + b2_ref[...]
            write_out()

    @pl.when((nvalid == 0) & (c == 0))
    def _():
        acc_s[...] = jnp.zeros(acc_s.shape, _f32)
        write_out()


def _moe_ffn(x_sorted, st_expert, st_rows, n_used, w1, b1, w2, b2, layer):
    def xmap(s, c, e_ref, r_ref, u_ref):
        return (jnp.minimum(s, u_ref[0] - 1), 0)

    def chunk(s, c, u_ref):
        return jnp.where(s < u_ref[0], c, MOE_NC - 1)

    def w1g_map(s, c, e_ref, r_ref, u_ref):
        return (layer, e_ref[s], 0, chunk(s, c, u_ref))

    def w1u_map(s, c, e_ref, r_ref, u_ref):
        return (layer, e_ref[s], 0, MOE_NC + chunk(s, c, u_ref))

    def w2_map(s, c, e_ref, r_ref, u_ref):
        return (layer, e_ref[s], chunk(s, c, u_ref), 0)

    def b2_map(s, c, e_ref, r_ref, u_ref):
        return (layer, e_ref[s], 0, 0)

    grid_spec = pltpu.PrefetchScalarGridSpec(
        num_scalar_prefetch=3,
        grid=(MOE_NS, MOE_NC),
        in_specs=[
            pl.BlockSpec((MOE_SUPER, D_MODEL // 2), xmap),
            pl.BlockSpec((None, None, D_MODEL, MOE_FC), w1g_map),
            pl.BlockSpec((None, None, D_MODEL, MOE_FC), w1u_map),
            pl.BlockSpec((None, None, 1, MOE_FC), w1g_map),
            pl.BlockSpec((None, None, 1, MOE_FC), w1u_map),
            pl.BlockSpec((None, None, MOE_FC, D_MODEL), w2_map),
            pl.BlockSpec((None, None, 1, D_MODEL), b2_map),
        ],
        out_specs=pl.BlockSpec(memory_space=pl.ANY),
        scratch_shapes=[
            pltpu.VMEM((D_MODEL, MOE_FC), _bf16),
            pltpu.VMEM((D_MODEL, MOE_FC), _bf16),
            pltpu.VMEM((MOE_FC, D_MODEL), _bf16),
            pltpu.VMEM((MOE_SUPER, D_MODEL), _bf16),
            pltpu.VMEM((MOE_SUPER, D_MODEL), _f32),
            pltpu.SemaphoreType.DMA(()),
        ],
    )
    return pl.pallas_call(
        _moe_kernel,
        grid_spec=grid_spec,
        out_shape=jax.ShapeDtypeStruct((MOE_NS * MOE_SUPER, D_MODEL), _f32),
        compiler_params=pltpu.CompilerParams(dimension_semantics=("arbitrary", "arbitrary"),
                                             vmem_limit_bytes=VMEM_LIMIT),
        name="moe_ffn",
    )(st_expert, st_rows, n_used, x_sorted, w1, w1,
      b1.reshape(DEPTH, N_EXPERTS, 1, 2 * D_FF), b1.reshape(DEPTH, N_EXPERTS, 1, 2 * D_FF),
      w2, b2.reshape(DEPTH, N_EXPERTS, 1, D_MODEL))


def _legacy_route_metadata(top_idx, top_w):
    n = N_TOK * TOP_K
    e_flat = top_idx.reshape(n)
    order = jnp.argsort(e_flat, stable=True).astype(jnp.int32)
    e_sorted = e_flat[order]
    grp_end = jnp.searchsorted(e_sorted, jnp.arange(N_EXPERTS, dtype=jnp.int32), side="right").astype(jnp.int32)
    grp_start = jnp.concatenate([jnp.zeros((1,), jnp.int32), grp_end[:-1]])
    counts = grp_end - grp_start
    n_st = (counts + MOE_SUPER - 1) // MOE_SUPER
    st_end = jnp.cumsum(n_st)
    st_start = st_end - n_st
    n_used = st_end[-1]
    s_ids = jnp.arange(MOE_NS, dtype=jnp.int32)
    s_clamped = jnp.minimum(s_ids, n_used - 1)
    st_expert = jnp.searchsorted(st_end, s_clamped, side="right").astype(jnp.int32)
    st_first_rank = (s_clamped - st_start[st_expert]) * MOE_SUPER
    st_rows = jnp.clip(counts[st_expert] - st_first_rank, 0, MOE_SUPER)
    st_rows = jnp.where(s_ids < n_used, st_rows, 0).astype(jnp.int32)
    p_ids = jnp.arange(MOE_NS * MOE_SUPER, dtype=jnp.int32)
    p_tile = p_ids // MOE_SUPER
    p_in_tile = p_ids % MOE_SUPER
    p_valid = p_in_tile < st_rows[p_tile]
    p_entry = jnp.where(p_valid, grp_start[st_expert[p_tile]] + st_first_rank[p_tile] + p_in_tile, 0)
    src_tok = jnp.where(p_valid, order[p_entry] // TOP_K, 0)
    row_w = jnp.where(p_valid, top_w.reshape(n)[order[p_entry]], 0.0)
    inv_order = jnp.argsort(order).astype(jnp.int32)
    rank = inv_order - grp_start[e_flat]
    pos = st_start[e_flat] * MOE_SUPER + rank
    return (src_tok, row_w.reshape(-1, 1), st_expert, st_rows, n_used.reshape(1).astype(jnp.int32),
            pos.reshape(N_TOK, TOP_K))


def _issue_row_copies(idx_ref, src_hbm, dst_ref, sem, n):
    def start(r, carry):
        pltpu.make_async_copy(src_hbm.at[idx_ref[0, r]], dst_ref.at[r], sem).start()
        return carry

    lax.fori_loop(0, n, start, 0, unroll=8)
    pltpu.make_async_copy(src_hbm.at[pl.ds(0, n)], dst_ref, sem).wait()


def _legacy_dispatch_kernel(used_ref, idx_ref, src_hbm, o_ref, sem):
    i = pl.program_id(0)

    @pl.when(i < used_ref[0])
    def _():
        _issue_row_copies(idx_ref, src_hbm, o_ref, sem, DISPATCH_TILE)

    @pl.when(i >= used_ref[0])
    def _():
        o_ref[...] = jnp.zeros(o_ref.shape, o_ref.dtype)


def _legacy_dispatch(h_words, src_tok, n_used_tiles):
    n_rows = src_tok.shape[0]
    nt = n_rows // DISPATCH_TILE
    width = h_words.shape[1]
    grid_spec = pltpu.PrefetchScalarGridSpec(
        num_scalar_prefetch=1,
        grid=(nt,),
        in_specs=[
            pl.BlockSpec((None, 1, DISPATCH_TILE), lambda i, u: (i, 0, 0), memory_space=pltpu.SMEM),
            pl.BlockSpec(memory_space=pl.ANY),
        ],
        out_specs=pl.BlockSpec((DISPATCH_TILE, width), lambda i, u: (i, 0)),
        scratch_shapes=[pltpu.SemaphoreType.DMA(())],
    )
    return pl.pallas_call(
        _dispatch_kernel,
        grid_spec=grid_spec,
        out_shape=jax.ShapeDtypeStruct((n_rows, width), h_words.dtype),
        compiler_params=pltpu.CompilerParams(dimension_semantics=("arbitrary",), vmem_limit_bytes=VMEM_LIMIT),
        name="moe_dispatch",
    )(n_used_tiles, src_tok.reshape(nt, 1, DISPATCH_TILE), h_words)


def _route_tables(top_idx, rank, counts):
    n_st = (counts + MOE_SUPER - 1) // MOE_SUPER
    st_end = jnp.cumsum(n_st)
    st_start = st_end - n_st
    n_used = st_end[-1]
    s_ids = jnp.arange(MOE_NS, dtype=jnp.int32)
    s_clamped = jnp.minimum(s_ids, n_used - 1)
    st_expert = jnp.sum((st_end[None, :] <= s_clamped[:, None]).astype(jnp.int32), axis=1)
    is_exp = st_expert[:, None] == jnp.arange(N_EXPERTS, dtype=jnp.int32)[None, :]
    tile_in_expert = s_clamped - jnp.sum(jnp.where(is_exp, st_start[None, :], 0), axis=1)
    st_rows = jnp.clip(jnp.sum(jnp.where(is_exp, counts[None, :], 0), axis=1) - tile_in_expert * MOE_SUPER,
                       0, MOE_SUPER)
    st_rows = jnp.where(s_ids < n_used, st_rows, 0).astype(jnp.int32)
    first_row = jnp.sum(jnp.where(top_idx[:, :, None] == jnp.arange(N_EXPERTS, dtype=jnp.int32),
                                  st_start * MOE_SUPER, 0), axis=-1)
    return st_expert, st_rows, n_used.reshape(1).astype(jnp.int32), first_row + rank


def _dispatch_kernel(pos_ref, h_ref, zero_hbm, o_hbm, sem):
    n = TOP_K * DISPATCH_TOKENS

    def start(j, carry):
        pltpu.make_async_copy(h_ref.at[j // TOP_K], o_hbm.at[pos_ref[0, j]], sem).start()
        return carry

    lax.fori_loop(0, n, start, 0, unroll=8)
    pltpu.make_async_copy(o_hbm.at[pl.ds(0, n)], o_hbm.at[pl.ds(0, n)], sem).wait()


def _dispatch(h_words, pos):
    nt = N_TOK // DISPATCH_TOKENS
    width = h_words.shape[1]
    n_rows = MOE_NS * MOE_SUPER
    return pl.pallas_call(
        _dispatch_kernel,
        grid=(nt,),
        in_specs=[
            pl.BlockSpec((None, 1, TOP_K * DISPATCH_TOKENS), lambda i: (i, 0, 0), memory_space=pltpu.SMEM),
            pl.BlockSpec((DISPATCH_TOKENS, width), lambda i: (i, 0)),
            pl.BlockSpec(memory_space=pl.ANY),
        ],
        out_specs=pl.BlockSpec(memory_space=pl.ANY),
        out_shape=jax.ShapeDtypeStruct((n_rows, width), h_words.dtype),
        scratch_shapes=[pltpu.SemaphoreType.DMA(())],
        input_output_aliases={2: 0},
        compiler_params=pltpu.CompilerParams(dimension_semantics=("arbitrary",), vmem_limit_bytes=VMEM_LIMIT),
        name="moe_dispatch",
    )(pos.reshape(nt, 1, TOP_K * DISPATCH_TOKENS), h_words, jnp.zeros((n_rows, width), h_words.dtype))


def _combine_kernel(pos_ref, y_hbm, x_ref, gt_ref, w_ref, o_ref, buf, sem):
    _issue_row_copies(pos_ref, y_hbm, buf, sem, TOP_K * COMBINE_TILE)
    y = w_ref[:, 0:1] * buf[0:COMBINE_TILE, :]
    for k in range(1, TOP_K):
        y = y + w_ref[:, k:k + 1] * buf[k * COMBINE_TILE:(k + 1) * COMBINE_TILE, :]
    o_ref[...] = x_ref[...] + gt_ref[...] * y


def _combine(y_sorted, pos, top_w, x, gt):
    nt = N_TOK // COMBINE_TILE
    tiles_per_row_tile = ROW_TILE // COMBINE_TILE
    pos_tiles = pos.reshape(nt, COMBINE_TILE, TOP_K).transpose(0, 2, 1).reshape(nt, 1, TOP_K * COMBINE_TILE)
    return pl.pallas_call(
        _combine_kernel,
        grid=(nt,),
        in_specs=[
            pl.BlockSpec((None, 1, TOP_K * COMBINE_TILE), lambda i: (i, 0, 0), memory_space=pltpu.SMEM),
            pl.BlockSpec(memory_space=pl.ANY),
            pl.BlockSpec((COMBINE_TILE, D_MODEL), lambda i: (i, 0)),
            pl.BlockSpec((None, 1, D_MODEL), lambda i: (_group_of_tile(i // tiles_per_row_tile), 0, 0)),
            pl.BlockSpec((COMBINE_TILE, LANES), lambda i: (i, 0)),
        ],
        out_specs=pl.BlockSpec((COMBINE_TILE, D_MODEL), lambda i: (i, 0)),
        out_shape=jax.ShapeDtypeStruct((N_TOK, D_MODEL), _f32),
        scratch_shapes=[pltpu.VMEM((TOP_K * COMBINE_TILE, D_MODEL), _f32), pltpu.SemaphoreType.DMA(())],
        compiler_params=pltpu.CompilerParams(dimension_semantics=("arbitrary",), vmem_limit_bytes=VMEM_LIMIT),
        name="moe_combine",
    )(pos_tiles, y_sorted, x, gt, top_w)


def _moe_sublayer(x, mods, p, layer):
    sh2, sc2, gt2 = mods[3], mods[4], mods[5]
    h, top_idx, top_w, rank, counts = _router(x, p["norm2_g"], sc2, sh2, p["router_w"], p["router_b"], layer)
    st_expert, st_rows, n_used, pos = _route_tables(top_idx, rank, counts)
    x_sorted = _dispatch(h, pos)
    y_sorted = _moe_ffn(x_sorted, st_expert, st_rows, n_used, p["exp_w1"], p["exp_b1"], p["exp_w2"], p["exp_b2"],
                        layer)
    return _combine(y_sorted, pos, top_w, x, gt2)


def _group_of_rows(i, tile):
    return jnp.where(i < N_CTX // tile, 0, 1 + (i - N_CTX // tile) // (DEC_SEQ // tile))


def _mod_kernel(c_ref, w_ref, b_ref, o_ref):
    c = c_ref[...]
    a = (c * jax.nn.sigmoid(c)).astype(_bf16)
    o_ref[...] = jnp.dot(a, w_ref[...].astype(_bf16), preferred_element_type=_f32) + b_ref[...]


def _modulation(cond_rows, w_mod, b_mod, layer):
    n = N_MOD * D_MODEL
    return pl.pallas_call(
        _mod_kernel,
        grid=(n // MOD_TN,),
        in_specs=[
            pl.BlockSpec((SUBLANES, D_MODEL), lambda j: (0, 0)),
            pl.BlockSpec((None, D_MODEL, MOD_TN), lambda j: (layer, 0, j)),
            pl.BlockSpec((None, 1, MOD_TN), lambda j: (layer, 0, j)),
        ],
        out_specs=pl.BlockSpec((SUBLANES, MOD_TN), lambda j: (0, j)),
        out_shape=jax.ShapeDtypeStruct((SUBLANES, n), _f32),
        compiler_params=pltpu.CompilerParams(dimension_semantics=("arbitrary",), vmem_limit_bytes=VMEM_LIMIT),
        name="modulation",
    )(cond_rows, w_mod, b_mod.reshape(DEPTH, 1, n))


def _norm_mod_kernel(x_ref, g_ref, sc_ref, sh_ref, h_ref):
    x = x_ref[...]
    y = x * lax.rsqrt(jnp.mean(x * x, axis=-1, keepdims=True) + EPS) * g_ref[...]
    h_ref[...] = (y * (1.0 + sc_ref[...]) + sh_ref[...]).astype(_bf16)


def _norm_mod(x, norm_g, sc, sh, layer):
    row = lambda i: (i, 0)
    grp = lambda i: (_group_of_tile(i), 0, 0)
    lay = lambda i: (layer, 0, 0)
    return pl.pallas_call(
        _norm_mod_kernel,
        grid=(N_TOK // ROW_TILE,),
        in_specs=[
            pl.BlockSpec((ROW_TILE, D_MODEL), row),
            pl.BlockSpec((None, 1, D_MODEL), lay),
            pl.BlockSpec((None, 1, D_MODEL), grp),
            pl.BlockSpec((None, 1, D_MODEL), grp),
        ],
        out_specs=pl.BlockSpec((ROW_TILE, D_MODEL), row),
        out_shape=jax.ShapeDtypeStruct((N_TOK, D_MODEL), _bf16),
        compiler_params=pltpu.CompilerParams(dimension_semantics=("arbitrary",), vmem_limit_bytes=VMEM_LIMIT),
        name="norm_mod",
    )(x, norm_g.reshape(DEPTH, 1, D_MODEL), sc, sh)


def _in_proj_kernel(h_ref, w_ref, o_ref, w_s):
    @pl.when(pl.program_id(1) == 0)
    def _():
        w_s[...] = w_ref[...].astype(_bf16)

    o_ref[...] = jnp.dot(h_ref[...], w_s[...], preferred_element_type=_f32)


def _in_proj(h, w_in, layer):
    return pl.pallas_call(
        _in_proj_kernel,
        grid=(IN_COLS // PROJ_TN, N_TOK // PROJ_TM),
        in_specs=[
            pl.BlockSpec((PROJ_TM, D_MODEL), lambda j, i: (i, 0)),
            pl.BlockSpec((None, D_MODEL, PROJ_TN), lambda j, i: (layer, 0, j)),
        ],
        out_specs=pl.BlockSpec((PROJ_TM, PROJ_TN), lambda j, i: (i, j)),
        out_shape=jax.ShapeDtypeStruct((N_TOK, IN_COLS), _f32),
        scratch_shapes=[pltpu.VMEM((D_MODEL, PROJ_TN), _bf16)],
        compiler_params=pltpu.CompilerParams(dimension_semantics=("arbitrary", "arbitrary"),
                                             vmem_limit_bytes=VMEM_LIMIT),
        name="in_proj",
    )(h, w_in)


def _merge_kernel(a_ref, h_ref, s_ref, ga_ref, gh_ref, gs_ref, wa_ref, wh_ref, ws_ref, o_ref, wa_s, wh_s, ws_s):
    @pl.when(pl.program_id(1) == 0)
    def _():
        wa_s[...] = wa_ref[...].astype(_bf16)
        wh_s[...] = wh_ref[...].astype(_bf16)
        ws_s[...] = ws_ref[...].astype(_bf16)

    y = (jax.nn.sigmoid(ga_ref[...]) * jnp.dot(a_ref[...], wa_s[...], preferred_element_type=_f32)
         + jax.nn.sigmoid(gh_ref[...]) * jnp.dot(h_ref[...], wh_s[...], preferred_element_type=_f32)
         + jax.nn.sigmoid(gs_ref[...]) * jnp.dot(s_ref[...], ws_s[...], preferred_element_type=_f32))
    o_ref[...] = y.astype(_bf16)


def _merge(a_out, h_out, s_out, z, w_a, w_h, w_s, layer):
    ga0, gh0, gs0 = (sum(IN_SIZES[:k]) // MERGE_TN for k in (9, 10, 11))
    act = lambda j, i: (i, 0)
    wmap = lambda j, i: (layer, 0, j)
    return pl.pallas_call(
        _merge_kernel,
        grid=(D_MODEL // MERGE_TN, N_TOK // ROW_TILE),
        in_specs=[
            pl.BlockSpec((ROW_TILE, ATT_Q_W), act),
            pl.BlockSpec((ROW_TILE, HG_W), act),
            pl.BlockSpec((ROW_TILE, S5_W), act),
            pl.BlockSpec((ROW_TILE, MERGE_TN), lambda j, i: (i, ga0 + j)),
            pl.BlockSpec((ROW_TILE, MERGE_TN), lambda j, i: (i, gh0 + j)),
            pl.BlockSpec((ROW_TILE, MERGE_TN), lambda j, i: (i, gs0 + j)),
            pl.BlockSpec((None, ATT_Q_W, MERGE_TN), wmap),
            pl.BlockSpec((None, HG_W, MERGE_TN), wmap),
            pl.BlockSpec((None, S5_W, MERGE_TN), wmap),
        ],
        out_specs=pl.BlockSpec((ROW_TILE, MERGE_TN), lambda j, i: (i, j)),
        out_shape=jax.ShapeDtypeStruct((N_TOK, D_MODEL), _bf16),
        scratch_shapes=[pltpu.VMEM((ATT_Q_W, MERGE_TN), _bf16), pltpu.VMEM((HG_W, MERGE_TN), _bf16),
                        pltpu.VMEM((S5_W, MERGE_TN), _bf16)],
        compiler_params=pltpu.CompilerParams(dimension_semantics=("arbitrary", "arbitrary"),
                                             vmem_limit_bytes=VMEM_LIMIT),
        name="merge",
    )(a_out, h_out, s_out, z, z, z, w_a, w_h, w_s)


def _out_proj_kernel(y_ref, w_ref, x_ref, gt_ref, o_ref, w_s):
    @pl.when(pl.program_id(1) == 0)
    def _():
        w_s[...] = w_ref[...].astype(_bf16)

    o_ref[...] = x_ref[...] + gt_ref[...] * jnp.dot(y_ref[...], w_s[...], preferred_element_type=_f32)


def _out_proj(y, w_out, x, gt, layer):
    return pl.pallas_call(
        _out_proj_kernel,
        grid=(D_MODEL // MERGE_TN, N_TOK // ROW_TILE),
        in_specs=[
            pl.BlockSpec((ROW_TILE, D_MODEL), lambda j, i: (i, 0)),
            pl.BlockSpec((None, D_MODEL, MERGE_TN), lambda j, i: (layer, 0, j)),
            pl.BlockSpec((ROW_TILE, MERGE_TN), lambda j, i: (i, j)),
            pl.BlockSpec((None, 1, MERGE_TN), lambda j, i: (_group_of_tile(i), 0, j)),
        ],
        out_specs=pl.BlockSpec((ROW_TILE, MERGE_TN), lambda j, i: (i, j)),
        out_shape=jax.ShapeDtypeStruct((N_TOK, D_MODEL), _f32),
        scratch_shapes=[pltpu.VMEM((D_MODEL, MERGE_TN), _bf16)],
        compiler_params=pltpu.CompilerParams(dimension_semantics=("arbitrary", "arbitrary"),
                                             vmem_limit_bytes=VMEM_LIMIT),
        name="out_proj",
    )(y, w_out, x, gt)


COL_Q, COL_K, COL_V, COL_HQ, COL_HFF, COL_HFB, COL_HI, COL_HG, COL_SU = (sum(IN_SIZES[:k]) for k in range(9))
KV_GROUP_W = Q_GROUP * HEAD_DIM


def _head_norm(x, g):
    return x * lax.rsqrt(jnp.mean(x * x, axis=-1, keepdims=True) + EPS) * g


def _dot_nt(a, b):
    return lax.dot_general(a, b, (((1,), (1,)), ((), ())), preferred_element_type=_f32)


def _softmax_with_sink(s, sink):
    m = jnp.maximum(jnp.max(s, axis=-1, keepdims=True), sink)
    p = jnp.exp(s - m)
    den = jnp.sum(p, axis=-1, keepdims=True) + jnp.exp(sink - m)
    return p / den


def _ctx_attn_kernel(q_ref, k_ref, v_ref, qg_ref, kg_ref, sink_ref, o_ref, ko_ref, vo_ref):
    kn = _head_norm(k_ref[...], kg_ref[...])
    v = v_ref[...]
    ko_ref[...] = kn
    vo_ref[...] = v
    knb = kn.astype(_bf16)
    vb = v.astype(_bf16)
    for g in range(Q_GROUP):
        cols = slice(g * HEAD_DIM, (g + 1) * HEAD_DIM)
        qn = _head_norm(q_ref[:, cols], qg_ref[...])
        s = _dot_nt(qn.astype(_bf16), knb) * HEAD_DIM ** -0.5
        p = _softmax_with_sink(s, sink_ref[0:1, g:g + 1])
        o_ref[:, cols] = jnp.dot(p.astype(_bf16), vb, preferred_element_type=_f32).astype(_bf16)


def _ctx_attention(z, q_g, k_g, sink, layer):
    lay = lambda b, j: (layer, 0, 0)
    return pl.pallas_call(
        _ctx_attn_kernel,
        grid=(BATCH, N_KV_HEADS),
        in_specs=[
            pl.BlockSpec((SEQ, KV_GROUP_W), lambda b, j: (b, COL_Q // KV_GROUP_W + j)),
            pl.BlockSpec((SEQ, HEAD_DIM), lambda b, j: (b, COL_K // HEAD_DIM + j)),
            pl.BlockSpec((SEQ, HEAD_DIM), lambda b, j: (b, COL_V // HEAD_DIM + j)),
            pl.BlockSpec((None, 1, HEAD_DIM), lay),
            pl.BlockSpec((None, 1, HEAD_DIM), lay),
            pl.BlockSpec((None, None, 1, Q_GROUP), lambda b, j: (layer, j, 0, 0)),
        ],
        out_specs=[
            pl.BlockSpec((SEQ, KV_GROUP_W), lambda b, j: (b, j)),
            pl.BlockSpec((SEQ, HEAD_DIM), lambda b, j: (b, j)),
            pl.BlockSpec((SEQ, HEAD_DIM), lambda b, j: (b, j)),
        ],
        out_shape=[
            jax.ShapeDtypeStruct((N_CTX, ATT_Q_W), _bf16),
            jax.ShapeDtypeStruct((N_CTX, ATT_KV_W), _f32),
            jax.ShapeDtypeStruct((N_CTX, ATT_KV_W), _f32),
        ],
        compiler_params=pltpu.CompilerParams(dimension_semantics=("arbitrary", "arbitrary"),
                                             vmem_limit_bytes=VMEM_LIMIT),
        name="ctx_attention",
    )(z, z, z, q_g.reshape(DEPTH, 1, HEAD_DIM), k_g.reshape(DEPTH, 1, HEAD_DIM),
      sink.reshape(DEPTH, N_KV_HEADS, 1, Q_GROUP))


def _rope_tables():
    half = HEAD_DIM // 2
    nf = half // 2
    t = np.arange(DEC_SEQ)
    pos = np.stack([t // GRID_W, t % GRID_W], axis=1).astype(np.float64)
    inv = ROPE_BASE ** (-np.arange(nf, dtype=np.float64) / nf)
    dim = np.arange(HEAD_DIM)
    ang = pos[:, dim // half] * inv[dim % nf][None, :]
    sign = np.where((dim % half) < nf, -1.0, 1.0)
    return jnp.asarray(np.cos(ang), _f32), jnp.asarray(np.sin(ang) * sign, _f32)


def _rope(x, cos, sin_signed):
    nf = HEAD_DIM // 4
    lane = lax.broadcasted_iota(jnp.int32, x.shape, 1)
    partner = jnp.where((lane % (2 * nf)) < nf, pltpu.roll(x, HEAD_DIM - nf, 1), pltpu.roll(x, nf, 1))
    return x * cos + partner * sin_signed


def _win_attn_kernel(q_ref, kp_ref, kc_ref, kn_ref, vp_ref, vc_ref, vn_ref, cq_ref, sq_ref, cp_ref, sp_ref,
                     cn_ref, sn_ref, ck_ref, cv_ref, qg_ref, kg_ref, sink_ref, o_ref):
    n = pl.program_id(1)
    nb = pl.num_programs(1)
    kg = kg_ref[...]
    k_all = jnp.concatenate([
        _rope(_head_norm(kp_ref[...], kg), cp_ref[...], sp_ref[...]),
        _rope(_head_norm(kc_ref[...], kg), cq_ref[...], sq_ref[...]),
        _rope(_head_norm(kn_ref[...], kg), cn_ref[...], sn_ref[...]),
        ck_ref[...]], axis=0).astype(_bf16)
    v_all = jnp.concatenate([vp_ref[...], vc_ref[...], vn_ref[...], cv_ref[...]], axis=0).astype(_bf16)
    r = lax.broadcasted_iota(jnp.int32, (ATTN_BLOCK, ATTN_BLOCK), 0)
    c = lax.broadcasted_iota(jnp.int32, (ATTN_BLOCK, ATTN_BLOCK), 1)
    neg = jnp.full((ATTN_BLOCK, ATTN_BLOCK), -jnp.inf, _f32)
    zero = jnp.zeros((ATTN_BLOCK, ATTN_BLOCK), _f32)
    bias = jnp.concatenate([
        jnp.where(n > 0, jnp.where(c >= r, zero, neg), neg),
        zero,
        jnp.where(n < nb - 1, jnp.where(c <= r, zero, neg), neg),
        jnp.zeros((ATTN_BLOCK, ck_ref.shape[0]), _f32)], axis=1)
    for g in range(Q_GROUP):
        cols = slice(g * HEAD_DIM, (g + 1) * HEAD_DIM)
        qn = _rope(_head_norm(q_ref[:, cols], qg_ref[...]), cq_ref[...], sq_ref[...])
        s = _dot_nt(qn.astype(_bf16), k_all) * HEAD_DIM ** -0.5 + bias
        p = _softmax_with_sink(s, sink_ref[0:1, g:g + 1])
        o_ref[:, cols] = jnp.dot(p.astype(_bf16), v_all, preferred_element_type=_f32).astype(_bf16)


def _win_attention(z, cache_k, cache_v, q_g, k_g, sink, layer):
    cos, sin = _rope_tables()
    nb = DEC_SEQ // ATTN_BLOCK
    base = N_CTX // ATTN_BLOCK
    prev = lambda n: jnp.maximum(n - 1, 0)
    nxt = lambda n: jnp.minimum(n + 1, nb - 1)
    past = cache_k.shape[3]

    def zrow(sel, col0):
        return pl.BlockSpec((ATTN_BLOCK, HEAD_DIM), lambda b, n, j: (base + b * nb + sel(n), col0 // HEAD_DIM + j))

    def tab(sel):
        return pl.BlockSpec((ATTN_BLOCK, HEAD_DIM), lambda b, n, j: (sel(n), 0))

    same = lambda n: n
    lay = lambda b, n, j: (layer, 0, 0)
    cache = pl.BlockSpec((None, None, None, past, HEAD_DIM), lambda b, n, j: (b, layer, j, 0, 0))
    return pl.pallas_call(
        _win_attn_kernel,
        grid=(DEC_BATCH, nb, N_KV_HEADS),
        in_specs=[
            pl.BlockSpec((ATTN_BLOCK, KV_GROUP_W), lambda b, n, j: (base + b * nb + n, COL_Q // KV_GROUP_W + j)),
            zrow(prev, COL_K), zrow(same, COL_K), zrow(nxt, COL_K),
            zrow(prev, COL_V), zrow(same, COL_V), zrow(nxt, COL_V),
            tab(same), tab(same), tab(prev), tab(prev), tab(nxt), tab(nxt),
            cache, cache,
            pl.BlockSpec((None, 1, HEAD_DIM), lay),
            pl.BlockSpec((None, 1, HEAD_DIM), lay),
            pl.BlockSpec((None, None, 1, Q_GROUP), lambda b, n, j: (layer, j, 0, 0)),
        ],
        out_specs=pl.BlockSpec((ATTN_BLOCK, KV_GROUP_W), lambda b, n, j: (b * nb + n, j)),
        out_shape=jax.ShapeDtypeStruct((N_LAT, ATT_Q_W), _bf16),
        compiler_params=pltpu.CompilerParams(dimension_semantics=("arbitrary", "arbitrary", "arbitrary"),
                                             vmem_limit_bytes=VMEM_LIMIT),
        name="win_attention",
    )(z, z, z, z, z, z, z, cos, sin, cos, sin, cos, sin, cache_k, cache_v,
      q_g.reshape(DEPTH, 1, HEAD_DIM), k_g.reshape(DEPTH, 1, HEAD_DIM), sink.reshape(DEPTH, N_KV_HEADS, 1, Q_GROUP))


HG_BLOCK = 128
HG_LEVELS = 7
HG_NSEG = HG_LEVELS + 2
HG_HEADS_PER_STEP = 4


def _hgrn_constants():
    c = HG_BLOCK
    i = np.arange(c)
    seg = np.zeros((HG_NSEG, c, c), np.float32)
    mask = np.zeros((HG_LEVELS + 1, c, c), np.float32)
    mask[0] = np.eye(c)
    for lv in range(HG_LEVELS):
        b = 2 << lv
        h = b // 2
        mid = (i // b) * b + h
        second = (i % b) >= h
        for r in range(c):
            if second[r]:
                seg[lv, r, mid[r]:r + 1] = 1.0
            else:
                seg[lv, r, r + 1:mid[r]] = 1.0
        same = (i[:, None] // b) == (i[None, :] // b)
        mask[lv + 1] = same & second[:, None] & ~second[None, :]
    seg[HG_LEVELS] = np.tril(np.ones((c, c)))
    seg[HG_LEVELS + 1] = np.triu(np.ones((c, c)), 1)
    seg2 = np.stack([seg, seg[:, ::-1, ::-1]]).reshape(2, HG_NSEG * c, c)
    mask2 = np.stack([mask, mask[:, ::-1, ::-1]])
    return jnp.asarray(seg2, _bf16), jnp.asarray(mask2, _f32)


def _hgrn_schedule():
    rows, first, last, seq = [[], []], [], [], []
    for s in range(BATCH + DEC_BATCH):
        n = (SEQ if s < BATCH else DEC_SEQ) // HG_BLOCK
        base = s * (SEQ // HG_BLOCK) if s < BATCH else N_CTX // HG_BLOCK + (s - BATCH) * (DEC_SEQ // HG_BLOCK)
        for cpos in range(n):
            rows[0].append(base + cpos)
            rows[1].append(base + n - 1 - cpos)
            first.append(int(cpos == 0))
            last.append(int(cpos == n - 1))
            seq.append(s)
    as_i32 = lambda a: jnp.asarray(np.asarray(a, np.int32).reshape(-1))
    return as_i32(rows), as_i32(first), as_i32(last), as_i32(seq), len(seq)


def _hgrn_kernel(rows_ref, first_ref, last_ref, seq_ref, q_ref, f_ref, i_ref, lb_ref, seg_ref, mask_ref, s0_ref,
                 o_ref, sf_ref, st_s):
    job = pl.program_id(2)
    seg = seg_ref[...]

    @pl.when(first_ref[job] == 1)
    def _():
        for hh in range(HG_HEADS_PER_STEP):
            st_s[hh] = s0_ref[hh].T

    for hh in range(HG_HEADS_PER_STEP):
        cols = slice(hh * HG_DK, (hh + 1) * HG_DK)
        xq = q_ref[:, cols]
        q = xq * jax.nn.sigmoid(xq) * HG_DK ** -0.5
        xf = f_ref[:, cols]
        log_sig = jnp.minimum(xf, 0.0) - jnp.log(1.0 + jnp.exp(-jnp.abs(xf)))
        lb = lb_ref[hh]
        f = lb + (1.0 - lb) * jnp.exp(log_sig)
        log_f = jnp.where(lb > 0.0, jnp.log(f), log_sig)
        k = 1.0 - f
        v = i_ref[:, cols]
        vb = v.astype(_bf16)
        lf_hi, lf_lo = _split_bf16(log_f)
        decay = jnp.exp(jnp.dot(seg, lf_hi, preferred_element_type=_f32)
                        + jnp.dot(seg, lf_lo, preferred_element_type=_f32))
        scores = _dot_nt(q.astype(_bf16), k.astype(_bf16)) * mask_ref[0]
        for lv in range(HG_LEVELS):
            e = decay[lv * HG_BLOCK:(lv + 1) * HG_BLOCK]
            scores = scores + _dot_nt((q * e).astype(_bf16), (k * e).astype(_bf16)) * mask_ref[lv + 1]
        e_q = decay[HG_LEVELS * HG_BLOCK:(HG_LEVELS + 1) * HG_BLOCK]
        e_k = decay[(HG_LEVELS + 1) * HG_BLOCK:(HG_LEVELS + 2) * HG_BLOCK]
        st = st_s[hh]
        o_ref[:, cols] = (jnp.dot(scores.astype(_bf16), vb, preferred_element_type=_f32)
                          + _dot_nt((q * e_q).astype(_bf16), st.astype(_bf16)))
        total = jnp.sum(log_f, axis=0, keepdims=True)
        st_new = jnp.exp(total) * st + jnp.dot(v.T.astype(_bf16), (k * e_k).astype(_bf16),
                                               preferred_element_type=_f32)
        st_s[hh] = st_new

    @pl.when(last_ref[job] == 1)
    def _():
        for hh in range(HG_HEADS_PER_STEP):
            sf_ref[hh] = st_s[hh].T


def _hgrn(z, lb, s0):
    seg, mask = _hgrn_constants()
    rows, first, last, seq, n_jobs = _hgrn_schedule()

    hps = HG_HEADS_PER_STEP
    step_w = hps * HG_DK

    def zcol(col0):
        return pl.BlockSpec((HG_BLOCK, step_w),
                            lambda h, d, j, r, f, l, s: (r[d * n_jobs + j], col0 // step_w + h))

    state = pl.BlockSpec((None, None, hps, HG_DK, HG_DV), lambda h, d, j, r, f, l, s: (s[j], d, h, 0, 0))
    grid_spec = pltpu.PrefetchScalarGridSpec(
        num_scalar_prefetch=4,
        grid=(HG_HEADS // hps, 2, n_jobs),
        in_specs=[
            zcol(COL_HQ),
            pl.BlockSpec((HG_BLOCK, step_w),
                         lambda h, d, j, r, f, l, s: (r[d * n_jobs + j], (COL_HFF + d * HG_W) // step_w + h)),
            zcol(COL_HI),
            pl.BlockSpec((None, hps, 1, HG_DK), lambda h, d, j, r, f, l, s: (d, h, 0, 0)),
            pl.BlockSpec((None, HG_NSEG * HG_BLOCK, HG_BLOCK), lambda h, d, j, r, f, l, s: (d, 0, 0)),
            pl.BlockSpec((None, HG_LEVELS + 1, HG_BLOCK, HG_BLOCK), lambda h, d, j, r, f, l, s: (d, 0, 0, 0)),
            state,
        ],
        out_specs=[
            pl.BlockSpec((None, HG_BLOCK, step_w), lambda h, d, j, r, f, l, s: (d, r[d * n_jobs + j], h)),
            state,
        ],
        scratch_shapes=[pltpu.VMEM((hps, HG_DV, HG_DK), _f32)],
    )
    return pl.pallas_call(
        _hgrn_kernel,
        grid_spec=grid_spec,
        out_shape=[
            jax.ShapeDtypeStruct((2, N_TOK, HG_W), _f32),
            jax.ShapeDtypeStruct(s0.shape, _f32),
        ],
        compiler_params=pltpu.CompilerParams(dimension_semantics=("arbitrary", "arbitrary", "arbitrary"),
                                             vmem_limit_bytes=VMEM_LIMIT),
        name="hgrn",
    )(rows, first, last, seq, z, z, z, lb.reshape(2, HG_HEADS, 1, HG_DK), seg, mask, s0)


def _hgrn_out_kernel(of_ref, ob_ref, g_ref, ng_ref, o_ref):
    ng = ng_ref[...]
    for hh in range(HG_OUT_W // HG_DV):
        cols = slice(hh * HG_DV, (hh + 1) * HG_DV)
        o = _head_norm(of_ref[:, cols] + ob_ref[:, cols], ng)
        xg = g_ref[:, cols]
        o_ref[:, cols] = (o * (xg * jax.nn.sigmoid(xg))).astype(_bf16)


def _hgrn_out(o_dirs, z, norm_g, layer):
    return pl.pallas_call(
        _hgrn_out_kernel,
        grid=(N_TOK // ROW_TILE, HG_W // HG_OUT_W),
        in_specs=[
            pl.BlockSpec((None, ROW_TILE, HG_OUT_W), lambda i, j: (0, i, j)),
            pl.BlockSpec((None, ROW_TILE, HG_OUT_W), lambda i, j: (1, i, j)),
            pl.BlockSpec((ROW_TILE, HG_OUT_W), lambda i, j: (i, COL_HG // HG_OUT_W + j)),
            pl.BlockSpec((None, 1, HG_DV), lambda i, j: (layer, 0, 0)),
        ],
        out_specs=pl.BlockSpec((ROW_TILE, HG_OUT_W), lambda i, j: (i, j)),
        out_shape=jax.ShapeDtypeStruct((N_TOK, HG_W), _bf16),
        compiler_params=pltpu.CompilerParams(dimension_semantics=("arbitrary", "arbitrary"),
                                             vmem_limit_bytes=VMEM_LIMIT),
        name="hgrn_out",
    )(o_dirs, o_dirs, z, norm_g.reshape(DEPTH, 1, HG_DV))


def _rms_norm(x, g):
    xf = x.astype(_f32)
    y = xf * lax.rsqrt(jnp.mean(xf * xf, axis=-1, keepdims=True) + EPS)
    return (y * g.astype(_f32)).astype(x.dtype)


def _split_in(z):
    return jnp.split(z, np.cumsum(IN_SIZES)[:-1].tolist(), axis=-1)


def _attn_heads(q, k, v, q_g, k_g):
    b, t = q.shape[:2]
    q = _rms_norm(q.reshape(b, t, N_HEADS, HEAD_DIM), q_g)
    k = _rms_norm(k.reshape(b, t, N_KV_HEADS, HEAD_DIM), k_g)
    return q, k, v.reshape(b, t, N_KV_HEADS, HEAD_DIM)


def _rotate(x, pos):
    nf = x.shape[-1] // 2
    inv = ROPE_BASE ** (-jnp.arange(nf, dtype=_f32) / nf)
    ang = pos.astype(_f32)[:, None] * inv[None, :]
    cos = jnp.cos(ang)[None, :, None, :]
    sin = jnp.sin(ang)[None, :, None, :]
    x1, x2 = x[..., :nf], x[..., nf:]
    return jnp.concatenate([x1 * cos - x2 * sin, x1 * sin + x2 * cos], axis=-1)


def _axial_rope(x):
    t = x.shape[1]
    rows = t // GRID_W
    row = jnp.repeat(jnp.arange(rows), GRID_W)
    col = jnp.tile(jnp.arange(GRID_W), rows)
    half = HEAD_DIM // 2
    return jnp.concatenate([_rotate(x[..., :half], row), _rotate(x[..., half:], col)], axis=-1)


def _sink_column(sink, lead_shape):
    s = sink.astype(_f32).reshape(N_KV_HEADS, Q_GROUP, 1, 1)
    return jnp.broadcast_to(s, lead_shape + (1,))


def _context_attention(q, k, v, sink):
    b, l = q.shape[:2]
    scale = HEAD_DIM ** -0.5
    qb = q.reshape(b, l, N_KV_HEADS, Q_GROUP, HEAD_DIM)
    s = jnp.einsum('bqhgd,bshd->bhgqs', qb, k).astype(_f32) * scale
    s = jnp.concatenate([s, _sink_column(sink, s.shape[:-1])], axis=-1)
    pr = jax.nn.softmax(s, axis=-1)[..., :l]
    o = jnp.einsum('bhgqs,bshd->bqhgd', pr, v)
    return o.reshape(b, l, N_HEADS * HEAD_DIM)


def _window_attention(q, k, v, k_ctx, v_ctx, sink):
    b, t = q.shape[:2]
    nb = t // ATTN_BLOCK
    lc = k_ctx.shape[1]
    scale = HEAD_DIM ** -0.5
    qb = q.reshape(b, nb, ATTN_BLOCK, N_KV_HEADS, Q_GROUP, HEAD_DIM)
    pad = ((0, 0), (ATTN_BLOCK, ATTN_BLOCK), (0, 0), (0, 0))

    def bands(a):
        ap = jnp.pad(a, pad).reshape(b, nb + 2, ATTN_BLOCK, N_KV_HEADS, HEAD_DIM)
        return jnp.concatenate([ap[:, :-2], ap[:, 1:-1], ap[:, 2:]], axis=2)

    kb, vb = bands(k), bands(v)
    blk = jnp.arange(nb)[:, None] * ATTN_BLOCK
    qpos = blk + jnp.arange(ATTN_BLOCK)[None, :]
    kpos = blk - ATTN_BLOCK + jnp.arange(3 * ATTN_BLOCK)[None, :]
    rel = kpos[:, None, :] - qpos[:, :, None]
    valid = (jnp.abs(rel) <= WINDOW) & (kpos[:, None, :] >= 0) & (kpos[:, None, :] < t)
    s_loc = jnp.einsum('bnqhgd,bnshd->bnhgqs', qb, kb).astype(_f32) * scale
    s_loc = jnp.where(valid[None, :, None, None], s_loc, -jnp.inf)
    s_ctx = jnp.einsum('bnqhgd,bshd->bnhgqs', qb, k_ctx).astype(_f32) * scale
    s = jnp.concatenate([s_loc, s_ctx, _sink_column(sink, s_loc.shape[:-1])], axis=-1)
    pr = jax.nn.softmax(s, axis=-1)
    w = 3 * ATTN_BLOCK
    o = (jnp.einsum('bnhgqs,bnshd->bnqhgd', pr[..., :w], vb)
         + jnp.einsum('bnhgqs,bshd->bnqhgd', pr[..., w:w + lc], v_ctx))
    return o.reshape(b, t, N_HEADS * HEAD_DIM)


def _hgrn2_forget(f_pre, lb):
    x = f_pre.astype(_f32)
    log_f = jnp.logaddexp(jnp.log(lb), jnp.log1p(-lb) + jax.nn.log_sigmoid(x))
    return 1.0 - jnp.exp(log_f), log_f


def _gla_chunkwise(q, k, v, log_f, s0):
    b, t, h, dk = q.shape
    dv = v.shape[-1]
    nc = t // HG_CHUNK
    q, k, log_f = (a.reshape(b, nc, HG_CHUNK, h, dk) for a in (q, k, log_f))
    v = v.reshape(b, nc, HG_CHUNK, h, dv)
    cum = jnp.cumsum(log_f, axis=2)
    last = cum[:, :, -1]
    causal = jnp.tril(jnp.ones((HG_CHUNK, HG_CHUNK), dtype=bool))[None, None, :, :, None, None]
    diff = cum[:, :, :, None] - cum[:, :, None, :]
    decay = jnp.exp(jnp.where(causal, diff, -jnp.inf))
    scores = jnp.einsum('bctshd,bcthd,bcshd->bchts', decay, q, k)
    o_intra = jnp.einsum('bchts,bcshe->bcthe', scores, v)
    kv = jnp.einsum('bcshd,bcshe->bchde', k * jnp.exp(last[:, :, None] - cum), v)

    def step(state, inp):
        g_c, kv_c = inp
        return jnp.exp(g_c)[..., None] * state + kv_c, state

    s_final, s_prev = lax.scan(step, s0, (jnp.moveaxis(last, 1, 0), jnp.moveaxis(kv, 1, 0)))
    s_prev = jnp.moveaxis(s_prev, 0, 1)
    o_inter = jnp.einsum('bcthd,bchde->bcthe', q * jnp.exp(cum), s_prev)
    return (o_intra + o_inter).reshape(b, t, h, dv), s_final


def _hgrn2_bidir(hq, hff, hfb, hi, hg, lb, norm_g, s0):
    b, t = hq.shape[:2]

    def heads(a):
        return a.astype(_f32).reshape(b, t, HG_HEADS, HG_DK)

    def flip(a):
        return jnp.flip(a, axis=1)

    q = heads(jax.nn.silu(hq.astype(_f32)) * HG_DK ** -0.5)
    v = heads(hi)
    k_f, lf_f = _hgrn2_forget(hff, lb[0])
    k_b, lf_b = _hgrn2_forget(hfb, lb[1])
    s0 = s0.astype(_f32)
    o_f, s_f = _gla_chunkwise(q, heads(k_f), v, heads(lf_f), s0[:, 0])
    o_b, s_b = _gla_chunkwise(flip(q), flip(heads(k_b)), flip(v), flip(heads(lf_b)), s0[:, 1])
    o = _rms_norm(o_f + flip(o_b), norm_g) * jax.nn.silu(heads(hg))
    return o.reshape(b, t, HG_W), jnp.stack([s_f, s_b], axis=1)


def _s5_param_kernel(are_ref, aim_ref, ldt_ref, bre_ref, bim_ref, abre_ref, abim_ref, bpre_ref, bpim_ref):
    bre = bre_ref[...]
    bim = bim_ref[...]
    for d in range(2):
        lr = are_ref[d]
        li = aim_ref[d]
        dt = jnp.exp(ldt_ref[d])
        mag = jnp.exp(lr * dt)
        ar = mag * jnp.cos(li * dt)
        ai = mag * jnp.sin(li * dt)
        abre_ref[d] = ar
        abim_ref[d] = ai
        nr = ar - 1.0
        den = lr * lr + li * li
        gr = (nr * lr + ai * li) / den
        gi = (ai * lr - nr * li) / den
        gr3 = gr[:, None, :]
        gi3 = gi[:, None, :]
        bpre_ref[d] = gr3 * bre - gi3 * bim
        bpim_ref[d] = gr3 * bim + gi3 * bre


def _s5_params(a_re, a_im, log_dt, b_re, b_im, c_re, c_im):
    g, pst, ch = S5_GROUPS, S5_STATE, S5_CH
    abre, abim, bpre, bpim = pl.pallas_call(
        _s5_param_kernel,
        out_shape=[
            jax.ShapeDtypeStruct((2, g, pst), _f32),
            jax.ShapeDtypeStruct((2, g, pst), _f32),
            jax.ShapeDtypeStruct((2, g, ch, pst), _f32),
            jax.ShapeDtypeStruct((2, g, ch, pst), _f32),
        ],
        name="s5_params",
    )(a_re, a_im, log_dt.reshape(2, g, 1), b_re.transpose(0, 2, 1), b_im.transpose(0, 2, 1))
    eye = jnp.eye(S5_SLAB_GROUPS, dtype=_f32)

    def b_slabs(bp):
        bp = bp.reshape(2, S5_NSLAB, S5_SLAB_GROUPS, ch, pst)
        w = bp[:, :, :, :, None, :] * eye[None, None, :, None, :, None]
        return w.reshape(2, S5_NSLAB, S5_SLAB_CH, S5_SLAB_ST)

    def c_slabs(cm):
        cm = cm.reshape(S5_NSLAB, S5_SLAB_GROUPS, ch, pst).transpose(0, 1, 3, 2)
        w = cm[:, :, :, None, :] * eye[None, :, None, :, None]
        return w.reshape(S5_NSLAB, S5_SLAB_ST, S5_SLAB_CH)

    b_blk = jnp.concatenate([b_slabs(bpre), b_slabs(bpim)], axis=-1).astype(_bf16)
    return (abre.reshape(2, 1, g * pst), abim.reshape(2, 1, g * pst), b_blk,
            c_slabs(c_re).astype(_bf16), c_slabs(c_im).astype(_bf16))


def _s5_scan_kernel(u_ref, b_ref, cre_ref, cim_ref, are_ref, aim_ref, x0re_ref, x0im_ref,
                    y_ref, xfre_ref, xfim_ref, bu_s, zre_s, zim_s, *, tb_steps, rd):
    tb = pl.program_id(2)
    n_rg = rd // SUBLANES
    st = S5_SLAB_ST

    @pl.when(tb == 0)
    def _():
        zre_s[...] = x0re_ref[...]
        zim_s[...] = x0im_ref[...]

    u = u_ref[...].reshape(tb_steps * rd, S5_SLAB_CH).astype(_bf16)
    bu_s[...] = jnp.dot(u, b_ref[...], preferred_element_type=_f32).reshape(tb_steps, rd, 2 * st)
    are = jnp.broadcast_to(are_ref[...], (SUBLANES, st))
    aim = jnp.broadcast_to(aim_ref[...], (SUBLANES, st))

    def body(t, carry):
        new = []
        for rg in range(n_rg):
            zre, zim = carry[rg]
            rows = slice(rg * SUBLANES, (rg + 1) * SUBLANES)
            nre = are * zre - aim * zim + bu_s[t, rows, 0:st]
            nim = are * zim + aim * zre + bu_s[t, rows, st:2 * st]
            bu_s[t, rows, 0:st] = nre
            bu_s[t, rows, st:2 * st] = nim
            new.append((nre, nim))
        return tuple(new)

    init = tuple((zre_s[rg * SUBLANES:(rg + 1) * SUBLANES, :], zim_s[rg * SUBLANES:(rg + 1) * SUBLANES, :])
                 for rg in range(n_rg))
    fin = lax.fori_loop(0, tb_steps, body, init, unroll=4)
    for rg in range(n_rg):
        zre_s[rg * SUBLANES:(rg + 1) * SUBLANES, :] = fin[rg][0]
        zim_s[rg * SUBLANES:(rg + 1) * SUBLANES, :] = fin[rg][1]

    xs = bu_s[...].reshape(tb_steps * rd, 2 * st)
    y = (jnp.dot(xs[:, 0:st].astype(_bf16), cre_ref[...], preferred_element_type=_f32)
         - jnp.dot(xs[:, st:2 * st].astype(_bf16), cim_ref[...], preferred_element_type=_f32))
    y_ref[...] = y.reshape(tb_steps, rd, S5_SLAB_CH)

    @pl.when(tb == pl.num_programs(2) - 1)
    def _():
        xfre_ref[...] = zre_s[...]
        xfim_ref[...] = zim_s[...]


def _s5_scan(u, params, x0_re, x0_im):
    abre, abim, b_blk, c_re, c_im = params
    _, t_len, rd, _ = u.shape
    tb_steps = S5_BLOCK_ROWS // rd
    n_tb = t_len // tb_steps
    st = S5_SLAB_ST
    grid = (2, S5_NSLAB, n_tb)
    kern = functools.partial(_s5_scan_kernel, tb_steps=tb_steps, rd=rd)
    return pl.pallas_call(
        kern,
        grid=grid,
        in_specs=[
            pl.BlockSpec((None, tb_steps, rd, S5_SLAB_CH), lambda d, s, t: (d, t, 0, s)),
            pl.BlockSpec((None, None, S5_SLAB_CH, 2 * st), lambda d, s, t: (d, s, 0, 0)),
            pl.BlockSpec((None, st, S5_SLAB_CH), lambda d, s, t: (s, 0, 0)),
            pl.BlockSpec((None, st, S5_SLAB_CH), lambda d, s, t: (s, 0, 0)),
            pl.BlockSpec((None, 1, st), lambda d, s, t: (d, 0, s)),
            pl.BlockSpec((None, 1, st), lambda d, s, t: (d, 0, s)),
            pl.BlockSpec((None, rd, st), lambda d, s, t: (d, 0, s)),
            pl.BlockSpec((None, rd, st), lambda d, s, t: (d, 0, s)),
        ],
        out_specs=[
            pl.BlockSpec((None, tb_steps, rd, S5_SLAB_CH), lambda d, s, t: (d, t, 0, s)),
            pl.BlockSpec((None, rd, st), lambda d, s, t: (d, 0, s)),
            pl.BlockSpec((None, rd, st), lambda d, s, t: (d, 0, s)),
        ],
        out_shape=[
            jax.ShapeDtypeStruct((2, t_len, rd, S5_W), _f32),
            jax.ShapeDtypeStruct((2, rd, S5_GROUPS * S5_STATE), _f32),
            jax.ShapeDtypeStruct((2, rd, S5_GROUPS * S5_STATE), _f32),
        ],
        scratch_shapes=[
            pltpu.VMEM((tb_steps, rd, 2 * st), _f32),
            pltpu.VMEM((rd, st), _f32),
            pltpu.VMEM((rd, st), _f32),
        ],
        compiler_params=pltpu.CompilerParams(dimension_semantics=("arbitrary", "arbitrary", "arbitrary"),
                                             vmem_limit_bytes=VMEM_LIMIT),
        name="s5_scan",
    )(u, b_blk, c_re, c_im, abre, abim, x0_re, x0_im)


def _gelu_tanh(x):
    return 0.5 * x * (1.0 + jnp.tanh(math.sqrt(2.0 / math.pi) * (x + 0.044715 * (x * x * x))))


def _s5_out_kernel(yf_ref, yb_ref, u_ref, d_ref, w_ref, b_ref, o_ref, w_s):
    @pl.when(pl.program_id(0) == 0)
    def _():
        w_s[...] = w_ref[...].astype(_bf16)

    y = _gelu_tanh(yf_ref[...] + yb_ref[...] + d_ref[...] * u_ref[...])
    z = jnp.dot(y.astype(_bf16), w_s[...], preferred_element_type=_f32) + b_ref[...]
    o_ref[...] = (y * jax.nn.sigmoid(z)).astype(_bf16)


def _s5_out(yf, yb, u, d_vec, w_glu, b_glu, layer):
    nt = N_TOK // ROW_TILE
    row = lambda i: (i, 0)
    lay = lambda i: (layer, 0, 0)
    return pl.pallas_call(
        _s5_out_kernel,
        grid=(nt,),
        in_specs=[
            pl.BlockSpec((ROW_TILE, S5_W), row),
            pl.BlockSpec((ROW_TILE, S5_W), row),
            pl.BlockSpec((ROW_TILE, S5_W), row),
            pl.BlockSpec((None, 1, S5_W), lay),
            pl.BlockSpec((None, S5_W, S5_W), lay),
            pl.BlockSpec((None, 1, S5_W), lay),
        ],
        out_specs=pl.BlockSpec((ROW_TILE, S5_W), row),
        out_shape=jax.ShapeDtypeStruct((N_TOK, S5_W), _bf16),
        scratch_shapes=[pltpu.VMEM((S5_W, S5_W), _bf16)],
        compiler_params=pltpu.CompilerParams(dimension_semantics=("arbitrary",), vmem_limit_bytes=VMEM_LIMIT),
        name="s5_out",
    )(yf, yb, u, d_vec.reshape(DEPTH, 1, S5_W), w_glu, b_glu.reshape(DEPTH, 1, S5_W))


def _s5_branch(su_c, su_l, sp, s5_re0, s5_im0, layer):
    params = _s5_params(sp['s5_a_re'][layer], sp['s5_a_im'][layer], sp['s5_log_dt'][layer], sp['s5_b_re'][layer],
                        sp['s5_b_im'][layer], sp['s5_c_re'][layer], sp['s5_c_im'][layer])
    nst = S5_GROUPS * S5_STATE

    def both_dirs(u_tm):
        return jnp.stack([u_tm, u_tm[::-1]], axis=0)

    def merge_dirs(y):
        return y[0].transpose(1, 0, 2), y[1, ::-1].transpose(1, 0, 2)

    zeros = jnp.zeros((2, BATCH, nst), _f32)
    y_c, xf_re, xf_im = _s5_scan(both_dirs(su_c.transpose(1, 0, 2)), params, zeros, zeros)
    yf_c, yb_c = merge_dirs(y_c)
    pad = ((0, 0), (0, SUBLANES - DEC_BATCH), (0, 0))
    u_l = both_dirs(jnp.pad(su_l.transpose(1, 0, 2), pad))

    def lat_state(x0):
        return jnp.pad(x0.transpose(1, 0, 2, 3).reshape(2, DEC_BATCH, nst), pad)

    y_l, _, _ = _s5_scan(u_l, params, lat_state(s5_re0), lat_state(s5_im0))
    yf_l, yb_l = merge_dirs(y_l[:, :, :DEC_BATCH])
    yf = jnp.concatenate([yf_c.reshape(N_CTX, S5_W), yf_l.reshape(N_LAT, S5_W)], axis=0)
    yb = jnp.concatenate([yb_c.reshape(N_CTX, S5_W), yb_l.reshape(N_LAT, S5_W)], axis=0)
    u = jnp.concatenate([su_c.reshape(N_CTX, S5_W), su_l.reshape(N_LAT, S5_W)], axis=0)
    s_out = _s5_out(yf, yb, u, sp['s5_d'], sp['s5_w_glu'], sp['s5_b_glu'], layer)

    def ctx_state(xf):
        return xf.reshape(2, BATCH, S5_GROUPS, S5_STATE).transpose(1, 0, 2, 3)

    return s_out, ctx_state(xf_re), ctx_state(xf_im)


def _merge_branches(a_out, h_out, s_out, ga, gh, gs, p):
    y = (jax.nn.sigmoid(ga) * (a_out @ p['w_br_attn'])
         + jax.nn.sigmoid(gh) * (h_out @ p['w_br_hg'])
         + jax.nn.sigmoid(gs) * (s_out @ p['w_br_s5']))
    return y @ p['w_out']


def _legacy_mixer_sublayer(xc, xl, mods, pl_, sp, lb, ck, cv, s_hg0, s5_re0, s5_im0, layer):
    sh1, sc1, gt1 = mods[0], mods[1], mods[2]

    def pre(x, sh, sc):
        h = _rms_norm(x, pl_['norm1_g']) * (1.0 + sc) + sh
        return _split_in(h @ pl_['w_in'])

    zc = pre(xc, sh1[0], sc1[0])
    zl = pre(xl, sh1[1:], sc1[1:])
    s_out, s5_re, s5_im = _s5_branch(zc[8], zl[8], sp, s5_re0, s5_im0, layer)
    s_out_c = s_out[:N_CTX].reshape(BATCH, SEQ, S5_W)
    s_out_l = s_out[N_CTX:].reshape(DEC_BATCH, DEC_SEQ, S5_W)
    q, k, v, hq, hff, hfb, hi, hg, _, ga, gh, gs = zc
    q, k, v = _attn_heads(q, k, v, pl_['q_norm_g'], pl_['k_norm_g'])
    a_out = _context_attention(q, k, v, pl_['sink'])
    z_hg = jnp.zeros((BATCH, 2, HG_HEADS, HG_DK, HG_DV), _f32)
    h_out, s_hg = _hgrn2_bidir(hq, hff, hfb, hi, hg, lb, pl_['hg_norm_g'], z_hg)
    xc = xc + gt1[0] * _merge_branches(a_out, h_out, s_out_c, ga, gh, gs, pl_)
    new = (k, v, s_hg, s5_re, s5_im)
    q, k, v, hq, hff, hfb, hi, hg, _, ga, gh, gs = zl
    q, k, v = _attn_heads(q, k, v, pl_['q_norm_g'], pl_['k_norm_g'])
    a_out = _window_attention(_axial_rope(q), _axial_rope(k), v, ck, cv, pl_['sink'])
    h_out, _ = _hgrn2_bidir(hq, hff, hfb, hi, hg, lb, pl_['hg_norm_g'], s_hg0)
    xl = xl + gt1[1:] * _merge_branches(a_out, h_out, s_out_l, ga, gh, gs, pl_)
    return xc, xl, new


def kernel(x_prompt, x_sample, cache_k, cache_v, state_hgrn, state_s5_re, state_s5_im, c, c_ctx, w_mod, b_mod, norm1_g, norm2_g, w_in, q_norm_g, k_norm_g, attn_sink, hg_lb_logits, hg_norm_g, s5_a_re, s5_a_im, s5_log_dt, s5_b_re, s5_b_im, s5_c_re, s5_c_im, s5_d, s5_w_glu, s5_b_glu, w_br_attn, w_br_hg, w_br_s5, w_out, router_w, router_b, exp_w1, exp_b1, exp_w2, exp_b2):
    lb_all = jnp.cumsum(jax.nn.softmax(hg_lb_logits.astype(_f32), axis=0), axis=0)
    lb_all = lb_all - lb_all[:1]
    cond_rows = jnp.concatenate([c_ctx[None, :], c, jnp.zeros((SUBLANES - N_GROUPS, D_MODEL), _f32)], axis=0)
    x = jnp.concatenate([x_prompt.reshape(N_CTX, D_MODEL), x_sample.reshape(N_LAT, D_MODEL)], axis=0)
    cache_k_t = cache_k.transpose(0, 1, 3, 2, 4)
    cache_v_t = cache_v.transpose(0, 1, 3, 2, 4)
    zero_state = jnp.zeros((BATCH, 2, HG_HEADS, HG_DK, HG_DV), _f32)
    new_k, new_v, new_hg, new_re, new_im = [], [], [], [], []
    moe_p = {"norm2_g": norm2_g, "router_w": router_w, "router_b": router_b, "exp_w1": exp_w1, "exp_b1": exp_b1,
             "exp_w2": exp_w2, "exp_b2": exp_b2}
    s5_p = {'s5_a_re': s5_a_re, 's5_a_im': s5_a_im, 's5_log_dt': s5_log_dt, 's5_b_re': s5_b_re, 's5_b_im': s5_b_im,
            's5_c_re': s5_c_re, 's5_c_im': s5_c_im, 's5_d': s5_d, 's5_w_glu': s5_w_glu, 's5_b_glu': s5_b_glu}
    for l in range(DEPTH):
        m = _modulation(cond_rows, w_mod, b_mod, l)[:N_GROUPS]
        mods = [a[:, None, :] for a in jnp.split(m, N_MOD, axis=-1)]
        sh1, sc1, gt1 = mods[0], mods[1], mods[2]
        z = _in_proj(_norm_mod(x, norm1_g, sc1, sh1, l), w_in, l)
        a_ctx, k_l, v_l = _ctx_attention(z, q_norm_g, k_norm_g, attn_sink, l)
        a_lat = _win_attention(z, cache_k_t, cache_v_t, q_norm_g, k_norm_g, attn_sink, l)
        a_out = jnp.concatenate([a_ctx, a_lat], axis=0)
        o_dirs, hg_fin = _hgrn(z, lb_all[l], jnp.concatenate([zero_state, state_hgrn[:, l]], axis=0))
        h_out = _hgrn_out(o_dirs, z, hg_norm_g, l)
        su = z[:, COL_SU:COL_SU + S5_W]
        s_out, re_l, im_l = _s5_branch(su[:N_CTX].reshape(BATCH, SEQ, S5_W),
                                       su[N_CTX:].reshape(DEC_BATCH, DEC_SEQ, S5_W), s5_p,
                                       state_s5_re[:, l], state_s5_im[:, l], l)
        y = _merge(a_out, h_out, s_out, z, w_br_attn, w_br_hg, w_br_s5, l)
        x = _out_proj(y, w_out, x, gt1, l)
        new_k.append(k_l.reshape(BATCH, SEQ, N_KV_HEADS, HEAD_DIM))
        new_v.append(v_l.reshape(BATCH, SEQ, N_KV_HEADS, HEAD_DIM))
        new_hg.append(hg_fin[:BATCH])
        new_re.append(re_l)
        new_im.append(im_l)
        x = _moe_sublayer(x, mods, moe_p, l)
    return (x[:N_CTX].reshape(BATCH, SEQ, D_MODEL), x[N_CTX:].reshape(DEC_BATCH, DEC_SEQ, D_MODEL),
            jnp.stack(new_k, axis=1), jnp.stack(new_v, axis=1), jnp.stack(new_hg, axis=1),
            jnp.stack(new_re, axis=1), jnp.stack(new_im, axis=1))
```

```python
import functools
import math

import jax
import jax.numpy as jnp
import numpy as np
from jax import lax
from jax.experimental import pallas as pl
from jax.experimental.pallas import tpu as pltpu

D_MODEL = 2048
BATCH = 16
SEQ = 256
DEPTH = 2
DEC_BATCH = 2
DEC_SEQ = 2048
GRID_W = 64
N_HEADS = 8
N_KV_HEADS = 2
HEAD_DIM = 128
Q_GROUP = N_HEADS // N_KV_HEADS
WINDOW = 128
ATTN_BLOCK = 128
ROPE_BASE = 10000.0
ATT_Q_W = N_HEADS * HEAD_DIM
ATT_KV_W = N_KV_HEADS * HEAD_DIM
HG_HEADS = 8
HG_DK = 128
HG_DV = 128
HG_W = HG_HEADS * HG_DK
HG_CHUNK = 16
S5_W = 1024
S5_CH = 16
S5_GROUPS = S5_W // S5_CH
S5_STATE = 64
N_EXPERTS = 32
TOP_K = 4
D_FF = 2048
SWIGLU_ALPHA = 1.702
SWIGLU_LIMIT = 7.0
N_MOD = 6
EPS = 1e-6
IN_SIZES = (ATT_Q_W, ATT_KV_W, ATT_KV_W, HG_W, HG_W, HG_W, HG_W, HG_W, S5_W, D_MODEL, D_MODEL, D_MODEL)
IN_COLS = sum(IN_SIZES)

N_CTX = BATCH * SEQ
N_LAT = DEC_BATCH * DEC_SEQ
N_TOK = N_CTX + N_LAT
N_GROUPS = 1 + DEC_BATCH

LANES = 128
ROW_TILE = 256
VMEM_LIMIT = 56 * 1024 * 1024

MOE_SUPER = 1024
MOE_SUB = 256
MOE_FC = 512
MOE_NC = D_FF // MOE_FC
MOE_NS = (N_TOK * TOP_K) // MOE_SUPER + N_EXPERTS
MOD_TN = 1536
PROJ_TN = 1536
PROJ_TM = 512
MERGE_TN = 512
HG_OUT_W = 512
DISPATCH_TILE = 256
DISPATCH_TOKENS = 64
COMBINE_TILE = 64

S5_SLAB_GROUPS = 8
S5_NSLAB = S5_GROUPS // S5_SLAB_GROUPS
S5_SLAB_CH = S5_SLAB_GROUPS * S5_CH
S5_SLAB_ST = S5_SLAB_GROUPS * S5_STATE
S5_BLOCK_ROWS = 1024
SUBLANES = 8

_bf16 = jnp.bfloat16
_f32 = jnp.float32


def _group_of_tile(i):
    ctx_tiles = N_CTX // ROW_TILE
    lat_tiles = DEC_SEQ // ROW_TILE
    return jnp.where(i < ctx_tiles, 0, 1 + (i - ctx_tiles) // lat_tiles)


def _split_bf16(a):
    hi = a.astype(_bf16)
    lo = (a - hi.astype(_f32)).astype(_bf16)
    return hi, lo


_HI16 = 0xFFFF0000


def _pack_bf16_pairs(h):
    half = h.shape[1] // 2
    bits = pltpu.bitcast(h.astype(_bf16).astype(_f32), jnp.uint32)
    return (bits[:, :half] >> 16) | (bits[:, half:] & jnp.uint32(_HI16))


def _unpack_bf16_pairs(words):
    lo = pltpu.bitcast(words << 16, _f32).astype(_bf16)
    hi = pltpu.bitcast(words & jnp.uint32(_HI16), _f32).astype(_bf16)
    return lo, hi


def _router_kernel(x_ref, g_ref, sc_ref, sh_ref, rw_ref, rb_ref, h_ref, idx_ref, w_ref, rank_ref, cnt_ref, cnt_s):
    x = x_ref[...]
    y = x * lax.rsqrt(jnp.mean(x * x, axis=-1, keepdims=True) + EPS) * g_ref[...]
    h = y * (1.0 + sc_ref[...]) + sh_ref[...]
    h_ref[...] = h
    h_hi, h_lo = _split_bf16(h)
    r_hi, r_lo = _split_bf16(rw_ref[...])
    logits = (jnp.dot(h_hi, r_hi, preferred_element_type=_f32)
              + jnp.dot(h_hi, r_lo, preferred_element_type=_f32)
              + jnp.dot(h_lo, r_hi, preferred_element_type=_f32)) + rb_ref[...]
    lane = lax.broadcasted_iota(jnp.int32, logits.shape, 1)
    work = logits
    vals, idxs = [], []
    for _ in range(TOP_K):
        m = jnp.max(work, axis=-1, keepdims=True)
        i = jnp.min(jnp.where(work == m, lane, N_EXPERTS), axis=-1, keepdims=True)
        vals.append(m)
        idxs.append(i)
        work = jnp.where(lane == i, -jnp.inf, work)
    es = [jnp.exp(v - vals[0]) for v in vals]
    den = es[0] + es[1] + es[2] + es[3]
    @pl.when(pl.program_id(0) == 0)
    def _():
        cnt_s[...] = jnp.zeros(cnt_s.shape, _f32)

    out_lane = lax.broadcasted_iota(jnp.int32, idx_ref.shape, 1)
    tr = lax.broadcasted_iota(jnp.int32, (ROW_TILE, ROW_TILE), 0)
    tc = lax.broadcasted_iota(jnp.int32, (ROW_TILE, ROW_TILE), 1)
    earlier = jnp.where(tc < tr, 1.0, 0.0).astype(_bf16)
    running = cnt_s[...]
    idx_out = jnp.zeros(idx_ref.shape, jnp.int32)
    w_out = jnp.zeros(w_ref.shape, _f32)
    rank_out = jnp.zeros(rank_ref.shape, jnp.int32)
    for k in range(TOP_K):
        onehot = jnp.where(out_lane == idxs[k], 1.0, 0.0)
        before = jnp.dot(earlier, onehot.astype(_bf16), preferred_element_type=_f32)
        rank = jnp.sum(onehot * (running + before), axis=-1, keepdims=True)
        running = running + jnp.sum(onehot, axis=0, keepdims=True)
        idx_out = jnp.where(out_lane == k, idxs[k], idx_out)
        w_out = jnp.where(out_lane == k, es[k] / den, w_out)
        rank_out = jnp.where(out_lane == k, rank.astype(jnp.int32), rank_out)
    cnt_s[...] = running
    idx_ref[...] = idx_out
    w_ref[...] = w_out
    rank_ref[...] = rank_out
    cnt_ref[...] = running


def _router(x, norm_g, sc, sh, router_w, router_b, layer):
    nt = N_TOK // ROW_TILE
    row = lambda i: (i, 0)
    grp = lambda i: (_group_of_tile(i), 0, 0)
    lay = lambda i: (layer, 0, 0)
    h, idx, w, rank, cnt = pl.pallas_call(
        _router_kernel,
        grid=(nt,),
        in_specs=[
            pl.BlockSpec((ROW_TILE, D_MODEL), row),
            pl.BlockSpec((None, 1, D_MODEL), lay),
            pl.BlockSpec((None, 1, D_MODEL), grp),
            pl.BlockSpec((None, 1, D_MODEL), grp),
            pl.BlockSpec((None, D_MODEL, N_EXPERTS), lay),
            pl.BlockSpec((None, 1, N_EXPERTS), lay),
        ],
        out_specs=[
            pl.BlockSpec((ROW_TILE, D_MODEL), row),
            pl.BlockSpec((ROW_TILE, LANES), row),
            pl.BlockSpec((ROW_TILE, LANES), row),
            pl.BlockSpec((ROW_TILE, LANES), row),
            pl.BlockSpec((1, LANES), lambda i: (0, 0)),
        ],
        out_shape=[
            jax.ShapeDtypeStruct((N_TOK, D_MODEL), _f32),
            jax.ShapeDtypeStruct((N_TOK, LANES), jnp.int32),
            jax.ShapeDtypeStruct((N_TOK, LANES), _f32),
            jax.ShapeDtypeStruct((N_TOK, LANES), jnp.int32),
            jax.ShapeDtypeStruct((1, LANES), _f32),
        ],
        scratch_shapes=[pltpu.VMEM((1, LANES), _f32)],
        compiler_params=pltpu.CompilerParams(dimension_semantics=("arbitrary",), vmem_limit_bytes=VMEM_LIMIT),
        name="router",
    )(x, norm_g.reshape(DEPTH, 1, D_MODEL), sc, sh, router_w, router_b.reshape(DEPTH, 1, N_EXPERTS))
    return h, idx[:, :TOP_K], w, rank[:, :TOP_K], cnt[0, :N_EXPERTS].astype(jnp.int32)


def _moe_kernel(exp_ref, rows_ref, used_ref, x_hbm, w1g_ref, w1u_ref, b1g_ref, b1u_ref, w2_ref, b2_ref,
                o_hbm, w1g_s, w1u_s, w2_s, x_s, acc_s, sem):
    s = pl.program_id(0)
    c = pl.program_id(1)
    nvalid = rows_ref[s]
    tile_rows = pl.ds(pl.multiple_of(s * MOE_SUPER, MOE_SUPER), MOE_SUPER)

    def write_out():
        out_copy = pltpu.make_async_copy(acc_s, o_hbm.at[tile_rows], sem)
        out_copy.start()
        out_copy.wait()

    @pl.when(nvalid > 0)
    def _():
        @pl.when(c == 0)
        def _():
            in_copy = pltpu.make_async_copy(x_hbm.at[tile_rows], acc_s, sem)
            in_copy.start()
            in_copy.wait()
            x_s[...] = acc_s[...].astype(_bf16)
            acc_s[...] = jnp.zeros(acc_s.shape, _f32)

        w1g_s[...] = w1g_ref[...].astype(_bf16)
        w1u_s[...] = w1u_ref[...].astype(_bf16)
        w2_s[...] = w2_ref[...].astype(_bf16)
        for r in range(MOE_SUPER // MOE_SUB):
            @pl.when(r * MOE_SUB < nvalid)
            def _():
                rows = pl.ds(r * MOE_SUB, MOE_SUB)
                xs = x_s[rows, :]
                zg = jnp.dot(xs, w1g_s[...], preferred_element_type=_f32) + b1g_ref[...]
                zu = jnp.dot(xs, w1u_s[...], preferred_element_type=_f32) + b1u_ref[...]
                g = jnp.minimum(zg, SWIGLU_LIMIT)
                u = jnp.clip(zu, -SWIGLU_LIMIT, SWIGLU_LIMIT)
                act = g * jax.nn.sigmoid(SWIGLU_ALPHA * g) * (u + 1.0)
                acc_s[rows, :] += jnp.dot(act.astype(_bf16), w2_s[...], preferred_element_type=_f32)

        @pl.when(c == MOE_NC - 1)
        def _():
            acc_s[...] = acc_s[...] + b2_ref[...]
            write_out()

    @pl.when((nvalid == 0) & (c == 0))
    def _():
        acc_s[...] = jnp.zeros(acc_s.shape, _f32)
        write_out()


def _moe_ffn(x_sorted, st_expert, st_rows, n_used, w1, b1, w2, b2, layer):
    def chunk(s, c, u_ref):
        return jnp.where(s < u_ref[0], c, MOE_NC - 1)

    def w1g_map(s, c, e_ref, r_ref, u_ref):
        return (layer, e_ref[s], 0, chunk(s, c, u_ref))

    def w1u_map(s, c, e_ref, r_ref, u_ref):
        return (layer, e_ref[s], 0, MOE_NC + chunk(s, c, u_ref))

    def w2_map(s, c, e_ref, r_ref, u_ref):
        return (layer, e_ref[s], chunk(s, c, u_ref), 0)

    def b2_map(s, c, e_ref, r_ref, u_ref):
        return (layer, e_ref[s], 0, 0)

    grid_spec = pltpu.PrefetchScalarGridSpec(
        num_scalar_prefetch=3,
        grid=(MOE_NS, MOE_NC),
        in_specs=[
            pl.BlockSpec(memory_space=pl.ANY),
            pl.BlockSpec((None, None, D_MODEL, MOE_FC), w1g_map),
            pl.BlockSpec((None, None, D_MODEL, MOE_FC), w1u_map),
            pl.BlockSpec((None, None, 1, MOE_FC), w1g_map),
            pl.BlockSpec((None, None, 1, MOE_FC), w1u_map),
            pl.BlockSpec((None, None, MOE_FC, D_MODEL), w2_map),
            pl.BlockSpec((None, None, 1, D_MODEL), b2_map),
        ],
        out_specs=pl.BlockSpec(memory_space=pl.ANY),
        scratch_shapes=[
            pltpu.VMEM((D_MODEL, MOE_FC), _bf16),
            pltpu.VMEM((D_MODEL, MOE_FC), _bf16),
            pltpu.VMEM((MOE_FC, D_MODEL), _bf16),
            pltpu.VMEM((MOE_SUPER, D_MODEL), _bf16),
            pltpu.VMEM((MOE_SUPER, D_MODEL), _f32),
            pltpu.SemaphoreType.DMA(()),
        ],
    )
    return pl.pallas_call(
        _moe_kernel,
        grid_spec=grid_spec,
        out_shape=jax.ShapeDtypeStruct((MOE_NS * MOE_SUPER, D_MODEL), _f32),
        compiler_params=pltpu.CompilerParams(dimension_semantics=("arbitrary", "arbitrary"),
                                             vmem_limit_bytes=VMEM_LIMIT),
        name="moe_ffn",
    )(st_expert, st_rows, n_used, x_sorted, w1, w1,
      b1.reshape(DEPTH, N_EXPERTS, 1, 2 * D_FF), b1.reshape(DEPTH, N_EXPERTS, 1, 2 * D_FF),
      w2, b2.reshape(DEPTH, N_EXPERTS, 1, D_MODEL))


def _legacy_route_metadata(top_idx, top_w):
    n = N_TOK * TOP_K
    e_flat = top_idx.reshape(n)
    order = jnp.argsort(e_flat, stable=True).astype(jnp.int32)
    e_sorted = e_flat[order]
    grp_end = jnp.searchsorted(e_sorted, jnp.arange(N_EXPERTS, dtype=jnp.int32), side="right").astype(jnp.int32)
    grp_start = jnp.concatenate([jnp.zeros((1,), jnp.int32), grp_end[:-1]])
    counts = grp_end - grp_start
    n_st = (counts + MOE_SUPER - 1) // MOE_SUPER
    st_end = jnp.cumsum(n_st)
    st_start = st_end - n_st
    n_used = st_end[-1]
    s_ids = jnp.arange(MOE_NS, dtype=jnp.int32)
    s_clamped = jnp.minimum(s_ids, n_used - 1)
    st_expert = jnp.searchsorted(st_end, s_clamped, side="right").astype(jnp.int32)
    st_first_rank = (s_clamped - st_start[st_expert]) * MOE_SUPER
    st_rows = jnp.clip(counts[st_expert] - st_first_rank, 0, MOE_SUPER)
    st_rows = jnp.where(s_ids < n_used, st_rows, 0).astype(jnp.int32)
    p_ids = jnp.arange(MOE_NS * MOE_SUPER, dtype=jnp.int32)
    p_tile = p_ids // MOE_SUPER
    p_in_tile = p_ids % MOE_SUPER
    p_valid = p_in_tile < st_rows[p_tile]
    p_entry = jnp.where(p_valid, grp_start[st_expert[p_tile]] + st_first_rank[p_tile] + p_in_tile, 0)
    src_tok = jnp.where(p_valid, order[p_entry] // TOP_K, 0)
    row_w = jnp.where(p_valid, top_w.reshape(n)[order[p_entry]], 0.0)
    inv_order = jnp.argsort(order).astype(jnp.int32)
    rank = inv_order - grp_start[e_flat]
    pos = st_start[e_flat] * MOE_SUPER + rank
    return (src_tok, row_w.reshape(-1, 1), st_expert, st_rows, n_used.reshape(1).astype(jnp.int32),
            pos.reshape(N_TOK, TOP_K))


def _issue_row_copies(idx_ref, src_hbm, dst_ref, sem, n):
    def start(r, carry):
        pltpu.make_async_copy(src_hbm.at[idx_ref[0, r]], dst_ref.at[r], sem).start()
        return carry

    lax.fori_loop(0, n, start, 0, unroll=8)
    pltpu.make_async_copy(src_hbm.at[pl.ds(0, n)], dst_ref, sem).wait()


def _legacy_dispatch_kernel(used_ref, idx_ref, src_hbm, o_ref, sem):
    i = pl.program_id(0)

    @pl.when(i < used_ref[0])
    def _():
        _issue_row_copies(idx_ref, src_hbm, o_ref, sem, DISPATCH_TILE)

    @pl.when(i >= used_ref[0])
    def _():
        o_ref[...] = jnp.zeros(o_ref.shape, o_ref.dtype)


def _legacy_dispatch(h_words, src_tok, n_used_tiles):
    n_rows = src_tok.shape[0]
    nt = n_rows // DISPATCH_TILE
    width = h_words.shape[1]
    grid_spec = pltpu.PrefetchScalarGridSpec(
        num_scalar_prefetch=1,
        grid=(nt,),
        in_specs=[
            pl.BlockSpec((None, 1, DISPATCH_TILE), lambda i, u: (i, 0, 0), memory_space=pltpu.SMEM),
            pl.BlockSpec(memory_space=pl.ANY),
        ],
        out_specs=pl.BlockSpec((DISPATCH_TILE, width), lambda i, u: (i, 0)),
        scratch_shapes=[pltpu.SemaphoreType.DMA(())],
    )
    return pl.pallas_call(
        _dispatch_kernel,
        grid_spec=grid_spec,
        out_shape=jax.ShapeDtypeStruct((n_rows, width), h_words.dtype),
        compiler_params=pltpu.CompilerParams(dimension_semantics=("arbitrary",), vmem_limit_bytes=VMEM_LIMIT),
        name="moe_dispatch",
    )(n_used_tiles, src_tok.reshape(nt, 1, DISPATCH_TILE), h_words)


def _route_tables(top_idx, rank, counts):
    n_st = (counts + MOE_SUPER - 1) // MOE_SUPER
    st_end = jnp.cumsum(n_st)
    st_start = st_end - n_st
    n_used = st_end[-1]
    s_ids = jnp.arange(MOE_NS, dtype=jnp.int32)
    s_clamped = jnp.minimum(s_ids, n_used - 1)
    st_expert = jnp.sum((st_end[None, :] <= s_clamped[:, None]).astype(jnp.int32), axis=1)
    is_exp = st_expert[:, None] == jnp.arange(N_EXPERTS, dtype=jnp.int32)[None, :]
    tile_in_expert = s_clamped - jnp.sum(jnp.where(is_exp, st_start[None, :], 0), axis=1)
    st_rows = jnp.clip(jnp.sum(jnp.where(is_exp, counts[None, :], 0), axis=1) - tile_in_expert * MOE_SUPER,
                       0, MOE_SUPER)
    st_rows = jnp.where(s_ids < n_used, st_rows, 0).astype(jnp.int32)
    first_row = jnp.sum(jnp.where(top_idx[:, :, None] == jnp.arange(N_EXPERTS, dtype=jnp.int32),
                                  st_start * MOE_SUPER, 0), axis=-1)
    return st_expert, st_rows, n_used.reshape(1).astype(jnp.int32), first_row + rank


def _dispatch_kernel(pos_ref, h_ref, zero_hbm, o_hbm, sem):
    n = TOP_K * DISPATCH_TOKENS

    def start(j, carry):
        pltpu.make_async_copy(h_ref.at[j // TOP_K], o_hbm.at[pos_ref[0, j]], sem).start()
        return carry

    lax.fori_loop(0, n, start, 0, unroll=8)
    pltpu.make_async_copy(o_hbm.at[pl.ds(0, n)], o_hbm.at[pl.ds(0, n)], sem).wait()


def _dispatch(h_words, pos):
    nt = N_TOK // DISPATCH_TOKENS
    width = h_words.shape[1]
    n_rows = MOE_NS * MOE_SUPER
    return pl.pallas_call(
        _dispatch_kernel,
        grid=(nt,),
        in_specs=[
            pl.BlockSpec((None, 1, TOP_K * DISPATCH_TOKENS), lambda i: (i, 0, 0), memory_space=pltpu.SMEM),
            pl.BlockSpec((DISPATCH_TOKENS, width), lambda i: (i, 0)),
            pl.BlockSpec(memory_space=pl.ANY),
        ],
        out_specs=pl.BlockSpec(memory_space=pl.ANY),
        out_shape=jax.ShapeDtypeStruct((n_rows, width), h_words.dtype),
        scratch_shapes=[pltpu.SemaphoreType.DMA(())],
        input_output_aliases={2: 0},
        compiler_params=pltpu.CompilerParams(dimension_semantics=("arbitrary",), vmem_limit_bytes=VMEM_LIMIT),
        name="moe_dispatch",
    )(pos.reshape(nt, 1, TOP_K * DISPATCH_TOKENS), h_words, jnp.zeros((n_rows, width), h_words.dtype))


def _combine_kernel(pos_ref, y_hbm, x_ref, gt_ref, w_ref, o_ref, buf, sem):
    _issue_row_copies(pos_ref, y_hbm, buf, sem, TOP_K * COMBINE_TILE)
    y = w_ref[:, 0:1] * buf[0:COMBINE_TILE, :]
    for k in range(1, TOP_K):
        y = y + w_ref[:, k:k + 1] * buf[k * COMBINE_TILE:(k + 1) * COMBINE_TILE, :]
    o_ref[...] = x_ref[...] + gt_ref[...] * y


def _combine(y_sorted, pos, top_w, x, gt):
    nt = N_TOK // COMBINE_TILE
    tiles_per_row_tile = ROW_TILE // COMBINE_TILE
    pos_tiles = pos.reshape(nt, COMBINE_TILE, TOP_K).transpose(0, 2, 1).reshape(nt, 1, TOP_K * COMBINE_TILE)
    return pl.pallas_call(
        _combine_kernel,
        grid=(nt,),
        in_specs=[
            pl.BlockSpec((None, 1, TOP_K * COMBINE_TILE), lambda i: (i, 0, 0), memory_space=pltpu.SMEM),
            pl.BlockSpec(memory_space=pl.ANY),
            pl.BlockSpec((COMBINE_TILE, D_MODEL), lambda i: (i, 0)),
            pl.BlockSpec((None, 1, D_MODEL), lambda i: (_group_of_tile(i // tiles_per_row_tile), 0, 0)),
            pl.BlockSpec((COMBINE_TILE, LANES), lambda i: (i, 0)),
        ],
        out_specs=pl.BlockSpec((COMBINE_TILE, D_MODEL), lambda i: (i, 0)),
        out_shape=jax.ShapeDtypeStruct((N_TOK, D_MODEL), _f32),
        scratch_shapes=[pltpu.VMEM((TOP_K * COMBINE_TILE, D_MODEL), _f32), pltpu.SemaphoreType.DMA(())],
        compiler_params=pltpu.CompilerParams(dimension_semantics=("arbitrary",), vmem_limit_bytes=VMEM_LIMIT),
        name="moe_combine",
    )(pos_tiles, y_sorted, x, gt, top_w)


def _moe_sublayer(x, mods, p, layer):
    sh2, sc2, gt2 = mods[3], mods[4], mods[5]
    h, top_idx, top_w, rank, counts = _router(x, p["norm2_g"], sc2, sh2, p["router_w"], p["router_b"], layer)
    st_expert, st_rows, n_used, pos = _route_tables(top_idx, rank, counts)
    x_sorted = _dispatch(h, pos)
    y_sorted = _moe_ffn(x_sorted, st_expert, st_rows, n_used, p["exp_w1"], p["exp_b1"], p["exp_w2"], p["exp_b2"],
                        layer)
    return _combine(y_sorted, pos, top_w, x, gt2)


def _group_of_rows(i, tile):
    return jnp.where(i < N_CTX // tile, 0, 1 + (i - N_CTX // tile) // (DEC_SEQ // tile))


def _mod_kernel(c_ref, w_ref, b_ref, o_ref):
    c = c_ref[...]
    a = (c * jax.nn.sigmoid(c)).astype(_bf16)
    o_ref[...] = jnp.dot(a, w_ref[...].astype(_bf16), preferred_element_type=_f32) + b_ref[...]


def _modulation(cond_rows, w_mod, b_mod, layer):
    n = N_MOD * D_MODEL
    return pl.pallas_call(
        _mod_kernel,
        grid=(n // MOD_TN,),
        in_specs=[
            pl.BlockSpec((SUBLANES, D_MODEL), lambda j: (0, 0)),
            pl.BlockSpec((None, D_MODEL, MOD_TN), lambda j: (layer, 0, j)),
            pl.BlockSpec((None, 1, MOD_TN), lambda j: (layer, 0, j)),
        ],
        out_specs=pl.BlockSpec((SUBLANES, MOD_TN), lambda j: (0, j)),
        out_shape=jax.ShapeDtypeStruct((SUBLANES, n), _f32),
        compiler_params=pltpu.CompilerParams(dimension_semantics=("arbitrary",), vmem_limit_bytes=VMEM_LIMIT),
        name="modulation",
    )(cond_rows, w_mod, b_mod.reshape(DEPTH, 1, n))


def _norm_mod_kernel(x_ref, g_ref, sc_ref, sh_ref, h_ref):
    x = x_ref[...]
    y = x * lax.rsqrt(jnp.mean(x * x, axis=-1, keepdims=True) + EPS) * g_ref[...]
    h_ref[...] = (y * (1.0 + sc_ref[...]) + sh_ref[...]).astype(_bf16)


def _norm_mod(x, norm_g, sc, sh, layer):
    row = lambda i: (i, 0)
    grp = lambda i: (_group_of_tile(i), 0, 0)
    lay = lambda i: (layer, 0, 0)
    return pl.pallas_call(
        _norm_mod_kernel,
        grid=(N_TOK // ROW_TILE,),
        in_specs=[
            pl.BlockSpec((ROW_TILE, D_MODEL), row),
            pl.BlockSpec((None, 1, D_MODEL), lay),
            pl.BlockSpec((None, 1, D_MODEL), grp),
            pl.BlockSpec((None, 1, D_MODEL), grp),
        ],
        out_specs=pl.BlockSpec((ROW_TILE, D_MODEL), row),
        out_shape=jax.ShapeDtypeStruct((N_TOK, D_MODEL), _bf16),
        compiler_params=pltpu.CompilerParams(dimension_semantics=("arbitrary",), vmem_limit_bytes=VMEM_LIMIT),
        name="norm_mod",
    )(x, norm_g.reshape(DEPTH, 1, D_MODEL), sc, sh)


def _in_proj_kernel(h_ref, w_ref, o_ref, w_s):
    @pl.when(pl.program_id(1) == 0)
    def _():
        w_s[...] = w_ref[...].astype(_bf16)

    o_ref[...] = jnp.dot(h_ref[...], w_s[...], preferred_element_type=_f32)


def _in_proj(h, w_in, layer):
    return pl.pallas_call(
        _in_proj_kernel,
        grid=(IN_COLS // PROJ_TN, N_TOK // PROJ_TM),
        in_specs=[
            pl.BlockSpec((PROJ_TM, D_MODEL), lambda j, i: (i, 0)),
            pl.BlockSpec((None, D_MODEL, PROJ_TN), lambda j, i: (layer, 0, j)),
        ],
        out_specs=pl.BlockSpec((PROJ_TM, PROJ_TN), lambda j, i: (i, j)),
        out_shape=jax.ShapeDtypeStruct((N_TOK, IN_COLS), _f32),
        scratch_shapes=[pltpu.VMEM((D_MODEL, PROJ_TN), _bf16)],
        compiler_params=pltpu.CompilerParams(dimension_semantics=("arbitrary", "arbitrary"),
                                             vmem_limit_bytes=VMEM_LIMIT),
        name="in_proj",
    )(h, w_in)


def _merge_kernel(a_ref, h_ref, s_ref, ga_ref, gh_ref, gs_ref, wa_ref, wh_ref, ws_ref, o_ref, wa_s, wh_s, ws_s):
    @pl.when(pl.program_id(1) == 0)
    def _():
        wa_s[...] = wa_ref[...].astype(_bf16)
        wh_s[...] = wh_ref[...].astype(_bf16)
        ws_s[...] = ws_ref[...].astype(_bf16)

    y = (jax.nn.sigmoid(ga_ref[...]) * jnp.dot(a_ref[...], wa_s[...], preferred_element_type=_f32)
         + jax.nn.sigmoid(gh_ref[...]) * jnp.dot(h_ref[...], wh_s[...], preferred_element_type=_f32)
         + jax.nn.sigmoid(gs_ref[...]) * jnp.dot(s_ref[...], ws_s[...], preferred_element_type=_f32))
    o_ref[...] = y.astype(_bf16)


def _merge(a_out, h_out, s_out, z, w_a, w_h, w_s, layer):
    ga0, gh0, gs0 = (sum(IN_SIZES[:k]) // MERGE_TN for k in (9, 10, 11))
    act = lambda j, i: (i, 0)
    wmap = lambda j, i: (layer, 0, j)
    return pl.pallas_call(
        _merge_kernel,
        grid=(D_MODEL // MERGE_TN, N_TOK // ROW_TILE),
        in_specs=[
            pl.BlockSpec((ROW_TILE, ATT_Q_W), act),
            pl.BlockSpec((ROW_TILE, HG_W), act),
            pl.BlockSpec((ROW_TILE, S5_W), act),
            pl.BlockSpec((ROW_TILE, MERGE_TN), lambda j, i: (i, ga0 + j)),
            pl.BlockSpec((ROW_TILE, MERGE_TN), lambda j, i: (i, gh0 + j)),
            pl.BlockSpec((ROW_TILE, MERGE_TN), lambda j, i: (i, gs0 + j)),
            pl.BlockSpec((None, ATT_Q_W, MERGE_TN), wmap),
            pl.BlockSpec((None, HG_W, MERGE_TN), wmap),
            pl.BlockSpec((None, S5_W, MERGE_TN), wmap),
        ],
        out_specs=pl.BlockSpec((ROW_TILE, MERGE_TN), lambda j, i: (i, j)),
        out_shape=jax.ShapeDtypeStruct((N_TOK, D_MODEL), _bf16),
        scratch_shapes=[pltpu.VMEM((ATT_Q_W, MERGE_TN), _bf16), pltpu.VMEM((HG_W, MERGE_TN), _bf16),
                        pltpu.VMEM((S5_W, MERGE_TN), _bf16)],
        compiler_params=pltpu.CompilerParams(dimension_semantics=("arbitrary", "arbitrary"),
                                             vmem_limit_bytes=VMEM_LIMIT),
        name="merge",
    )(a_out, h_out, s_out, z, z, z, w_a, w_h, w_s)


def _out_proj_kernel(y_ref, w_ref, x_ref, gt_ref, o_ref, w_s):
    @pl.when(pl.program_id(1) == 0)
    def _():
        w_s[...] = w_ref[...].astype(_bf16)

    o_ref[...] = x_ref[...] + gt_ref[...] * jnp.dot(y_ref[...], w_s[...], preferred_element_type=_f32)


def _out_proj(y, w_out, x, gt, layer):
    return pl.pallas_call(
        _out_proj_kernel,
        grid=(D_MODEL // MERGE_TN, N_TOK // ROW_TILE),
        in_specs=[
            pl.BlockSpec((ROW_TILE, D_MODEL), lambda j, i: (i, 0)),
            pl.BlockSpec((None, D_MODEL, MERGE_TN), lambda j, i: (layer, 0, j)),
            pl.BlockSpec((ROW_TILE, MERGE_TN), lambda j, i: (i, j)),
            pl.BlockSpec((None, 1, MERGE_TN), lambda j, i: (_group_of_tile(i), 0, j)),
        ],
        out_specs=pl.BlockSpec((ROW_TILE, MERGE_TN), lambda j, i: (i, j)),
        out_shape=jax.ShapeDtypeStruct((N_TOK, D_MODEL), _f32),
        scratch_shapes=[pltpu.VMEM((D_MODEL, MERGE_TN), _bf16)],
        compiler_params=pltpu.CompilerParams(dimension_semantics=("arbitrary", "arbitrary"),
                                             vmem_limit_bytes=VMEM_LIMIT),
        name="out_proj",
    )(y, w_out, x, gt)


COL_Q, COL_K, COL_V, COL_HQ, COL_HFF, COL_HFB, COL_HI, COL_HG, COL_SU = (sum(IN_SIZES[:k]) for k in range(9))
KV_GROUP_W = Q_GROUP * HEAD_DIM


def _head_norm(x, g):
    return x * lax.rsqrt(jnp.mean(x * x, axis=-1, keepdims=True) + EPS) * g


def _dot_nt(a, b):
    return lax.dot_general(a, b, (((1,), (1,)), ((), ())), preferred_element_type=_f32)


def _softmax_with_sink(s, sink):
    m = jnp.maximum(jnp.max(s, axis=-1, keepdims=True), sink)
    p = jnp.exp(s - m)
    den = jnp.sum(p, axis=-1, keepdims=True) + jnp.exp(sink - m)
    return p / den


def _ctx_attn_kernel(q_ref, k_ref, v_ref, qg_ref, kg_ref, sink_ref, o_ref, ko_ref, vo_ref):
    kn = _head_norm(k_ref[...], kg_ref[...])
    v = v_ref[...]
    ko_ref[...] = kn
    vo_ref[...] = v
    knb = kn.astype(_bf16)
    vb = v.astype(_bf16)
    for g in range(Q_GROUP):
        cols = slice(g * HEAD_DIM, (g + 1) * HEAD_DIM)
        qn = _head_norm(q_ref[:, cols], qg_ref[...])
        s = _dot_nt(qn.astype(_bf16), knb) * HEAD_DIM ** -0.5
        p = _softmax_with_sink(s, sink_ref[0:1, g:g + 1])
        o_ref[:, cols] = jnp.dot(p.astype(_bf16), vb, preferred_element_type=_f32).astype(_bf16)


def _ctx_attention(z, q_g, k_g, sink, layer):
    lay = lambda b, j: (layer, 0, 0)
    return pl.pallas_call(
        _ctx_attn_kernel,
        grid=(BATCH, N_KV_HEADS),
        in_specs=[
            pl.BlockSpec((SEQ, KV_GROUP_W), lambda b, j: (b, COL_Q // KV_GROUP_W + j)),
            pl.BlockSpec((SEQ, HEAD_DIM), lambda b, j: (b, COL_K // HEAD_DIM + j)),
            pl.BlockSpec((SEQ, HEAD_DIM), lambda b, j: (b, COL_V // HEAD_DIM + j)),
            pl.BlockSpec((None, 1, HEAD_DIM), lay),
            pl.BlockSpec((None, 1, HEAD_DIM), lay),
            pl.BlockSpec((None, None, 1, Q_GROUP), lambda b, j: (layer, j, 0, 0)),
        ],
        out_specs=[
            pl.BlockSpec((SEQ, KV_GROUP_W), lambda b, j: (b, j)),
            pl.BlockSpec((SEQ, HEAD_DIM), lambda b, j: (b, j)),
            pl.BlockSpec((SEQ, HEAD_DIM), lambda b, j: (b, j)),
        ],
        out_shape=[
            jax.ShapeDtypeStruct((N_CTX, ATT_Q_W), _bf16),
            jax.ShapeDtypeStruct((N_CTX, ATT_KV_W), _f32),
            jax.ShapeDtypeStruct((N_CTX, ATT_KV_W), _f32),
        ],
        compiler_params=pltpu.CompilerParams(dimension_semantics=("arbitrary", "arbitrary"),
                                             vmem_limit_bytes=VMEM_LIMIT),
        name="ctx_attention",
    )(z, z, z, q_g.reshape(DEPTH, 1, HEAD_DIM), k_g.reshape(DEPTH, 1, HEAD_DIM),
      sink.reshape(DEPTH, N_KV_HEADS, 1, Q_GROUP))


def _rope_tables():
    half = HEAD_DIM // 2
    nf = half // 2
    t = np.arange(DEC_SEQ)
    pos = np.stack([t // GRID_W, t % GRID_W], axis=1).astype(np.float64)
    inv = ROPE_BASE ** (-np.arange(nf, dtype=np.float64) / nf)
    dim = np.arange(HEAD_DIM)
    ang = pos[:, dim // half] * inv[dim % nf][None, :]
    sign = np.where((dim % half) < nf, -1.0, 1.0)
    return jnp.asarray(np.cos(ang), _f32), jnp.asarray(np.sin(ang) * sign, _f32)


def _rope(x, cos, sin_signed):
    nf = HEAD_DIM // 4
    lane = lax.broadcasted_iota(jnp.int32, x.shape, 1)
    partner = jnp.where((lane % (2 * nf)) < nf, pltpu.roll(x, HEAD_DIM - nf, 1), pltpu.roll(x, nf, 1))
    return x * cos + partner * sin_signed


def _win_attn_kernel(q_ref, kp_ref, kc_ref, kn_ref, vp_ref, vc_ref, vn_ref, cq_ref, sq_ref, cp_ref, sp_ref,
                     cn_ref, sn_ref, ck_ref, cv_ref, qg_ref, kg_ref, sink_ref, o_ref):
    n = pl.program_id(1)
    nb = pl.num_programs(1)
    kg = kg_ref[...]
    k_all = jnp.concatenate([
        _rope(_head_norm(kp_ref[...], kg), cp_ref[...], sp_ref[...]),
        _rope(_head_norm(kc_ref[...], kg), cq_ref[...], sq_ref[...]),
        _rope(_head_norm(kn_ref[...], kg), cn_ref[...], sn_ref[...]),
        ck_ref[...]], axis=0).astype(_bf16)
    v_all = jnp.concatenate([vp_ref[...], vc_ref[...], vn_ref[...], cv_ref[...]], axis=0).astype(_bf16)
    r = lax.broadcasted_iota(jnp.int32, (ATTN_BLOCK, ATTN_BLOCK), 0)
    c = lax.broadcasted_iota(jnp.int32, (ATTN_BLOCK, ATTN_BLOCK), 1)
    neg = jnp.full((ATTN_BLOCK, ATTN_BLOCK), -jnp.inf, _f32)
    zero = jnp.zeros((ATTN_BLOCK, ATTN_BLOCK), _f32)
    bias = jnp.concatenate([
        jnp.where(n > 0, jnp.where(c >= r, zero, neg), neg),
        zero,
        jnp.where(n < nb - 1, jnp.where(c <= r, zero, neg), neg),
        jnp.zeros((ATTN_BLOCK, ck_ref.shape[0]), _f32)], axis=1)
    for g in range(Q_GROUP):
        cols = slice(g * HEAD_DIM, (g + 1) * HEAD_DIM)
        qn = _rope(_head_norm(q_ref[:, cols], qg_ref[...]), cq_ref[...], sq_ref[...])
        s = _dot_nt(qn.astype(_bf16), k_all) * HEAD_DIM ** -0.5 + bias
        p = _softmax_with_sink(s, sink_ref[0:1, g:g + 1])
        o_ref[:, cols] = jnp.dot(p.astype(_bf16), v_all, preferred_element_type=_f32).astype(_bf16)


def _win_attention(z, cache_k, cache_v, q_g, k_g, sink, layer):
    cos, sin = _rope_tables()
    nb = DEC_SEQ // ATTN_BLOCK
    base = N_CTX // ATTN_BLOCK
    prev = lambda n: jnp.maximum(n - 1, 0)
    nxt = lambda n: jnp.minimum(n + 1, nb - 1)
    past = cache_k.shape[3]

    def zrow(sel, col0):
        return pl.BlockSpec((ATTN_BLOCK, HEAD_DIM), lambda b, n, j: (base + b * nb + sel(n), col0 // HEAD_DIM + j))

    def tab(sel):
        return pl.BlockSpec((ATTN_BLOCK, HEAD_DIM), lambda b, n, j: (sel(n), 0))

    same = lambda n: n
    lay = lambda b, n, j: (layer, 0, 0)
    cache = pl.BlockSpec((None, None, None, past, HEAD_DIM), lambda b, n, j: (b, layer, j, 0, 0))
    return pl.pallas_call(
        _win_attn_kernel,
        grid=(DEC_BATCH, nb, N_KV_HEADS),
        in_specs=[
            pl.BlockSpec((ATTN_BLOCK, KV_GROUP_W), lambda b, n, j: (base + b * nb + n, COL_Q // KV_GROUP_W + j)),
            zrow(prev, COL_K), zrow(same, COL_K), zrow(nxt, COL_K),
            zrow(prev, COL_V), zrow(same, COL_V), zrow(nxt, COL_V),
            tab(same), tab(same), tab(prev), tab(prev), tab(nxt), tab(nxt),
            cache, cache,
            pl.BlockSpec((None, 1, HEAD_DIM), lay),
            pl.BlockSpec((None, 1, HEAD_DIM), lay),
            pl.BlockSpec((None, None, 1, Q_GROUP), lambda b, n, j: (layer, j, 0, 0)),
        ],
        out_specs=pl.BlockSpec((ATTN_BLOCK, KV_GROUP_W), lambda b, n, j: (b * nb + n, j)),
        out_shape=jax.ShapeDtypeStruct((N_LAT, ATT_Q_W), _bf16),
        compiler_params=pltpu.CompilerParams(dimension_semantics=("arbitrary", "arbitrary", "arbitrary"),
                                             vmem_limit_bytes=VMEM_LIMIT),
        name="win_attention",
    )(z, z, z, z, z, z, z, cos, sin, cos, sin, cos, sin, cache_k, cache_v,
      q_g.reshape(DEPTH, 1, HEAD_DIM), k_g.reshape(DEPTH, 1, HEAD_DIM), sink.reshape(DEPTH, N_KV_HEADS, 1, Q_GROUP))


HG_BLOCK = 128
HG_LEVELS = 7
HG_NSEG = HG_LEVELS + 2
HG_HEADS_PER_STEP = 4


def _hgrn_constants():
    c = HG_BLOCK
    i = np.arange(c)
    seg = np.zeros((HG_NSEG, c, c), np.float32)
    mask = np.zeros((HG_LEVELS + 1, c, c), np.float32)
    mask[0] = np.eye(c)
    for lv in range(HG_LEVELS):
        b = 2 << lv
        h = b // 2
        mid = (i // b) * b + h
        second = (i % b) >= h
        for r in range(c):
            if second[r]:
                seg[lv, r, mid[r]:r + 1] = 1.0
            else:
                seg[lv, r, r + 1:mid[r]] = 1.0
        same = (i[:, None] // b) == (i[None, :] // b)
        mask[lv + 1] = same & second[:, None] & ~second[None, :]
    seg[HG_LEVELS] = np.tril(np.ones((c, c)))
    seg[HG_LEVELS + 1] = np.triu(np.ones((c, c)), 1)
    seg2 = np.stack([seg, seg[:, ::-1, ::-1]]).reshape(2, HG_NSEG * c, c)
    mask2 = np.stack([mask, mask[:, ::-1, ::-1]])
    return jnp.asarray(seg2, _bf16), jnp.asarray(mask2, _f32)


def _hgrn_schedule():
    rows, first, last, seq = [[], []], [], [], []
    for s in range(BATCH + DEC_BATCH):
        n = (SEQ if s < BATCH else DEC_SEQ) // HG_BLOCK
        base = s * (SEQ // HG_BLOCK) if s < BATCH else N_CTX // HG_BLOCK + (s - BATCH) * (DEC_SEQ // HG_BLOCK)
        for cpos in range(n):
            rows[0].append(base + cpos)
            rows[1].append(base + n - 1 - cpos)
            first.append(int(cpos == 0))
            last.append(int(cpos == n - 1))
            seq.append(s)
    as_i32 = lambda a: jnp.asarray(np.asarray(a, np.int32).reshape(-1))
    return as_i32(rows), as_i32(first), as_i32(last), as_i32(seq), len(seq)


def _hgrn_kernel(rows_ref, first_ref, last_ref, seq_ref, q_ref, f_ref, i_ref, lb_ref, seg_ref, mask_ref, s0_ref,
                 o_ref, sf_ref, st_s):
    job = pl.program_id(2)
    seg = seg_ref[...]

    @pl.when(first_ref[job] == 1)
    def _():
        for hh in range(HG_HEADS_PER_STEP):
            st_s[hh] = s0_ref[hh].T

    for hh in range(HG_HEADS_PER_STEP):
        cols = slice(hh * HG_DK, (hh + 1) * HG_DK)
        xq = q_ref[:, cols]
        q = xq * jax.nn.sigmoid(xq) * HG_DK ** -0.5
        xf = f_ref[:, cols]
        log_sig = jnp.minimum(xf, 0.0) - jnp.log(1.0 + jnp.exp(-jnp.abs(xf)))
        lb = lb_ref[hh]
        f = lb + (1.0 - lb) * jnp.exp(log_sig)
        log_f = jnp.where(lb > 0.0, jnp.log(f), log_sig)
        k = 1.0 - f
        v = i_ref[:, cols]
        vb = v.astype(_bf16)
        lf_hi, lf_lo = _split_bf16(log_f)
        decay = jnp.exp(jnp.dot(seg, lf_hi, preferred_element_type=_f32)
                        + jnp.dot(seg, lf_lo, preferred_element_type=_f32))
        scores = _dot_nt(q.astype(_bf16), k.astype(_bf16)) * mask_ref[0]
        for lv in range(HG_LEVELS):
            e = decay[lv * HG_BLOCK:(lv + 1) * HG_BLOCK]
            scores = scores + _dot_nt((q * e).astype(_bf16), (k * e).astype(_bf16)) * mask_ref[lv + 1]
        e_q = decay[HG_LEVELS * HG_BLOCK:(HG_LEVELS + 1) * HG_BLOCK]
        e_k = decay[(HG_LEVELS + 1) * HG_BLOCK:(HG_LEVELS + 2) * HG_BLOCK]
        st = st_s[hh]
        o_ref[:, cols] = (jnp.dot(scores.astype(_bf16), vb, preferred_element_type=_f32)
                          + _dot_nt((q * e_q).astype(_bf16), st.astype(_bf16)))
        total = jnp.sum(log_f, axis=0, keepdims=True)
        st_new = jnp.exp(total) * st + jnp.dot(v.T.astype(_bf16), (k * e_k).astype(_bf16),
                                               preferred_element_type=_f32)
        st_s[hh] = st_new

    @pl.when(last_ref[job] == 1)
    def _():
        for hh in range(HG_HEADS_PER_STEP):
            sf_ref[hh] = st_s[hh].T


def _hgrn(z, lb, s0):
    seg, mask = _hgrn_constants()
    rows, first, last, seq, n_jobs = _hgrn_schedule()

    hps = HG_HEADS_PER_STEP
    step_w = hps * HG_DK

    def zcol(col0):
        return pl.BlockSpec((HG_BLOCK, step_w),
                            lambda h, d, j, r, f, l, s: (r[d * n_jobs + j], col0 // step_w + h))

    state = pl.BlockSpec((None, None, hps, HG_DK, HG_DV), lambda h, d, j, r, f, l, s: (s[j], d, h, 0, 0))
    grid_spec = pltpu.PrefetchScalarGridSpec(
        num_scalar_prefetch=4,
        grid=(HG_HEADS // hps, 2, n_jobs),
        in_specs=[
            zcol(COL_HQ),
            pl.BlockSpec((HG_BLOCK, step_w),
                         lambda h, d, j, r, f, l, s: (r[d * n_jobs + j], (COL_HFF + d * HG_W) // step_w + h)),
            zcol(COL_HI),
            pl.BlockSpec((None, hps, 1, HG_DK), lambda h, d, j, r, f, l, s: (d, h, 0, 0)),
            pl.BlockSpec((None, HG_NSEG * HG_BLOCK, HG_BLOCK), lambda h, d, j, r, f, l, s: (d, 0, 0)),
            pl.BlockSpec((None, HG_LEVELS + 1, HG_BLOCK, HG_BLOCK), lambda h, d, j, r, f, l, s: (d, 0, 0, 0)),
            state,
        ],
        out_specs=[
            pl.BlockSpec((None, HG_BLOCK, step_w), lambda h, d, j, r, f, l, s: (d, r[d * n_jobs + j], h)),
            state,
        ],
        scratch_shapes=[pltpu.VMEM((hps, HG_DV, HG_DK), _f32)],
    )
    return pl.pallas_call(
        _hgrn_kernel,
        grid_spec=grid_spec,
        out_shape=[
            jax.ShapeDtypeStruct((2, N_TOK, HG_W), _f32),
            jax.ShapeDtypeStruct(s0.shape, _f32),
        ],
        compiler_params=pltpu.CompilerParams(dimension_semantics=("arbitrary", "arbitrary", "arbitrary"),
                                             vmem_limit_bytes=VMEM_LIMIT),
        name="hgrn",
    )(rows, first, last, seq, z, z, z, lb.reshape(2, HG_HEADS, 1, HG_DK), seg, mask, s0)


def _hgrn_out_kernel(of_ref, ob_ref, g_ref, ng_ref, o_ref):
    ng = ng_ref[...]
    for hh in range(HG_OUT_W // HG_DV):
        cols = slice(hh * HG_DV, (hh + 1) * HG_DV)
        o = _head_norm(of_ref[:, cols] + ob_ref[:, cols], ng)
        xg = g_ref[:, cols]
        o_ref[:, cols] = (o * (xg * jax.nn.sigmoid(xg))).astype(_bf16)


def _hgrn_out(o_dirs, z, norm_g, layer):
    return pl.pallas_call(
        _hgrn_out_kernel,
        grid=(N_TOK // ROW_TILE, HG_W // HG_OUT_W),
        in_specs=[
            pl.BlockSpec((None, ROW_TILE, HG_OUT_W), lambda i, j: (0, i, j)),
            pl.BlockSpec((None, ROW_TILE, HG_OUT_W), lambda i, j: (1, i, j)),
            pl.BlockSpec((ROW_TILE, HG_OUT_W), lambda i, j: (i, COL_HG // HG_OUT_W + j)),
            pl.BlockSpec((None, 1, HG_DV), lambda i, j: (layer, 0, 0)),
        ],
        out_specs=pl.BlockSpec((ROW_TILE, HG_OUT_W), lambda i, j: (i, j)),
        out_shape=jax.ShapeDtypeStruct((N_TOK, HG_W), _bf16),
        compiler_params=pltpu.CompilerParams(dimension_semantics=("arbitrary", "arbitrary"),
                                             vmem_limit_bytes=VMEM_LIMIT),
        name="hgrn_out",
    )(o_dirs, o_dirs, z, norm_g.reshape(DEPTH, 1, HG_DV))


def _rms_norm(x, g):
    xf = x.astype(_f32)
    y = xf * lax.rsqrt(jnp.mean(xf * xf, axis=-1, keepdims=True) + EPS)
    return (y * g.astype(_f32)).astype(x.dtype)


def _split_in(z):
    return jnp.split(z, np.cumsum(IN_SIZES)[:-1].tolist(), axis=-1)


def _attn_heads(q, k, v, q_g, k_g):
    b, t = q.shape[:2]
    q = _rms_norm(q.reshape(b, t, N_HEADS, HEAD_DIM), q_g)
    k = _rms_norm(k.reshape(b, t, N_KV_HEADS, HEAD_DIM), k_g)
    return q, k, v.reshape(b, t, N_KV_HEADS, HEAD_DIM)


def _rotate(x, pos):
    nf = x.shape[-1] // 2
    inv = ROPE_BASE ** (-jnp.arange(nf, dtype=_f32) / nf)
    ang = pos.astype(_f32)[:, None] * inv[None, :]
    cos = jnp.cos(ang)[None, :, None, :]
    sin = jnp.sin(ang)[None, :, None, :]
    x1, x2 = x[..., :nf], x[..., nf:]
    return jnp.concatenate([x1 * cos - x2 * sin, x1 * sin + x2 * cos], axis=-1)


def _axial_rope(x):
    t = x.shape[1]
    rows = t // GRID_W
    row = jnp.repeat(jnp.arange(rows), GRID_W)
    col = jnp.tile(jnp.arange(GRID_W), rows)
    half = HEAD_DIM // 2
    return jnp.concatenate([_rotate(x[..., :half], row), _rotate(x[..., half:], col)], axis=-1)


def _sink_column(sink, lead_shape):
    s = sink.astype(_f32).reshape(N_KV_HEADS, Q_GROUP, 1, 1)
    return jnp.broadcast_to(s, lead_shape + (1,))


def _context_attention(q, k, v, sink):
    b, l = q.shape[:2]
    scale = HEAD_DIM ** -0.5
    qb = q.reshape(b, l, N_KV_HEADS, Q_GROUP, HEAD_DIM)
    s = jnp.einsum('bqhgd,bshd->bhgqs', qb, k).astype(_f32) * scale
    s = jnp.concatenate([s, _sink_column(sink, s.shape[:-1])], axis=-1)
    pr = jax.nn.softmax(s, axis=-1)[..., :l]
    o = jnp.einsum('bhgqs,bshd->bqhgd', pr, v)
    return o.reshape(b, l, N_HEADS * HEAD_DIM)


def _window_attention(q, k, v, k_ctx, v_ctx, sink):
    b, t = q.shape[:2]
    nb = t // ATTN_BLOCK
    lc = k_ctx.shape[1]
    scale = HEAD_DIM ** -0.5
    qb = q.reshape(b, nb, ATTN_BLOCK, N_KV_HEADS, Q_GROUP, HEAD_DIM)
    pad = ((0, 0), (ATTN_BLOCK, ATTN_BLOCK), (0, 0), (0, 0))

    def bands(a):
        ap = jnp.pad(a, pad).reshape(b, nb + 2, ATTN_BLOCK, N_KV_HEADS, HEAD_DIM)
        return jnp.concatenate([ap[:, :-2], ap[:, 1:-1], ap[:, 2:]], axis=2)

    kb, vb = bands(k), bands(v)
    blk = jnp.arange(nb)[:, None] * ATTN_BLOCK
    qpos = blk + jnp.arange(ATTN_BLOCK)[None, :]
    kpos = blk - ATTN_BLOCK + jnp.arange(3 * ATTN_BLOCK)[None, :]
    rel = kpos[:, None, :] - qpos[:, :, None]
    valid = (jnp.abs(rel) <= WINDOW) & (kpos[:, None, :] >= 0) & (kpos[:, None, :] < t)
    s_loc = jnp.einsum('bnqhgd,bnshd->bnhgqs', qb, kb).astype(_f32) * scale
    s_loc = jnp.where(valid[None, :, None, None], s_loc, -jnp.inf)
    s_ctx = jnp.einsum('bnqhgd,bshd->bnhgqs', qb, k_ctx).astype(_f32) * scale
    s = jnp.concatenate([s_loc, s_ctx, _sink_column(sink, s_loc.shape[:-1])], axis=-1)
    pr = jax.nn.softmax(s, axis=-1)
    w = 3 * ATTN_BLOCK
    o = (jnp.einsum('bnhgqs,bnshd->bnqhgd', pr[..., :w], vb)
         + jnp.einsum('bnhgqs,bshd->bnqhgd', pr[..., w:w + lc], v_ctx))
    return o.reshape(b, t, N_HEADS * HEAD_DIM)


def _hgrn2_forget(f_pre, lb):
    x = f_pre.astype(_f32)
    log_f = jnp.logaddexp(jnp.log(lb), jnp.log1p(-lb) + jax.nn.log_sigmoid(x))
    return 1.0 - jnp.exp(log_f), log_f


def _gla_chunkwise(q, k, v, log_f, s0):
    b, t, h, dk = q.shape
    dv = v.shape[-1]
    nc = t // HG_CHUNK
    q, k, log_f = (a.reshape(b, nc, HG_CHUNK, h, dk) for a in (q, k, log_f))
    v = v.reshape(b, nc, HG_CHUNK, h, dv)
    cum = jnp.cumsum(log_f, axis=2)
    last = cum[:, :, -1]
    causal = jnp.tril(jnp.ones((HG_CHUNK, HG_CHUNK), dtype=bool))[None, None, :, :, None, None]
    diff = cum[:, :, :, None] - cum[:, :, None, :]
    decay = jnp.exp(jnp.where(causal, diff, -jnp.inf))
    scores = jnp.einsum('bctshd,bcthd,bcshd->bchts', decay, q, k)
    o_intra = jnp.einsum('bchts,bcshe->bcthe', scores, v)
    kv = jnp.einsum('bcshd,bcshe->bchde', k * jnp.exp(last[:, :, None] - cum), v)

    def step(state, inp):
        g_c, kv_c = inp
        return jnp.exp(g_c)[..., None] * state + kv_c, state

    s_final, s_prev = lax.scan(step, s0, (jnp.moveaxis(last, 1, 0), jnp.moveaxis(kv, 1, 0)))
    s_prev = jnp.moveaxis(s_prev, 0, 1)
    o_inter = jnp.einsum('bcthd,bchde->bcthe', q * jnp.exp(cum), s_prev)
    return (o_intra + o_inter).reshape(b, t, h, dv), s_final


def _hgrn2_bidir(hq, hff, hfb, hi, hg, lb, norm_g, s0):
    b, t = hq.shape[:2]

    def heads(a):
        return a.astype(_f32).reshape(b, t, HG_HEADS, HG_DK)

    def flip(a):
        return jnp.flip(a, axis=1)

    q = heads(jax.nn.silu(hq.astype(_f32)) * HG_DK ** -0.5)
    v = heads(hi)
    k_f, lf_f = _hgrn2_forget(hff, lb[0])
    k_b, lf_b = _hgrn2_forget(hfb, lb[1])
    s0 = s0.astype(_f32)
    o_f, s_f = _gla_chunkwise(q, heads(k_f), v, heads(lf_f), s0[:, 0])
    o_b, s_b = _gla_chunkwise(flip(q), flip(heads(k_b)), flip(v), flip(heads(lf_b)), s0[:, 1])
    o = _rms_norm(o_f + flip(o_b), norm_g) * jax.nn.silu(heads(hg))
    return o.reshape(b, t, HG_W), jnp.stack([s_f, s_b], axis=1)


def _s5_param_kernel(are_ref, aim_ref, ldt_ref, bre_ref, bim_ref, abre_ref, abim_ref, bpre_ref, bpim_ref):
    bre = bre_ref[...]
    bim = bim_ref[...]
    for d in range(2):
        lr = are_ref[d]
        li = aim_ref[d]
        dt = jnp.exp(ldt_ref[d])
        mag = jnp.exp(lr * dt)
        ar = mag * jnp.cos(li * dt)
        ai = mag * jnp.sin(li * dt)
        abre_ref[d] = ar
        abim_ref[d] = ai
        nr = ar - 1.0
        den = lr * lr + li * li
        gr = (nr * lr + ai * li) / den
        gi = (ai * lr - nr * li) / den
        gr3 = gr[:, None, :]
        gi3 = gi[:, None, :]
        bpre_ref[d] = gr3 * bre - gi3 * bim
        bpim_ref[d] = gr3 * bim + gi3 * bre


def _s5_params(a_re, a_im, log_dt, b_re, b_im, c_re, c_im):
    g, pst, ch = S5_GROUPS, S5_STATE, S5_CH
    abre, abim, bpre, bpim = pl.pallas_call(
        _s5_param_kernel,
        out_shape=[
            jax.ShapeDtypeStruct((2, g, pst), _f32),
            jax.ShapeDtypeStruct((2, g, pst), _f32),
            jax.ShapeDtypeStruct((2, g, ch, pst), _f32),
            jax.ShapeDtypeStruct((2, g, ch, pst), _f32),
        ],
        name="s5_params",
    )(a_re, a_im, log_dt.reshape(2, g, 1), b_re.transpose(0, 2, 1), b_im.transpose(0, 2, 1))
    eye = jnp.eye(S5_SLAB_GROUPS, dtype=_f32)

    def b_slabs(bp):
        bp = bp.reshape(2, S5_NSLAB, S5_SLAB_GROUPS, ch, pst)
        w = bp[:, :, :, :, None, :] * eye[None, None, :, None, :, None]
        return w.reshape(2, S5_NSLAB, S5_SLAB_CH, S5_SLAB_ST)

    def c_slabs(cm):
        cm = cm.reshape(S5_NSLAB, S5_SLAB_GROUPS, ch, pst).transpose(0, 1, 3, 2)
        w = cm[:, :, :, None, :] * eye[None, :, None, :, None]
        return w.reshape(S5_NSLAB, S5_SLAB_ST, S5_SLAB_CH)

    b_blk = jnp.concatenate([b_slabs(bpre), b_slabs(bpim)], axis=-1).astype(_bf16)
    return (abre.reshape(2, 1, g * pst), abim.reshape(2, 1, g * pst), b_blk,
            c_slabs(c_re).astype(_bf16), c_slabs(c_im).astype(_bf16))


def _s5_scan_kernel(u_ref, b_ref, cre_ref, cim_ref, are_ref, aim_ref, x0re_ref, x0im_ref,
                    y_ref, xfre_ref, xfim_ref, bu_s, zre_s, zim_s, *, tb_steps, rd):
    tb = pl.program_id(2)
    n_rg = rd // SUBLANES
    st = S5_SLAB_ST

    @pl.when(tb == 0)
    def _():
        zre_s[...] = x0re_ref[...]
        zim_s[...] = x0im_ref[...]

    u = u_ref[...].reshape(tb_steps * rd, S5_SLAB_CH).astype(_bf16)
    bu_s[...] = jnp.dot(u, b_ref[...], preferred_element_type=_f32).reshape(tb_steps, rd, 2 * st)
    are = jnp.broadcast_to(are_ref[...], (SUBLANES, st))
    aim = jnp.broadcast_to(aim_ref[...], (SUBLANES, st))

    def body(t, carry):
        new = []
        for rg in range(n_rg):
            zre, zim = carry[rg]
            rows = slice(rg * SUBLANES, (rg + 1) * SUBLANES)
            nre = are * zre - aim * zim + bu_s[t, rows, 0:st]
            nim = are * zim + aim * zre + bu_s[t, rows, st:2 * st]
            bu_s[t, rows, 0:st] = nre
            bu_s[t, rows, st:2 * st] = nim
            new.append((nre, nim))
        return tuple(new)

    init = tuple((zre_s[rg * SUBLANES:(rg + 1) * SUBLANES, :], zim_s[rg * SUBLANES:(rg + 1) * SUBLANES, :])
                 for rg in range(n_rg))
    fin = lax.fori_loop(0, tb_steps, body, init, unroll=4)
    for rg in range(n_rg):
        zre_s[rg * SUBLANES:(rg + 1) * SUBLANES, :] = fin[rg][0]
        zim_s[rg * SUBLANES:(rg + 1) * SUBLANES, :] = fin[rg][1]

    xs = bu_s[...].reshape(tb_steps * rd, 2 * st)
    y = (jnp.dot(xs[:, 0:st].astype(_bf16), cre_ref[...], preferred_element_type=_f32)
         - jnp.dot(xs[:, st:2 * st].astype(_bf16), cim_ref[...], preferred_element_type=_f32))
    y_ref[...] = y.reshape(tb_steps, rd, S5_SLAB_CH)

    @pl.when(tb == pl.num_programs(2) - 1)
    def _():
        xfre_ref[...] = zre_s[...]
        xfim_ref[...] = zim_s[...]


def _s5_scan(u, params, x0_re, x0_im):
    abre, abim, b_blk, c_re, c_im = params
    _, t_len, rd, _ = u.shape
    tb_steps = S5_BLOCK_ROWS // rd
    n_tb = t_len // tb_steps
    st = S5_SLAB_ST
    grid = (2, S5_NSLAB, n_tb)
    kern = functools.partial(_s5_scan_kernel, tb_steps=tb_steps, rd=rd)
    return pl.pallas_call(
        kern,
        grid=grid,
        in_specs=[
            pl.BlockSpec((None, tb_steps, rd, S5_SLAB_CH), lambda d, s, t: (d, t, 0, s)),
            pl.BlockSpec((None, None, S5_SLAB_CH, 2 * st), lambda d, s, t: (d, s, 0, 0)),
            pl.BlockSpec((None, st, S5_SLAB_CH), lambda d, s, t: (s, 0, 0)),
            pl.BlockSpec((None, st, S5_SLAB_CH), lambda d, s, t: (s, 0, 0)),
            pl.BlockSpec((None, 1, st), lambda d, s, t: (d, 0, s)),
            pl.BlockSpec((None, 1, st), lambda d, s, t: (d, 0, s)),
            pl.BlockSpec((None, rd, st), lambda d, s, t: (d, 0, s)),
            pl.BlockSpec((None, rd, st), lambda d, s, t: (d, 0, s)),
        ],
        out_specs=[
            pl.BlockSpec((None, tb_steps, rd, S5_SLAB_CH), lambda d, s, t: (d, t, 0, s)),
            pl.BlockSpec((None, rd, st), lambda d, s, t: (d, 0, s)),
            pl.BlockSpec((None, rd, st), lambda d, s, t: (d, 0, s)),
        ],
        out_shape=[
            jax.ShapeDtypeStruct((2, t_len, rd, S5_W), _f32),
            jax.ShapeDtypeStruct((2, rd, S5_GROUPS * S5_STATE), _f32),
            jax.ShapeDtypeStruct((2, rd, S5_GROUPS * S5_STATE), _f32),
        ],
        scratch_shapes=[
            pltpu.VMEM((tb_steps, rd, 2 * st), _f32),
            pltpu.VMEM((rd, st), _f32),
            pltpu.VMEM((rd, st), _f32),
        ],
        compiler_params=pltpu.CompilerParams(dimension_semantics=("arbitrary", "arbitrary", "arbitrary"),
                                             vmem_limit_bytes=VMEM_LIMIT),
        name="s5_scan",
    )(u, b_blk, c_re, c_im, abre, abim, x0_re, x0_im)


def _gelu_tanh(x):
    return 0.5 * x * (1.0 + jnp.tanh(math.sqrt(2.0 / math.pi) * (x + 0.044715 * (x * x * x))))


def _s5_out_kernel(yf_ref, yb_ref, u_ref, d_ref, w_ref, b_ref, o_ref, w_s):
    @pl.when(pl.program_id(0) == 0)
    def _():
        w_s[...] = w_ref[...].astype(_bf16)

    y = _gelu_tanh(yf_ref[...] + yb_ref[...] + d_ref[...] * u_ref[...])
    z = jnp.dot(y.astype(_bf16), w_s[...], preferred_element_type=_f32) + b_ref[...]
    o_ref[...] = (y * jax.nn.sigmoid(z)).astype(_bf16)


def _s5_out(yf, yb, u, d_vec, w_glu, b_glu, layer):
    nt = N_TOK // ROW_TILE
    row = lambda i: (i, 0)
    lay = lambda i: (layer, 0, 0)
    return pl.pallas_call(
        _s5_out_kernel,
        grid=(nt,),
        in_specs=[
            pl.BlockSpec((ROW_TILE, S5_W), row),
            pl.BlockSpec((ROW_TILE, S5_W), row),
            pl.BlockSpec((ROW_TILE, S5_W), row),
            pl.BlockSpec((None, 1, S5_W), lay),
            pl.BlockSpec((None, S5_W, S5_W), lay),
            pl.BlockSpec((None, 1, S5_W), lay),
        ],
        out_specs=pl.BlockSpec((ROW_TILE, S5_W), row),
        out_shape=jax.ShapeDtypeStruct((N_TOK, S5_W), _bf16),
        scratch_shapes=[pltpu.VMEM((S5_W, S5_W), _bf16)],
        compiler_params=pltpu.CompilerParams(dimension_semantics=("arbitrary",), vmem_limit_bytes=VMEM_LIMIT),
        name="s5_out",
    )(yf, yb, u, d_vec.reshape(DEPTH, 1, S5_W), w_glu, b_glu.reshape(DEPTH, 1, S5_W))


def _s5_branch(su_c, su_l, sp, s5_re0, s5_im0, layer):
    params = _s5_params(sp['s5_a_re'][layer], sp['s5_a_im'][layer], sp['s5_log_dt'][layer], sp['s5_b_re'][layer],
                        sp['s5_b_im'][layer], sp['s5_c_re'][layer], sp['s5_c_im'][layer])
    nst = S5_GROUPS * S5_STATE

    def both_dirs(u_tm):
        return jnp.stack([u_tm, u_tm[::-1]], axis=0)

    def merge_dirs(y):
        return y[0].transpose(1, 0, 2), y[1, ::-1].transpose(1, 0, 2)

    zeros = jnp.zeros((2, BATCH, nst), _f32)
    y_c, xf_re, xf_im = _s5_scan(both_dirs(su_c.transpose(1, 0, 2)), params, zeros, zeros)
    yf_c, yb_c = merge_dirs(y_c)
    pad = ((0, 0), (0, SUBLANES - DEC_BATCH), (0, 0))
    u_l = both_dirs(jnp.pad(su_l.transpose(1, 0, 2), pad))

    def lat_state(x0):
        return jnp.pad(x0.transpose(1, 0, 2, 3).reshape(2, DEC_BATCH, nst), pad)

    y_l, _, _ = _s5_scan(u_l, params, lat_state(s5_re0), lat_state(s5_im0))
    yf_l, yb_l = merge_dirs(y_l[:, :, :DEC_BATCH])
    yf = jnp.concatenate([yf_c.reshape(N_CTX, S5_W), yf_l.reshape(N_LAT, S5_W)], axis=0)
    yb = jnp.concatenate([yb_c.reshape(N_CTX, S5_W), yb_l.reshape(N_LAT, S5_W)], axis=0)
    u = jnp.concatenate([su_c.reshape(N_CTX, S5_W), su_l.reshape(N_LAT, S5_W)], axis=0)
    s_out = _s5_out(yf, yb, u, sp['s5_d'], sp['s5_w_glu'], sp['s5_b_glu'], layer)

    def ctx_state(xf):
        return xf.reshape(2, BATCH, S5_GROUPS, S5_STATE).transpose(1, 0, 2, 3)

    return s_out, ctx_state(xf_re), ctx_state(xf_im)


def _merge_branches(a_out, h_out, s_out, ga, gh, gs, p):
    y = (jax.nn.sigmoid(ga) * (a_out @ p['w_br_attn'])
         + jax.nn.sigmoid(gh) * (h_out @ p['w_br_hg'])
         + jax.nn.sigmoid(gs) * (s_out @ p['w_br_s5']))
    return y @ p['w_out']


def _legacy_mixer_sublayer(xc, xl, mods, pl_, sp, lb, ck, cv, s_hg0, s5_re0, s5_im0, layer):
    sh1, sc1, gt1 = mods[0], mods[1], mods[2]

    def pre(x, sh, sc):
        h = _rms_norm(x, pl_['norm1_g']) * (1.0 + sc) + sh
        return _split_in(h @ pl_['w_in'])

    zc = pre(xc, sh1[0], sc1[0])
    zl = pre(xl, sh1[1:], sc1[1:])
    s_out, s5_re, s5_im = _s5_branch(zc[8], zl[8], sp, s5_re0, s5_im0, layer)
    s_out_c = s_out[:N_CTX].reshape(BATCH, SEQ, S5_W)
    s_out_l = s_out[N_CTX:].reshape(DEC_BATCH, DEC_SEQ, S5_W)
    q, k, v, hq, hff, hfb, hi, hg, _, ga, gh, gs = zc
    q, k, v = _attn_heads(q, k, v, pl_['q_norm_g'], pl_['k_norm_g'])
    a_out = _context_attention(q, k, v, pl_['sink'])
    z_hg = jnp.zeros((BATCH, 2, HG_HEADS, HG_DK, HG_DV), _f32)
    h_out, s_hg = _hgrn2_bidir(hq, hff, hfb, hi, hg, lb, pl_['hg_norm_g'], z_hg)
    xc = xc + gt1[0] * _merge_branches(a_out, h_out, s_out_c, ga, gh, gs, pl_)
    new = (k, v, s_hg, s5_re, s5_im)
    q, k, v, hq, hff, hfb, hi, hg, _, ga, gh, gs = zl
    q, k, v = _attn_heads(q, k, v, pl_['q_norm_g'], pl_['k_norm_g'])
    a_out = _window_attention(_axial_rope(q), _axial_rope(k), v, ck, cv, pl_['sink'])
    h_out, _ = _hgrn2_bidir(hq, hff, hfb, hi, hg, lb, pl_['hg_norm_g'], s_hg0)
    xl = xl + gt1[1:] * _merge_branches(a_out, h_out, s_out_l, ga, gh, gs, pl_)
    return xc, xl, new


def kernel(x_prompt, x_sample, cache_k, cache_v, state_hgrn, state_s5_re, state_s5_im, c, c_ctx, w_mod, b_mod, norm1_g, norm2_g, w_in, q_norm_g, k_norm_g, attn_sink, hg_lb_logits, hg_norm_g, s5_a_re, s5_a_im, s5_log_dt, s5_b_re, s5_b_im, s5_c_re, s5_c_im, s5_d, s5_w_glu, s5_b_glu, w_br_attn, w_br_hg, w_br_s5, w_out, router_w, router_b, exp_w1, exp_b1, exp_w2, exp_b2):
    lb_all = jnp.cumsum(jax.nn.softmax(hg_lb_logits.astype(_f32), axis=0), axis=0)
    lb_all = lb_all - lb_all[:1]
    cond_rows = jnp.concatenate([c_ctx[None, :], c, jnp.zeros((SUBLANES - N_GROUPS, D_MODEL), _f32)], axis=0)
    x = jnp.concatenate([x_prompt.reshape(N_CTX, D_MODEL), x_sample.reshape(N_LAT, D_MODEL)], axis=0)
    cache_k_t = cache_k.transpose(0, 1, 3, 2, 4)
    cache_v_t = cache_v.transpose(0, 1, 3, 2, 4)
    zero_state = jnp.zeros((BATCH, 2, HG_HEADS, HG_DK, HG_DV), _f32)
    new_k, new_v, new_hg, new_re, new_im = [], [], [], [], []
    moe_p = {"norm2_g": norm2_g, "router_w": router_w, "router_b": router_b, "exp_w1": exp_w1, "exp_b1": exp_b1,
             "exp_w2": exp_w2, "exp_b2": exp_b2}
    s5_p = {'s5_a_re': s5_a_re, 's5_a_im': s5_a_im, 's5_log_dt': s5_log_dt, 's5_b_re': s5_b_re, 's5_b_im': s5_b_im,
            's5_c_re': s5_c_re, 's5_c_im': s5_c_im, 's5_d': s5_d, 's5_w_glu': s5_w_glu, 's5_b_glu': s5_b_glu}
    for l in range(DEPTH):
        m = _modulation(cond_rows, w_mod, b_mod, l)[:N_GROUPS]
        mods = [a[:, None, :] for a in jnp.split(m, N_MOD, axis=-1)]
        sh1, sc1, gt1 = mods[0], mods[1], mods[2]
        z = _in_proj(_norm_mod(x, norm1_g, sc1, sh1, l), w_in, l)
        a_ctx, k_l, v_l = _ctx_attention(z, q_norm_g, k_norm_g, attn_sink, l)
        a_lat = _win_attention(z, cache_k_t, cache_v_t, q_norm_g, k_norm_g, attn_sink, l)
        a_out = jnp.concatenate([a_ctx, a_lat], axis=0)
        o_dirs, hg_fin = _hgrn(z, lb_all[l], jnp.concatenate([zero_state, state_hgrn[:, l]], axis=0))
        h_out = _hgrn_out(o_dirs, z, hg_norm_g, l)
        su = z[:, COL_SU:COL_SU + S5_W]
        s_out, re_l, im_l = _s5_branch(su[:N_CTX].reshape(BATCH, SEQ, S5_W),
                                       su[N_CTX:].reshape(DEC_BATCH, DEC_SEQ, S5_W), s5_p,
                                       state_s5_re[:, l], state_s5_im[:, l], l)
        y = _merge(a_out, h_out, s_out, z, w_br_attn, w_br_hg, w_br_s5, l)
        x = _out_proj(y, w_out, x, gt1, l)
        new_k.append(k_l.reshape(BATCH, SEQ, N_KV_HEADS, HEAD_DIM))
        new_v.append(v_l.reshape(BATCH, SEQ, N_KV_HEADS, HEAD_DIM))
        new_hg.append(hg_fin[:BATCH])
        new_re.append(re_l)
        new_im.append(im_l)
        x = _moe_sublayer(x, mods, moe_p, l)
    return (x[:N_CTX].reshape(BATCH, SEQ, D_MODEL), x[N_CTX:].reshape(DEC_BATCH, DEC_SEQ, D_MODEL),
            jnp.stack(new_k, axis=1), jnp.stack(new_v, axis=1), jnp.stack(new_hg, axis=1),
            jnp.stack(new_re, axis=1), jnp.stack(new_im, axis=1))
```

```python
import functools
import math

import jax
import jax.numpy as jnp
import numpy as np
from jax import lax
from jax.experimental import pallas as pl
from jax.experimental.pallas import tpu as pltpu

D_MODEL = 2048
BATCH = 16
SEQ = 256
DEPTH = 2
DEC_BATCH = 2
DEC_SEQ = 2048
GRID_W = 64
N_HEADS = 8
N_KV_HEADS = 2
HEAD_DIM = 128
Q_GROUP = N_HEADS // N_KV_HEADS
WINDOW = 128
ATTN_BLOCK = 128
ROPE_BASE = 10000.0
ATT_Q_W = N_HEADS * HEAD_DIM
ATT_KV_W = N_KV_HEADS * HEAD_DIM
HG_HEADS = 8
HG_DK = 128
HG_DV = 128
HG_W = HG_HEADS * HG_DK
S5_W = 1024
S5_CH = 16
S5_GROUPS = S5_W // S5_CH
S5_STATE = 64
N_EXPERTS = 32
TOP_K = 4
D_FF = 2048
SWIGLU_ALPHA = 1.702
SWIGLU_LIMIT = 7.0
N_MOD = 6
EPS = 1e-6
IN_SIZES = (ATT_Q_W, ATT_KV_W, ATT_KV_W, HG_W, HG_W, HG_W, HG_W, HG_W, S5_W, D_MODEL, D_MODEL, D_MODEL)
IN_COLS = sum(IN_SIZES)

N_CTX = BATCH * SEQ
N_LAT = DEC_BATCH * DEC_SEQ
N_TOK = N_CTX + N_LAT
N_GROUPS = 1 + DEC_BATCH

LANES = 128
ROW_TILE = 256
VMEM_LIMIT = 56 * 1024 * 1024

MOE_SUPER = 1024
MOE_SUB = 256
MOE_FC = 512
MOE_NC = D_FF // MOE_FC
MOE_NS = (N_TOK * TOP_K) // MOE_SUPER + N_EXPERTS
MOD_TN = 1536
PROJ_TN = 1536
PROJ_TM = 512
MERGE_TN = 512
HG_OUT_W = 512
DISPATCH_TOKENS = 64
COMBINE_TILE = 64

S5_SLAB_GROUPS = 8
S5_NSLAB = S5_GROUPS // S5_SLAB_GROUPS
S5_SLAB_CH = S5_SLAB_GROUPS * S5_CH
S5_SLAB_ST = S5_SLAB_GROUPS * S5_STATE
S5_BLOCK_ROWS = 1024
SUBLANES = 8

_bf16 = jnp.bfloat16
_f32 = jnp.float32


def _group_of_tile(i):
    ctx_tiles = N_CTX // ROW_TILE
    lat_tiles = DEC_SEQ // ROW_TILE
    return jnp.where(i < ctx_tiles, 0, 1 + (i - ctx_tiles) // lat_tiles)


def _split_bf16(a):
    hi = a.astype(_bf16)
    lo = (a - hi.astype(_f32)).astype(_bf16)
    return hi, lo


def _router_kernel(x_ref, g_ref, sc_ref, sh_ref, rw_ref, rb_ref, h_ref, idx_ref, w_ref, rank_ref, cnt_ref, cnt_s):
    x = x_ref[...]
    y = x * lax.rsqrt(jnp.mean(x * x, axis=-1, keepdims=True) + EPS) * g_ref[...]
    h = y * (1.0 + sc_ref[...]) + sh_ref[...]
    h_ref[...] = h
    h_hi, h_lo = _split_bf16(h)
    r_hi, r_lo = _split_bf16(rw_ref[...])
    logits = (jnp.dot(h_hi, r_hi, preferred_element_type=_f32)
              + jnp.dot(h_hi, r_lo, preferred_element_type=_f32)
              + jnp.dot(h_lo, r_hi, preferred_element_type=_f32)) + rb_ref[...]
    lane = lax.broadcasted_iota(jnp.int32, logits.shape, 1)
    work = logits
    vals, idxs = [], []
    for _ in range(TOP_K):
        m = jnp.max(work, axis=-1, keepdims=True)
        i = jnp.min(jnp.where(work == m, lane, N_EXPERTS), axis=-1, keepdims=True)
        vals.append(m)
        idxs.append(i)
        work = jnp.where(lane == i, -jnp.inf, work)
    es = [jnp.exp(v - vals[0]) for v in vals]
    den = es[0] + es[1] + es[2] + es[3]
    @pl.when(pl.program_id(0) == 0)
    def _():
        cnt_s[...] = jnp.zeros(cnt_s.shape, _f32)

    out_lane = lax.broadcasted_iota(jnp.int32, idx_ref.shape, 1)
    tr = lax.broadcasted_iota(jnp.int32, (ROW_TILE, ROW_TILE), 0)
    tc = lax.broadcasted_iota(jnp.int32, (ROW_TILE, ROW_TILE), 1)
    earlier = jnp.where(tc < tr, 1.0, 0.0).astype(_bf16)
    running = cnt_s[...]
    idx_out = jnp.zeros(idx_ref.shape, jnp.int32)
    w_out = jnp.zeros(w_ref.shape, _f32)
    rank_out = jnp.zeros(rank_ref.shape, jnp.int32)
    for k in range(TOP_K):
        onehot = jnp.where(out_lane == idxs[k], 1.0, 0.0)
        before = jnp.dot(earlier, onehot.astype(_bf16), preferred_element_type=_f32)
        rank = jnp.sum(onehot * (running + before), axis=-1, keepdims=True)
        running = running + jnp.sum(onehot, axis=0, keepdims=True)
        idx_out = jnp.where(out_lane == k, idxs[k], idx_out)
        w_out = jnp.where(out_lane == k, es[k] / den, w_out)
        rank_out = jnp.where(out_lane == k, rank.astype(jnp.int32), rank_out)
    cnt_s[...] = running
    idx_ref[...] = idx_out
    w_ref[...] = w_out
    rank_ref[...] = rank_out
    cnt_ref[...] = running


def _router(x, norm_g, sc, sh, router_w, router_b, layer):
    nt = N_TOK // ROW_TILE
    row = lambda i: (i, 0)
    grp = lambda i: (_group_of_tile(i), 0, 0)
    lay = lambda i: (layer, 0, 0)
    h, idx, w, rank, cnt = pl.pallas_call(
        _router_kernel,
        grid=(nt,),
        in_specs=[
            pl.BlockSpec((ROW_TILE, D_MODEL), row),
            pl.BlockSpec((None, 1, D_MODEL), lay),
            pl.BlockSpec((None, 1, D_MODEL), grp),
            pl.BlockSpec((None, 1, D_MODEL), grp),
            pl.BlockSpec((None, D_MODEL, N_EXPERTS), lay),
            pl.BlockSpec((None, 1, N_EXPERTS), lay),
        ],
        out_specs=[
            pl.BlockSpec((ROW_TILE, D_MODEL), row),
            pl.BlockSpec((ROW_TILE, LANES), row),
            pl.BlockSpec((ROW_TILE, LANES), row),
            pl.BlockSpec((ROW_TILE, LANES), row),
            pl.BlockSpec((1, LANES), lambda i: (0, 0)),
        ],
        out_shape=[
            jax.ShapeDtypeStruct((N_TOK, D_MODEL), _f32),
            jax.ShapeDtypeStruct((N_TOK, LANES), jnp.int32),
            jax.ShapeDtypeStruct((N_TOK, LANES), _f32),
            jax.ShapeDtypeStruct((N_TOK, LANES), jnp.int32),
            jax.ShapeDtypeStruct((1, LANES), _f32),
        ],
        scratch_shapes=[pltpu.VMEM((1, LANES), _f32)],
        compiler_params=pltpu.CompilerParams(dimension_semantics=("arbitrary",), vmem_limit_bytes=VMEM_LIMIT),
        name="router",
    )(x, norm_g.reshape(DEPTH, 1, D_MODEL), sc, sh, router_w, router_b.reshape(DEPTH, 1, N_EXPERTS))
    return h, idx[:, :TOP_K], w, rank[:, :TOP_K], cnt[0, :N_EXPERTS].astype(jnp.int32)


def _moe_kernel(exp_ref, rows_ref, used_ref, x_hbm, w1g_ref, w1u_ref, b1g_ref, b1u_ref, w2_ref, b2_ref,
                o_hbm, w1g_s, w1u_s, w2_s, x_s, acc_s, sem):
    s = pl.program_id(0)
    c = pl.program_id(1)
    nvalid = rows_ref[s]
    tile_rows = pl.ds(pl.multiple_of(s * MOE_SUPER, MOE_SUPER), MOE_SUPER)

    def write_out():
        out_copy = pltpu.make_async_copy(acc_s, o_hbm.at[tile_rows], sem)
        out_copy.start()
        out_copy.wait()

    @pl.when(nvalid > 0)
    def _():
        @pl.when(c == 0)
        def _():
            in_copy = pltpu.make_async_copy(x_hbm.at[tile_rows], acc_s, sem)
            in_copy.start()
            in_copy.wait()
            x_s[...] = acc_s[...].astype(_bf16)
            acc_s[...] = jnp.zeros(acc_s.shape, _f32)

        w1g_s[...] = w1g_ref[...].astype(_bf16)
        w1u_s[...] = w1u_ref[...].astype(_bf16)
        w2_s[...] = w2_ref[...].astype(_bf16)
        for r in range(MOE_SUPER // MOE_SUB):
            @pl.when(r * MOE_SUB < nvalid)
            def _():
                rows = pl.ds(r * MOE_SUB, MOE_SUB)
                xs = x_s[rows, :]
                zg = jnp.dot(xs, w1g_s[...], preferred_element_type=_f32) + b1g_ref[...]
                zu = jnp.dot(xs, w1u_s[...], preferred_element_type=_f32) + b1u_ref[...]
                g = jnp.minimum(zg, SWIGLU_LIMIT)
                u = jnp.clip(zu, -SWIGLU_LIMIT, SWIGLU_LIMIT)
                act = g * jax.nn.sigmoid(SWIGLU_ALPHA * g) * (u + 1.0)
                acc_s[rows, :] += jnp.dot(act.astype(_bf16), w2_s[...], preferred_element_type=_f32)

        @pl.when(c == MOE_NC - 1)
        def _():
            acc_s[...] = acc_s[...] + b2_ref[...]
            write_out()

    @pl.when((nvalid == 0) & (c == 0))
    def _():
        acc_s[...] = jnp.zeros(acc_s.shape, _f32)
        write_out()


def _moe_ffn(x_sorted, st_expert, st_rows, n_used, w1, b1, w2, b2, layer):
    def chunk(s, c, u_ref):
        return jnp.where(s < u_ref[0], c, MOE_NC - 1)

    def w1g_map(s, c, e_ref, r_ref, u_ref):
        return (layer, e_ref[s], 0, chunk(s, c, u_ref))

    def w1u_map(s, c, e_ref, r_ref, u_ref):
        return (layer, e_ref[s], 0, MOE_NC + chunk(s, c, u_ref))

    def w2_map(s, c, e_ref, r_ref, u_ref):
        return (layer, e_ref[s], chunk(s, c, u_ref), 0)

    def b2_map(s, c, e_ref, r_ref, u_ref):
        return (layer, e_ref[s], 0, 0)

    grid_spec = pltpu.PrefetchScalarGridSpec(
        num_scalar_prefetch=3,
        grid=(MOE_NS, MOE_NC),
        in_specs=[
            pl.BlockSpec(memory_space=pl.ANY),
            pl.BlockSpec((None, None, D_MODEL, MOE_FC), w1g_map),
            pl.BlockSpec((None, None, D_MODEL, MOE_FC), w1u_map),
            pl.BlockSpec((None, None, 1, MOE_FC), w1g_map),
            pl.BlockSpec((None, None, 1, MOE_FC), w1u_map),
            pl.BlockSpec((None, None, MOE_FC, D_MODEL), w2_map),
            pl.BlockSpec((None, None, 1, D_MODEL), b2_map),
        ],
        out_specs=pl.BlockSpec(memory_space=pl.ANY),
        scratch_shapes=[
            pltpu.VMEM((D_MODEL, MOE_FC), _bf16),
            pltpu.VMEM((D_MODEL, MOE_FC), _bf16),
            pltpu.VMEM((MOE_FC, D_MODEL), _bf16),
            pltpu.VMEM((MOE_SUPER, D_MODEL), _bf16),
            pltpu.VMEM((MOE_SUPER, D_MODEL), _f32),
            pltpu.SemaphoreType.DMA(()),
        ],
    )
    return pl.pallas_call(
        _moe_kernel,
        grid_spec=grid_spec,
        out_shape=jax.ShapeDtypeStruct((MOE_NS * MOE_SUPER, D_MODEL), _f32),
        compiler_params=pltpu.CompilerParams(dimension_semantics=("arbitrary", "arbitrary"),
                                             vmem_limit_bytes=VMEM_LIMIT),
        name="moe_ffn",
    )(st_expert, st_rows, n_used, x_sorted, w1, w1,
      b1.reshape(DEPTH, N_EXPERTS, 1, 2 * D_FF), b1.reshape(DEPTH, N_EXPERTS, 1, 2 * D_FF),
      w2, b2.reshape(DEPTH, N_EXPERTS, 1, D_MODEL))


def _issue_row_copies(idx_ref, src_hbm, dst_ref, sem, n):
    def start(r, carry):
        pltpu.make_async_copy(src_hbm.at[idx_ref[0, r]], dst_ref.at[r], sem).start()
        return carry

    lax.fori_loop(0, n, start, 0, unroll=8)
    pltpu.make_async_copy(src_hbm.at[pl.ds(0, n)], dst_ref, sem).wait()


def _route_tables(top_idx, rank, counts):
    n_st = (counts + MOE_SUPER - 1) // MOE_SUPER
    st_end = jnp.cumsum(n_st)
    st_start = st_end - n_st
    n_used = st_end[-1]
    s_ids = jnp.arange(MOE_NS, dtype=jnp.int32)
    s_clamped = jnp.minimum(s_ids, n_used - 1)
    st_expert = jnp.sum((st_end[None, :] <= s_clamped[:, None]).astype(jnp.int32), axis=1)
    is_exp = st_expert[:, None] == jnp.arange(N_EXPERTS, dtype=jnp.int32)[None, :]
    tile_in_expert = s_clamped - jnp.sum(jnp.where(is_exp, st_start[None, :], 0), axis=1)
    st_rows = jnp.clip(jnp.sum(jnp.where(is_exp, counts[None, :], 0), axis=1) - tile_in_expert * MOE_SUPER,
                       0, MOE_SUPER)
    st_rows = jnp.where(s_ids < n_used, st_rows, 0).astype(jnp.int32)
    first_row = jnp.sum(jnp.where(top_idx[:, :, None] == jnp.arange(N_EXPERTS, dtype=jnp.int32),
                                  st_start * MOE_SUPER, 0), axis=-1)
    return st_expert, st_rows, n_used.reshape(1).astype(jnp.int32), first_row + rank


def _dispatch_kernel(pos_ref, h_ref, zero_hbm, o_hbm, sem):
    n = TOP_K * DISPATCH_TOKENS

    def start(j, carry):
        pltpu.make_async_copy(h_ref.at[j // TOP_K], o_hbm.at[pos_ref[0, j]], sem).start()
        return carry

    lax.fori_loop(0, n, start, 0, unroll=8)
    pltpu.make_async_copy(o_hbm.at[pl.ds(0, n)], o_hbm.at[pl.ds(0, n)], sem).wait()


def _dispatch(h_words, pos):
    nt = N_TOK // DISPATCH_TOKENS
    width = h_words.shape[1]
    n_rows = MOE_NS * MOE_SUPER
    return pl.pallas_call(
        _dispatch_kernel,
        grid=(nt,),
        in_specs=[
            pl.BlockSpec((None, 1, TOP_K * DISPATCH_TOKENS), lambda i: (i, 0, 0), memory_space=pltpu.SMEM),
            pl.BlockSpec((DISPATCH_TOKENS, width), lambda i: (i, 0)),
            pl.BlockSpec(memory_space=pl.ANY),
        ],
        out_specs=pl.BlockSpec(memory_space=pl.ANY),
        out_shape=jax.ShapeDtypeStruct((n_rows, width), h_words.dtype),
        scratch_shapes=[pltpu.SemaphoreType.DMA(())],
        input_output_aliases={2: 0},
        compiler_params=pltpu.CompilerParams(dimension_semantics=("arbitrary",), vmem_limit_bytes=VMEM_LIMIT),
        name="moe_dispatch",
    )(pos.reshape(nt, 1, TOP_K * DISPATCH_TOKENS), h_words, jnp.zeros((n_rows, width), h_words.dtype))


def _combine_kernel(pos_ref, y_hbm, x_ref, gt_ref, w_ref, o_ref, buf, sem):
    _issue_row_copies(pos_ref, y_hbm, buf, sem, TOP_K * COMBINE_TILE)
    y = w_ref[:, 0:1] * buf[0:COMBINE_TILE, :]
    for k in range(1, TOP_K):
        y = y + w_ref[:, k:k + 1] * buf[k * COMBINE_TILE:(k + 1) * COMBINE_TILE, :]
    o_ref[...] = x_ref[...] + gt_ref[...] * y


def _combine(y_sorted, pos, top_w, x, gt):
    nt = N_TOK // COMBINE_TILE
    tiles_per_row_tile = ROW_TILE // COMBINE_TILE
    pos_tiles = pos.reshape(nt, COMBINE_TILE, TOP_K).transpose(0, 2, 1).reshape(nt, 1, TOP_K * COMBINE_TILE)
    return pl.pallas_call(
        _combine_kernel,
        grid=(nt,),
        in_specs=[
            pl.BlockSpec((None, 1, TOP_K * COMBINE_TILE), lambda i: (i, 0, 0), memory_space=pltpu.SMEM),
            pl.BlockSpec(memory_space=pl.ANY),
            pl.BlockSpec((COMBINE_TILE, D_MODEL), lambda i: (i, 0)),
            pl.BlockSpec((None, 1, D_MODEL), lambda i: (_group_of_tile(i // tiles_per_row_tile), 0, 0)),
            pl.BlockSpec((COMBINE_TILE, LANES), lambda i: (i, 0)),
        ],
        out_specs=pl.BlockSpec((COMBINE_TILE, D_MODEL), lambda i: (i, 0)),
        out_shape=jax.ShapeDtypeStruct((N_TOK, D_MODEL), _f32),
        scratch_shapes=[pltpu.VMEM((TOP_K * COMBINE_TILE, D_MODEL), _f32), pltpu.SemaphoreType.DMA(())],
        compiler_params=pltpu.CompilerParams(dimension_semantics=("arbitrary",), vmem_limit_bytes=VMEM_LIMIT),
        name="moe_combine",
    )(pos_tiles, y_sorted, x, gt, top_w)


def _moe_sublayer(x, mods, p, layer):
    sh2, sc2, gt2 = mods[3], mods[4], mods[5]
    h, top_idx, top_w, rank, counts = _router(x, p["norm2_g"], sc2, sh2, p["router_w"], p["router_b"], layer)
    st_expert, st_rows, n_used, pos = _route_tables(top_idx, rank, counts)
    x_sorted = _dispatch(h, pos)
    y_sorted = _moe_ffn(x_sorted, st_expert, st_rows, n_used, p["exp_w1"], p["exp_b1"], p["exp_w2"], p["exp_b2"],
                        layer)
    return _combine(y_sorted, pos, top_w, x, gt2)


def _mod_kernel(c_ref, w_ref, b_ref, o_ref):
    c = c_ref[...]
    a = (c * jax.nn.sigmoid(c)).astype(_bf16)
    o_ref[...] = jnp.dot(a, w_ref[...].astype(_bf16), preferred_element_type=_f32) + b_ref[...]


def _modulation(cond_rows, w_mod, b_mod, layer):
    n = N_MOD * D_MODEL
    return pl.pallas_call(
        _mod_kernel,
        grid=(n // MOD_TN,),
        in_specs=[
            pl.BlockSpec((SUBLANES, D_MODEL), lambda j: (0, 0)),
            pl.BlockSpec((None, D_MODEL, MOD_TN), lambda j: (layer, 0, j)),
            pl.BlockSpec((None, 1, MOD_TN), lambda j: (layer, 0, j)),
        ],
        out_specs=pl.BlockSpec((SUBLANES, MOD_TN), lambda j: (0, j)),
        out_shape=jax.ShapeDtypeStruct((SUBLANES, n), _f32),
        compiler_params=pltpu.CompilerParams(dimension_semantics=("arbitrary",), vmem_limit_bytes=VMEM_LIMIT),
        name="modulation",
    )(cond_rows, w_mod, b_mod.reshape(DEPTH, 1, n))


def _norm_mod_kernel(x_ref, g_ref, sc_ref, sh_ref, h_ref):
    x = x_ref[...]
    y = x * lax.rsqrt(jnp.mean(x * x, axis=-1, keepdims=True) + EPS) * g_ref[...]
    h_ref[...] = (y * (1.0 + sc_ref[...]) + sh_ref[...]).astype(_bf16)


def _norm_mod(x, norm_g, sc, sh, layer):
    row = lambda i: (i, 0)
    grp = lambda i: (_group_of_tile(i), 0, 0)
    lay = lambda i: (layer, 0, 0)
    return pl.pallas_call(
        _norm_mod_kernel,
        grid=(N_TOK // ROW_TILE,),
        in_specs=[
            pl.BlockSpec((ROW_TILE, D_MODEL), row),
            pl.BlockSpec((None, 1, D_MODEL), lay),
            pl.BlockSpec((None, 1, D_MODEL), grp),
            pl.BlockSpec((None, 1, D_MODEL), grp),
        ],
        out_specs=pl.BlockSpec((ROW_TILE, D_MODEL), row),
        out_shape=jax.ShapeDtypeStruct((N_TOK, D_MODEL), _bf16),
        compiler_params=pltpu.CompilerParams(dimension_semantics=("arbitrary",), vmem_limit_bytes=VMEM_LIMIT),
        name="norm_mod",
    )(x, norm_g.reshape(DEPTH, 1, D_MODEL), sc, sh)


def _in_proj_kernel(h_ref, w_ref, o_ref, w_s):
    @pl.when(pl.program_id(1) == 0)
    def _():
        w_s[...] = w_ref[...].astype(_bf16)

    o_ref[...] = jnp.dot(h_ref[...], w_s[...], preferred_element_type=_f32)


def _in_proj(h, w_in, layer):
    return pl.pallas_call(
        _in_proj_kernel,
        grid=(IN_COLS // PROJ_TN, N_TOK // PROJ_TM),
        in_specs=[
            pl.BlockSpec((PROJ_TM, D_MODEL), lambda j, i: (i, 0)),
            pl.BlockSpec((None, D_MODEL, PROJ_TN), lambda j, i: (layer, 0, j)),
        ],
        out_specs=pl.BlockSpec((PROJ_TM, PROJ_TN), lambda j, i: (i, j)),
        out_shape=jax.ShapeDtypeStruct((N_TOK, IN_COLS), _f32),
        scratch_shapes=[pltpu.VMEM((D_MODEL, PROJ_TN), _bf16)],
        compiler_params=pltpu.CompilerParams(dimension_semantics=("arbitrary", "arbitrary"),
                                             vmem_limit_bytes=VMEM_LIMIT),
        name="in_proj",
    )(h, w_in)


def _merge_kernel(a_ref, h_ref, s_ref, ga_ref, gh_ref, gs_ref, wa_ref, wh_ref, ws_ref, o_ref, wa_s, wh_s, ws_s):
    @pl.when(pl.program_id(1) == 0)
    def _():
        wa_s[...] = wa_ref[...].astype(_bf16)
        wh_s[...] = wh_ref[...].astype(_bf16)
        ws_s[...] = ws_ref[...].astype(_bf16)

    y = (jax.nn.sigmoid(ga_ref[...]) * jnp.dot(a_ref[...], wa_s[...], preferred_element_type=_f32)
         + jax.nn.sigmoid(gh_ref[...]) * jnp.dot(h_ref[...], wh_s[...], preferred_element_type=_f32)
         + jax.nn.sigmoid(gs_ref[...]) * jnp.dot(s_ref[...], ws_s[...], preferred_element_type=_f32))
    o_ref[...] = y.astype(_bf16)


def _merge(a_out, h_out, s_out, z, w_a, w_h, w_s, layer):
    ga0, gh0, gs0 = (sum(IN_SIZES[:k]) // MERGE_TN for k in (9, 10, 11))
    act = lambda j, i: (i, 0)
    wmap = lambda j, i: (layer, 0, j)
    return pl.pallas_call(
        _merge_kernel,
        grid=(D_MODEL // MERGE_TN, N_TOK // ROW_TILE),
        in_specs=[
            pl.BlockSpec((ROW_TILE, ATT_Q_W), act),
            pl.BlockSpec((ROW_TILE, HG_W), act),
            pl.BlockSpec((ROW_TILE, S5_W), act),
            pl.BlockSpec((ROW_TILE, MERGE_TN), lambda j, i: (i, ga0 + j)),
            pl.BlockSpec((ROW_TILE, MERGE_TN), lambda j, i: (i, gh0 + j)),
            pl.BlockSpec((ROW_TILE, MERGE_TN), lambda j, i: (i, gs0 + j)),
            pl.BlockSpec((None, ATT_Q_W, MERGE_TN), wmap),
            pl.BlockSpec((None, HG_W, MERGE_TN), wmap),
            pl.BlockSpec((None, S5_W, MERGE_TN), wmap),
        ],
        out_specs=pl.BlockSpec((ROW_TILE, MERGE_TN), lambda j, i: (i, j)),
        out_shape=jax.ShapeDtypeStruct((N_TOK, D_MODEL), _bf16),
        scratch_shapes=[pltpu.VMEM((ATT_Q_W, MERGE_TN), _bf16), pltpu.VMEM((HG_W, MERGE_TN), _bf16),
                        pltpu.VMEM((S5_W, MERGE_TN), _bf16)],
        compiler_params=pltpu.CompilerParams(dimension_semantics=("arbitrary", "arbitrary"),
                                             vmem_limit_bytes=VMEM_LIMIT),
        name="merge",
    )(a_out, h_out, s_out, z, z, z, w_a, w_h, w_s)


def _out_proj_kernel(y_ref, w_ref, x_ref, gt_ref, o_ref, w_s):
    @pl.when(pl.program_id(1) == 0)
    def _():
        w_s[...] = w_ref[...].astype(_bf16)

    o_ref[...] = x_ref[...] + gt_ref[...] * jnp.dot(y_ref[...], w_s[...], preferred_element_type=_f32)


def _out_proj(y, w_out, x, gt, layer):
    return pl.pallas_call(
        _out_proj_kernel,
        grid=(D_MODEL // MERGE_TN, N_TOK // ROW_TILE),
        in_specs=[
            pl.BlockSpec((ROW_TILE, D_MODEL), lambda j, i: (i, 0)),
            pl.BlockSpec((None, D_MODEL, MERGE_TN), lambda j, i: (layer, 0, j)),
            pl.BlockSpec((ROW_TILE, MERGE_TN), lambda j, i: (i, j)),
            pl.BlockSpec((None, 1, MERGE_TN), lambda j, i: (_group_of_tile(i), 0, j)),
        ],
        out_specs=pl.BlockSpec((ROW_TILE, MERGE_TN), lambda j, i: (i, j)),
        out_shape=jax.ShapeDtypeStruct((N_TOK, D_MODEL), _f32),
        scratch_shapes=[pltpu.VMEM((D_MODEL, MERGE_TN), _bf16)],
        compiler_params=pltpu.CompilerParams(dimension_semantics=("arbitrary", "arbitrary"),
                                             vmem_limit_bytes=VMEM_LIMIT),
        name="out_proj",
    )(y, w_out, x, gt)


COL_Q, COL_K, COL_V, COL_HQ, COL_HFF, COL_HFB, COL_HI, COL_HG, COL_SU = (sum(IN_SIZES[:k]) for k in range(9))
KV_GROUP_W = Q_GROUP * HEAD_DIM


def _head_norm(x, g):
    return x * lax.rsqrt(jnp.mean(x * x, axis=-1, keepdims=True) + EPS) * g


def _dot_nt(a, b):
    return lax.dot_general(a, b, (((1,), (1,)), ((), ())), preferred_element_type=_f32)


def _softmax_with_sink(s, sink):
    m = jnp.maximum(jnp.max(s, axis=-1, keepdims=True), sink)
    p = jnp.exp(s - m)
    den = jnp.sum(p, axis=-1, keepdims=True) + jnp.exp(sink - m)
    return p / den


def _ctx_attn_kernel(q_ref, k_ref, v_ref, qg_ref, kg_ref, sink_ref, o_ref, ko_ref, vo_ref):
    kn = _head_norm(k_ref[...], kg_ref[...])
    v = v_ref[...]
    ko_ref[...] = kn
    vo_ref[...] = v
    knb = kn.astype(_bf16)
    vb = v.astype(_bf16)
    for g in range(Q_GROUP):
        cols = slice(g * HEAD_DIM, (g + 1) * HEAD_DIM)
        qn = _head_norm(q_ref[:, cols], qg_ref[...])
        s = _dot_nt(qn.astype(_bf16), knb) * HEAD_DIM ** -0.5
        p = _softmax_with_sink(s, sink_ref[0:1, g:g + 1])
        o_ref[:, cols] = jnp.dot(p.astype(_bf16), vb, preferred_element_type=_f32).astype(_bf16)


def _ctx_attention(z, q_g, k_g, sink, layer):
    lay = lambda b, j: (layer, 0, 0)
    return pl.pallas_call(
        _ctx_attn_kernel,
        grid=(BATCH, N_KV_HEADS),
        in_specs=[
            pl.BlockSpec((SEQ, KV_GROUP_W), lambda b, j: (b, COL_Q // KV_GROUP_W + j)),
            pl.BlockSpec((SEQ, HEAD_DIM), lambda b, j: (b, COL_K // HEAD_DIM + j)),
            pl.BlockSpec((SEQ, HEAD_DIM), lambda b, j: (b, COL_V // HEAD_DIM + j)),
            pl.BlockSpec((None, 1, HEAD_DIM), lay),
            pl.BlockSpec((None, 1, HEAD_DIM), lay),
            pl.BlockSpec((None, None, 1, Q_GROUP), lambda b, j: (layer, j, 0, 0)),
        ],
        out_specs=[
            pl.BlockSpec((SEQ, KV_GROUP_W), lambda b, j: (b, j)),
            pl.BlockSpec((SEQ, HEAD_DIM), lambda b, j: (b, j)),
            pl.BlockSpec((SEQ, HEAD_DIM), lambda b, j: (b, j)),
        ],
        out_shape=[
            jax.ShapeDtypeStruct((N_CTX, ATT_Q_W), _bf16),
            jax.ShapeDtypeStruct((N_CTX, ATT_KV_W), _f32),
            jax.ShapeDtypeStruct((N_CTX, ATT_KV_W), _f32),
        ],
        compiler_params=pltpu.CompilerParams(dimension_semantics=("arbitrary", "arbitrary"),
                                             vmem_limit_bytes=VMEM_LIMIT),
        name="ctx_attention",
    )(z, z, z, q_g.reshape(DEPTH, 1, HEAD_DIM), k_g.reshape(DEPTH, 1, HEAD_DIM),
      sink.reshape(DEPTH, N_KV_HEADS, 1, Q_GROUP))


def _rope_tables():
    half = HEAD_DIM // 2
    nf = half // 2
    t = np.arange(DEC_SEQ)
    pos = np.stack([t // GRID_W, t % GRID_W], axis=1).astype(np.float64)
    inv = ROPE_BASE ** (-np.arange(nf, dtype=np.float64) / nf)
    dim = np.arange(HEAD_DIM)
    ang = pos[:, dim // half] * inv[dim % nf][None, :]
    sign = np.where((dim % half) < nf, -1.0, 1.0)
    return jnp.asarray(np.cos(ang), _f32), jnp.asarray(np.sin(ang) * sign, _f32)


def _rope(x, cos, sin_signed):
    nf = HEAD_DIM // 4
    lane = lax.broadcasted_iota(jnp.int32, x.shape, 1)
    partner = jnp.where((lane % (2 * nf)) < nf, pltpu.roll(x, HEAD_DIM - nf, 1), pltpu.roll(x, nf, 1))
    return x * cos + partner * sin_signed


def _win_attn_kernel(q_ref, kp_ref, kc_ref, kn_ref, vp_ref, vc_ref, vn_ref, cq_ref, sq_ref, cp_ref, sp_ref,
                     cn_ref, sn_ref, ck_ref, cv_ref, qg_ref, kg_ref, sink_ref, o_ref):
    n = pl.program_id(1)
    nb = pl.num_programs(1)
    kg = kg_ref[...]
    k_all = jnp.concatenate([
        _rope(_head_norm(kp_ref[...], kg), cp_ref[...], sp_ref[...]),
        _rope(_head_norm(kc_ref[...], kg), cq_ref[...], sq_ref[...]),
        _rope(_head_norm(kn_ref[...], kg), cn_ref[...], sn_ref[...]),
        ck_ref[...]], axis=0).astype(_bf16)
    v_all = jnp.concatenate([vp_ref[...], vc_ref[...], vn_ref[...], cv_ref[...]], axis=0).astype(_bf16)
    r = lax.broadcasted_iota(jnp.int32, (ATTN_BLOCK, ATTN_BLOCK), 0)
    c = lax.broadcasted_iota(jnp.int32, (ATTN_BLOCK, ATTN_BLOCK), 1)
    neg = jnp.full((ATTN_BLOCK, ATTN_BLOCK), -jnp.inf, _f32)
    zero = jnp.zeros((ATTN_BLOCK, ATTN_BLOCK), _f32)
    bias = jnp.concatenate([
        jnp.where(n > 0, jnp.where(c >= r, zero, neg), neg),
        zero,
        jnp.where(n < nb - 1, jnp.where(c <= r, zero, neg), neg),
        jnp.zeros((ATTN_BLOCK, ck_ref.shape[0]), _f32)], axis=1)
    for g in range(Q_GROUP):
        cols = slice(g * HEAD_DIM, (g + 1) * HEAD_DIM)
        qn = _rope(_head_norm(q_ref[:, cols], qg_ref[...]), cq_ref[...], sq_ref[...])
        s = _dot_nt(qn.astype(_bf16), k_all) * HEAD_DIM ** -0.5 + bias
        p = _softmax_with_sink(s, sink_ref[0:1, g:g + 1])
        o_ref[:, cols] = jnp.dot(p.astype(_bf16), v_all, preferred_element_type=_f32).astype(_bf16)


def _win_attention(z, cache_k, cache_v, q_g, k_g, sink, layer):
    cos, sin = _rope_tables()
    nb = DEC_SEQ // ATTN_BLOCK
    base = N_CTX // ATTN_BLOCK
    prev = lambda n: jnp.maximum(n - 1, 0)
    nxt = lambda n: jnp.minimum(n + 1, nb - 1)
    past = cache_k.shape[3]

    def zrow(sel, col0):
        return pl.BlockSpec((ATTN_BLOCK, HEAD_DIM), lambda b, n, j: (base + b * nb + sel(n), col0 // HEAD_DIM + j))

    def tab(sel):
        return pl.BlockSpec((ATTN_BLOCK, HEAD_DIM), lambda b, n, j: (sel(n), 0))

    same = lambda n: n
    lay = lambda b, n, j: (layer, 0, 0)
    cache = pl.BlockSpec((None, None, None, past, HEAD_DIM), lambda b, n, j: (b, layer, j, 0, 0))
    return pl.pallas_call(
        _win_attn_kernel,
        grid=(DEC_BATCH, nb, N_KV_HEADS),
        in_specs=[
            pl.BlockSpec((ATTN_BLOCK, KV_GROUP_W), lambda b, n, j: (base + b * nb + n, COL_Q // KV_GROUP_W + j)),
            zrow(prev, COL_K), zrow(same, COL_K), zrow(nxt, COL_K),
            zrow(prev, COL_V), zrow(same, COL_V), zrow(nxt, COL_V),
            tab(same), tab(same), tab(prev), tab(prev), tab(nxt), tab(nxt),
            cache, cache,
            pl.BlockSpec((None, 1, HEAD_DIM), lay),
            pl.BlockSpec((None, 1, HEAD_DIM), lay),
            pl.BlockSpec((None, None, 1, Q_GROUP), lambda b, n, j: (layer, j, 0, 0)),
        ],
        out_specs=pl.BlockSpec((ATTN_BLOCK, KV_GROUP_W), lambda b, n, j: (b * nb + n, j)),
        out_shape=jax.ShapeDtypeStruct((N_LAT, ATT_Q_W), _bf16),
        compiler_params=pltpu.CompilerParams(dimension_semantics=("arbitrary", "arbitrary", "arbitrary"),
                                             vmem_limit_bytes=VMEM_LIMIT),
        name="win_attention",
    )(z, z, z, z, z, z, z, cos, sin, cos, sin, cos, sin, cache_k, cache_v,
      q_g.reshape(DEPTH, 1, HEAD_DIM), k_g.reshape(DEPTH, 1, HEAD_DIM), sink.reshape(DEPTH, N_KV_HEADS, 1, Q_GROUP))


HG_BLOCK = 128
HG_LEVELS = 7
HG_NSEG = HG_LEVELS + 2
HG_HEADS_PER_STEP = 4


def _hgrn_constants():
    c = HG_BLOCK
    i = np.arange(c)
    seg = np.zeros((HG_NSEG, c, c), np.float32)
    mask = np.zeros((HG_LEVELS + 1, c, c), np.float32)
    mask[0] = np.eye(c)
    for lv in range(HG_LEVELS):
        b = 2 << lv
        h = b // 2
        mid = (i // b) * b + h
        second = (i % b) >= h
        for r in range(c):
            if second[r]:
                seg[lv, r, mid[r]:r + 1] = 1.0
            else:
                seg[lv, r, r + 1:mid[r]] = 1.0
        same = (i[:, None] // b) == (i[None, :] // b)
        mask[lv + 1] = same & second[:, None] & ~second[None, :]
    seg[HG_LEVELS] = np.tril(np.ones((c, c)))
    seg[HG_LEVELS + 1] = np.triu(np.ones((c, c)), 1)
    seg2 = np.stack([seg, seg[:, ::-1, ::-1]]).reshape(2, HG_NSEG * c, c)
    mask2 = np.stack([mask, mask[:, ::-1, ::-1]])
    return jnp.asarray(seg2, _bf16), jnp.asarray(mask2, _f32)


def _hgrn_schedule():
    rows, first, last, seq = [[], []], [], [], []
    for s in range(BATCH + DEC_BATCH):
        n = (SEQ if s < BATCH else DEC_SEQ) // HG_BLOCK
        base = s * (SEQ // HG_BLOCK) if s < BATCH else N_CTX // HG_BLOCK + (s - BATCH) * (DEC_SEQ // HG_BLOCK)
        for cpos in range(n):
            rows[0].append(base + cpos)
            rows[1].append(base + n - 1 - cpos)
            first.append(int(cpos == 0))
            last.append(int(cpos == n - 1))
            seq.append(s)
    as_i32 = lambda a: jnp.asarray(np.asarray(a, np.int32).reshape(-1))
    return as_i32(rows), as_i32(first), as_i32(last), as_i32(seq), len(seq)


def _hgrn_kernel(rows_ref, first_ref, last_ref, seq_ref, q_ref, f_ref, i_ref, lb_ref, seg_ref, mask_ref, s0_ref,
                 o_ref, sf_ref, st_s):
    job = pl.program_id(2)
    seg = seg_ref[...]

    @pl.when(first_ref[job] == 1)
    def _():
        for hh in range(HG_HEADS_PER_STEP):
            st_s[hh] = s0_ref[hh].T

    for hh in range(HG_HEADS_PER_STEP):
        cols = slice(hh * HG_DK, (hh + 1) * HG_DK)
        xq = q_ref[:, cols]
        q = xq * jax.nn.sigmoid(xq) * HG_DK ** -0.5
        xf = f_ref[:, cols]
        log_sig = jnp.minimum(xf, 0.0) - jnp.log(1.0 + jnp.exp(-jnp.abs(xf)))
        lb = lb_ref[hh]
        f = lb + (1.0 - lb) * jnp.exp(log_sig)
        log_f = jnp.where(lb > 0.0, jnp.log(f), log_sig)
        k = 1.0 - f
        v = i_ref[:, cols]
        vb = v.astype(_bf16)
        lf_hi, lf_lo = _split_bf16(log_f)
        decay = jnp.exp(jnp.dot(seg, lf_hi, preferred_element_type=_f32)
                        + jnp.dot(seg, lf_lo, preferred_element_type=_f32))
        scores = _dot_nt(q.astype(_bf16), k.astype(_bf16)) * mask_ref[0]
        for lv in range(HG_LEVELS):
            e = decay[lv * HG_BLOCK:(lv + 1) * HG_BLOCK]
            scores = scores + _dot_nt((q * e).astype(_bf16), (k * e).astype(_bf16)) * mask_ref[lv + 1]
        e_q = decay[HG_LEVELS * HG_BLOCK:(HG_LEVELS + 1) * HG_BLOCK]
        e_k = decay[(HG_LEVELS + 1) * HG_BLOCK:(HG_LEVELS + 2) * HG_BLOCK]
        st = st_s[hh]
        o_ref[:, cols] = (jnp.dot(scores.astype(_bf16), vb, preferred_element_type=_f32)
                          + _dot_nt((q * e_q).astype(_bf16), st.astype(_bf16)))
        total = jnp.sum(log_f, axis=0, keepdims=True)
        st_new = jnp.exp(total) * st + jnp.dot(v.T.astype(_bf16), (k * e_k).astype(_bf16),
                                               preferred_element_type=_f32)
        st_s[hh] = st_new

    @pl.when(last_ref[job] == 1)
    def _():
        for hh in range(HG_HEADS_PER_STEP):
            sf_ref[hh] = st_s[hh].T


def _hgrn(z, lb, s0):
    seg, mask = _hgrn_constants()
    rows, first, last, seq, n_jobs = _hgrn_schedule()

    hps = HG_HEADS_PER_STEP
    step_w = hps * HG_DK

    def zcol(col0):
        return pl.BlockSpec((HG_BLOCK, step_w),
                            lambda h, d, j, r, f, l, s: (r[d * n_jobs + j], col0 // step_w + h))

    state = pl.BlockSpec((None, None, hps, HG_DK, HG_DV), lambda h, d, j, r, f, l, s: (s[j], d, h, 0, 0))
    grid_spec = pltpu.PrefetchScalarGridSpec(
        num_scalar_prefetch=4,
        grid=(HG_HEADS // hps, 2, n_jobs),
        in_specs=[
            zcol(COL_HQ),
            pl.BlockSpec((HG_BLOCK, step_w),
                         lambda h, d, j, r, f, l, s: (r[d * n_jobs + j], (COL_HFF + d * HG_W) // step_w + h)),
            zcol(COL_HI),
            pl.BlockSpec((None, hps, 1, HG_DK), lambda h, d, j, r, f, l, s: (d, h, 0, 0)),
            pl.BlockSpec((None, HG_NSEG * HG_BLOCK, HG_BLOCK), lambda h, d, j, r, f, l, s: (d, 0, 0)),
            pl.BlockSpec((None, HG_LEVELS + 1, HG_BLOCK, HG_BLOCK), lambda h, d, j, r, f, l, s: (d, 0, 0, 0)),
            state,
        ],
        out_specs=[
            pl.BlockSpec((None, HG_BLOCK, step_w), lambda h, d, j, r, f, l, s: (d, r[d * n_jobs + j], h)),
            state,
        ],
        scratch_shapes=[pltpu.VMEM((hps, HG_DV, HG_DK), _f32)],
    )
    return pl.pallas_call(
        _hgrn_kernel,
        grid_spec=grid_spec,
        out_shape=[
            jax.ShapeDtypeStruct((2, N_TOK, HG_W), _f32),
            jax.ShapeDtypeStruct(s0.shape, _f32),
        ],
        compiler_params=pltpu.CompilerParams(dimension_semantics=("arbitrary", "arbitrary", "arbitrary"),
                                             vmem_limit_bytes=VMEM_LIMIT),
        name="hgrn",
    )(rows, first, last, seq, z, z, z, lb.reshape(2, HG_HEADS, 1, HG_DK), seg, mask, s0)


def _hgrn_out_kernel(of_ref, ob_ref, g_ref, ng_ref, o_ref):
    ng = ng_ref[...]
    for hh in range(HG_OUT_W // HG_DV):
        cols = slice(hh * HG_DV, (hh + 1) * HG_DV)
        o = _head_norm(of_ref[:, cols] + ob_ref[:, cols], ng)
        xg = g_ref[:, cols]
        o_ref[:, cols] = (o * (xg * jax.nn.sigmoid(xg))).astype(_bf16)


def _hgrn_out(o_dirs, z, norm_g, layer):
    return pl.pallas_call(
        _hgrn_out_kernel,
        grid=(N_TOK // ROW_TILE, HG_W // HG_OUT_W),
        in_specs=[
            pl.BlockSpec((None, ROW_TILE, HG_OUT_W), lambda i, j: (0, i, j)),
            pl.BlockSpec((None, ROW_TILE, HG_OUT_W), lambda i, j: (1, i, j)),
            pl.BlockSpec((ROW_TILE, HG_OUT_W), lambda i, j: (i, COL_HG // HG_OUT_W + j)),
            pl.BlockSpec((None, 1, HG_DV), lambda i, j: (layer, 0, 0)),
        ],
        out_specs=pl.BlockSpec((ROW_TILE, HG_OUT_W), lambda i, j: (i, j)),
        out_shape=jax.ShapeDtypeStruct((N_TOK, HG_W), _bf16),
        compiler_params=pltpu.CompilerParams(dimension_semantics=("arbitrary", "arbitrary"),
                                             vmem_limit_bytes=VMEM_LIMIT),
        name="hgrn_out",
    )(o_dirs, o_dirs, z, norm_g.reshape(DEPTH, 1, HG_DV))


def _s5_param_kernel(are_ref, aim_ref, ldt_ref, bre_ref, bim_ref, abre_ref, abim_ref, bpre_ref, bpim_ref):
    bre = bre_ref[...]
    bim = bim_ref[...]
    for d in range(2):
        lr = are_ref[d]
        li = aim_ref[d]
        dt = jnp.exp(ldt_ref[d])
        mag = jnp.exp(lr * dt)
        ar = mag * jnp.cos(li * dt)
        ai = mag * jnp.sin(li * dt)
        abre_ref[d] = ar
        abim_ref[d] = ai
        nr = ar - 1.0
        den = lr * lr + li * li
        gr = (nr * lr + ai * li) / den
        gi = (ai * lr - nr * li) / den
        gr3 = gr[:, None, :]
        gi3 = gi[:, None, :]
        bpre_ref[d] = gr3 * bre - gi3 * bim
        bpim_ref[d] = gr3 * bim + gi3 * bre


def _s5_params(a_re, a_im, log_dt, b_re, b_im, c_re, c_im):
    g, pst, ch = S5_GROUPS, S5_STATE, S5_CH
    abre, abim, bpre, bpim = pl.pallas_call(
        _s5_param_kernel,
        out_shape=[
            jax.ShapeDtypeStruct((2, g, pst), _f32),
            jax.ShapeDtypeStruct((2, g, pst), _f32),
            jax.ShapeDtypeStruct((2, g, ch, pst), _f32),
            jax.ShapeDtypeStruct((2, g, ch, pst), _f32),
        ],
        name="s5_params",
    )(a_re, a_im, log_dt.reshape(2, g, 1), b_re.transpose(0, 2, 1), b_im.transpose(0, 2, 1))
    eye = jnp.eye(S5_SLAB_GROUPS, dtype=_f32)

    def b_slabs(bp):
        bp = bp.reshape(2, S5_NSLAB, S5_SLAB_GROUPS, ch, pst)
        w = bp[:, :, :, :, None, :] * eye[None, None, :, None, :, None]
        return w.reshape(2, S5_NSLAB, S5_SLAB_CH, S5_SLAB_ST)

    def c_slabs(cm):
        cm = cm.reshape(S5_NSLAB, S5_SLAB_GROUPS, ch, pst).transpose(0, 1, 3, 2)
        w = cm[:, :, :, None, :] * eye[None, :, None, :, None]
        return w.reshape(S5_NSLAB, S5_SLAB_ST, S5_SLAB_CH)

    b_blk = jnp.concatenate([b_slabs(bpre), b_slabs(bpim)], axis=-1).astype(_bf16)
    return (abre.reshape(2, 1, g * pst), abim.reshape(2, 1, g * pst), b_blk,
            c_slabs(c_re).astype(_bf16), c_slabs(c_im).astype(_bf16))


def _s5_scan_kernel(u_ref, b_ref, cre_ref, cim_ref, are_ref, aim_ref, x0re_ref, x0im_ref,
                    y_ref, xfre_ref, xfim_ref, bu_s, zre_s, zim_s, *, tb_steps, rd):
    d = pl.program_id(0)
    tb = pl.program_id(2)
    n_rg = rd // SUBLANES
    st = S5_SLAB_ST

    @pl.when(tb == 0)
    def _():
        zre_s[...] = x0re_ref[...]
        zim_s[...] = x0im_ref[...]

    u = u_ref[...].reshape(tb_steps * rd, S5_SLAB_CH).astype(_bf16)
    bu_s[...] = jnp.dot(u, b_ref[...], preferred_element_type=_f32).reshape(tb_steps, rd, 2 * st)
    are = jnp.broadcast_to(are_ref[...], (SUBLANES, st))
    aim = jnp.broadcast_to(aim_ref[...], (SUBLANES, st))

    def body(i, carry):
        t = jnp.where(d == 0, i, tb_steps - 1 - i)
        new = []
        for rg in range(n_rg):
            zre, zim = carry[rg]
            rows = slice(rg * SUBLANES, (rg + 1) * SUBLANES)
            nre = are * zre - aim * zim + bu_s[t, rows, 0:st]
            nim = are * zim + aim * zre + bu_s[t, rows, st:2 * st]
            bu_s[t, rows, 0:st] = nre
            bu_s[t, rows, st:2 * st] = nim
            new.append((nre, nim))
        return tuple(new)

    init = tuple((zre_s[rg * SUBLANES:(rg + 1) * SUBLANES, :], zim_s[rg * SUBLANES:(rg + 1) * SUBLANES, :])
                 for rg in range(n_rg))
    fin = lax.fori_loop(0, tb_steps, body, init, unroll=4)
    for rg in range(n_rg):
        zre_s[rg * SUBLANES:(rg + 1) * SUBLANES, :] = fin[rg][0]
        zim_s[rg * SUBLANES:(rg + 1) * SUBLANES, :] = fin[rg][1]

    xs = bu_s[...].reshape(tb_steps * rd, 2 * st)
    y = (jnp.dot(xs[:, 0:st].astype(_bf16), cre_ref[...], preferred_element_type=_f32)
         - jnp.dot(xs[:, st:2 * st].astype(_bf16), cim_ref[...], preferred_element_type=_f32))
    y_ref[...] = y.reshape(tb_steps, rd, S5_SLAB_CH)

    @pl.when(tb == pl.num_programs(2) - 1)
    def _():
        xfre_ref[...] = zre_s[...]
        xfim_ref[...] = zim_s[...]


def _s5_scan(u, params, x0_re, x0_im):
    abre, abim, b_blk, c_re, c_im = params
    t_len, rd, _ = u.shape
    tb_steps = S5_BLOCK_ROWS // rd
    n_tb = t_len // tb_steps
    st = S5_SLAB_ST
    grid = (2, S5_NSLAB, n_tb)
    kern = functools.partial(_s5_scan_kernel, tb_steps=tb_steps, rd=rd)
    tblock = lambda d, t: jnp.where(d == 0, t, n_tb - 1 - t)
    return pl.pallas_call(
        kern,
        grid=grid,
        in_specs=[
            pl.BlockSpec((tb_steps, rd, S5_SLAB_CH), lambda d, s, t: (tblock(d, t), 0, s)),
            pl.BlockSpec((None, None, S5_SLAB_CH, 2 * st), lambda d, s, t: (d, s, 0, 0)),
            pl.BlockSpec((None, st, S5_SLAB_CH), lambda d, s, t: (s, 0, 0)),
            pl.BlockSpec((None, st, S5_SLAB_CH), lambda d, s, t: (s, 0, 0)),
            pl.BlockSpec((None, 1, st), lambda d, s, t: (d, 0, s)),
            pl.BlockSpec((None, 1, st), lambda d, s, t: (d, 0, s)),
            pl.BlockSpec((None, rd, st), lambda d, s, t: (d, 0, s)),
            pl.BlockSpec((None, rd, st), lambda d, s, t: (d, 0, s)),
        ],
        out_specs=[
            pl.BlockSpec((None, tb_steps, rd, S5_SLAB_CH), lambda d, s, t: (d, tblock(d, t), 0, s)),
            pl.BlockSpec((None, rd, st), lambda d, s, t: (d, 0, s)),
            pl.BlockSpec((None, rd, st), lambda d, s, t: (d, 0, s)),
        ],
        out_shape=[
            jax.ShapeDtypeStruct((2, t_len, rd, S5_W), _f32),
            jax.ShapeDtypeStruct((2, rd, S5_GROUPS * S5_STATE), _f32),
            jax.ShapeDtypeStruct((2, rd, S5_GROUPS * S5_STATE), _f32),
        ],
        scratch_shapes=[
            pltpu.VMEM((tb_steps, rd, 2 * st), _f32),
            pltpu.VMEM((rd, st), _f32),
            pltpu.VMEM((rd, st), _f32),
        ],
        compiler_params=pltpu.CompilerParams(dimension_semantics=("arbitrary", "arbitrary", "arbitrary"),
                                             vmem_limit_bytes=VMEM_LIMIT),
        name="s5_scan",
    )(u, b_blk, c_re, c_im, abre, abim, x0_re, x0_im)


def _gelu_tanh(x):
    return 0.5 * x * (1.0 + jnp.tanh(math.sqrt(2.0 / math.pi) * (x + 0.044715 * (x * x * x))))


def _s5_out_kernel(yf_ref, yb_ref, u_ref, d_ref, w_ref, b_ref, o_ref, w_s):
    @pl.when(pl.program_id(0) == 0)
    def _():
        w_s[...] = w_ref[...].astype(_bf16)

    y = _gelu_tanh(yf_ref[...] + yb_ref[...] + d_ref[...] * u_ref[...])
    z = jnp.dot(y.astype(_bf16), w_s[...], preferred_element_type=_f32) + b_ref[...]
    o_ref[...] = (y * jax.nn.sigmoid(z)).astype(_bf16)


def _s5_out(yf, yb, u, d_vec, w_glu, b_glu, layer):
    nt = N_TOK // ROW_TILE
    row = lambda i: (i, 0)
    lay = lambda i: (layer, 0, 0)
    return pl.pallas_call(
        _s5_out_kernel,
        grid=(nt,),
        in_specs=[
            pl.BlockSpec((ROW_TILE, S5_W), row),
            pl.BlockSpec((ROW_TILE, S5_W), row),
            pl.BlockSpec((ROW_TILE, S5_W), row),
            pl.BlockSpec((None, 1, S5_W), lay),
            pl.BlockSpec((None, S5_W, S5_W), lay),
            pl.BlockSpec((None, 1, S5_W), lay),
        ],
        out_specs=pl.BlockSpec((ROW_TILE, S5_W), row),
        out_shape=jax.ShapeDtypeStruct((N_TOK, S5_W), _bf16),
        scratch_shapes=[pltpu.VMEM((S5_W, S5_W), _bf16)],
        compiler_params=pltpu.CompilerParams(dimension_semantics=("arbitrary",), vmem_limit_bytes=VMEM_LIMIT),
        name="s5_out",
    )(yf, yb, u, d_vec.reshape(DEPTH, 1, S5_W), w_glu, b_glu.reshape(DEPTH, 1, S5_W))


def _s5_branch(su_c, su_l, sp, s5_re0, s5_im0, layer):
    params = _s5_params(sp['s5_a_re'][layer], sp['s5_a_im'][layer], sp['s5_log_dt'][layer], sp['s5_b_re'][layer],
                        sp['s5_b_im'][layer], sp['s5_c_re'][layer], sp['s5_c_im'][layer])
    nst = S5_GROUPS * S5_STATE

    def split_dirs(y):
        return y[0].transpose(1, 0, 2), y[1].transpose(1, 0, 2)

    zeros = jnp.zeros((2, BATCH, nst), _f32)
    y_c, xf_re, xf_im = _s5_scan(su_c.transpose(1, 0, 2), params, zeros, zeros)
    yf_c, yb_c = split_dirs(y_c)
    pad = ((0, 0), (0, SUBLANES - DEC_BATCH), (0, 0))
    u_l = jnp.pad(su_l.transpose(1, 0, 2), pad)

    def lat_state(x0):
        return jnp.pad(x0.transpose(1, 0, 2, 3).reshape(2, DEC_BATCH, nst), pad)

    y_l, _, _ = _s5_scan(u_l, params, lat_state(s5_re0), lat_state(s5_im0))
    yf_l, yb_l = split_dirs(y_l[:, :, :DEC_BATCH])
    yf = jnp.concatenate([yf_c.reshape(N_CTX, S5_W), yf_l.reshape(N_LAT, S5_W)], axis=0)
    yb = jnp.concatenate([yb_c.reshape(N_CTX, S5_W), yb_l.reshape(N_LAT, S5_W)], axis=0)
    u = jnp.concatenate([su_c.reshape(N_CTX, S5_W), su_l.reshape(N_LAT, S5_W)], axis=0)
    s_out = _s5_out(yf, yb, u, sp['s5_d'], sp['s5_w_glu'], sp['s5_b_glu'], layer)

    def ctx_state(xf):
        return xf.reshape(2, BATCH, S5_GROUPS, S5_STATE).transpose(1, 0, 2, 3)

    return s_out, ctx_state(xf_re), ctx_state(xf_im)


def kernel(x_prompt, x_sample, cache_k, cache_v, state_hgrn, state_s5_re, state_s5_im, c, c_ctx, w_mod, b_mod, norm1_g, norm2_g, w_in, q_norm_g, k_norm_g, attn_sink, hg_lb_logits, hg_norm_g, s5_a_re, s5_a_im, s5_log_dt, s5_b_re, s5_b_im, s5_c_re, s5_c_im, s5_d, s5_w_glu, s5_b_glu, w_br_attn, w_br_hg, w_br_s5, w_out, router_w, router_b, exp_w1, exp_b1, exp_w2, exp_b2):
    lb_all = jnp.cumsum(jax.nn.softmax(hg_lb_logits.astype(_f32), axis=0), axis=0)
    lb_all = lb_all - lb_all[:1]
    cond_rows = jnp.concatenate([c_ctx[None, :], c, jnp.zeros((SUBLANES - N_GROUPS, D_MODEL), _f32)], axis=0)
    x = jnp.concatenate([x_prompt.reshape(N_CTX, D_MODEL), x_sample.reshape(N_LAT, D_MODEL)], axis=0)
    cache_k_t = cache_k.transpose(0, 1, 3, 2, 4)
    cache_v_t = cache_v.transpose(0, 1, 3, 2, 4)
    zero_state = jnp.zeros((BATCH, 2, HG_HEADS, HG_DK, HG_DV), _f32)
    new_k, new_v, new_hg, new_re, new_im = [], [], [], [], []
    moe_p = {"norm2_g": norm2_g, "router_w": router_w, "router_b": router_b, "exp_w1": exp_w1, "exp_b1": exp_b1,
             "exp_w2": exp_w2, "exp_b2": exp_b2}
    s5_p = {'s5_a_re': s5_a_re, 's5_a_im': s5_a_im, 's5_log_dt': s5_log_dt, 's5_b_re': s5_b_re, 's5_b_im': s5_b_im,
            's5_c_re': s5_c_re, 's5_c_im': s5_c_im, 's5_d': s5_d, 's5_w_glu': s5_w_glu, 's5_b_glu': s5_b_glu}
    for l in range(DEPTH):
        m = _modulation(cond_rows, w_mod, b_mod, l)[:N_GROUPS]
        mods = [a[:, None, :] for a in jnp.split(m, N_MOD, axis=-1)]
        sh1, sc1, gt1 = mods[0], mods[1], mods[2]
        z = _in_proj(_norm_mod(x, norm1_g, sc1, sh1, l), w_in, l)
        a_ctx, k_l, v_l = _ctx_attention(z, q_norm_g, k_norm_g, attn_sink, l)
        a_lat = _win_attention(z, cache_k_t, cache_v_t, q_norm_g, k_norm_g, attn_sink, l)
        a_out = jnp.concatenate([a_ctx, a_lat], axis=0)
        o_dirs, hg_fin = _hgrn(z, lb_all[l], jnp.concatenate([zero_state, state_hgrn[:, l]], axis=0))
        h_out = _hgrn_out(o_dirs, z, hg_norm_g, l)
        su = z[:, COL_SU:COL_SU + S5_W]
        s_out, re_l, im_l = _s5_branch(su[:N_CTX].reshape(BATCH, SEQ, S5_W),
                                       su[N_CTX:].reshape(DEC_BATCH, DEC_SEQ, S5_W), s5_p,
                                       state_s5_re[:, l], state_s5_im[:, l], l)
        y = _merge(a_out, h_out, s_out, z, w_br_attn, w_br_hg, w_br_s5, l)
        x = _out_proj(y, w_out, x, gt1, l)
        new_k.append(k_l.reshape(BATCH, SEQ, N_KV_HEADS, HEAD_DIM))
        new_v.append(v_l.reshape(BATCH, SEQ, N_KV_HEADS, HEAD_DIM))
        new_hg.append(hg_fin[:BATCH])
        new_re.append(re_l)
        new_im.append(im_l)
        x = _moe_sublayer(x, mods, moe_p, l)
    return (x[:N_CTX].reshape(BATCH, SEQ, D_MODEL), x[N_CTX:].reshape(DEC_BATCH, DEC_SEQ, D_MODEL),
            jnp.stack(new_k, axis=1), jnp.stack(new_v, axis=1), jnp.stack(new_hg, axis=1),
            jnp.stack(new_re, axis=1), jnp.stack(new_im, axis=1))
```

```python
import functools
import math

import jax
import jax.numpy as jnp
import numpy as np
from jax import lax
from jax.experimental import pallas as pl
from jax.experimental.pallas import tpu as pltpu

D_MODEL = 2048
BATCH = 16
SEQ = 256
DEPTH = 2
DEC_BATCH = 2
DEC_SEQ = 2048
GRID_W = 64
N_HEADS = 8
N_KV_HEADS = 2
HEAD_DIM = 128
Q_GROUP = N_HEADS // N_KV_HEADS
WINDOW = 128
ATTN_BLOCK = 128
ROPE_BASE = 10000.0
ATT_Q_W = N_HEADS * HEAD_DIM
ATT_KV_W = N_KV_HEADS * HEAD_DIM
HG_HEADS = 8
HG_DK = 128
HG_DV = 128
HG_W = HG_HEADS * HG_DK
S5_W = 1024
S5_CH = 16
S5_GROUPS = S5_W // S5_CH
S5_STATE = 64
N_EXPERTS = 32
TOP_K = 4
D_FF = 2048
SWIGLU_ALPHA = 1.702
SWIGLU_LIMIT = 7.0
N_MOD = 6
EPS = 1e-6
IN_SIZES = (ATT_Q_W, ATT_KV_W, ATT_KV_W, HG_W, HG_W, HG_W, HG_W, HG_W, S5_W, D_MODEL, D_MODEL, D_MODEL)
IN_COLS = sum(IN_SIZES)

N_CTX = BATCH * SEQ
N_LAT = DEC_BATCH * DEC_SEQ
N_TOK = N_CTX + N_LAT
N_GROUPS = 1 + DEC_BATCH

LANES = 128
ROW_TILE = 256
VMEM_LIMIT = 56 * 1024 * 1024

MOE_SUPER = 1024
MOE_SUB = 256
MOE_FC = 512
MOE_NC = D_FF // MOE_FC
MOE_NS = (N_TOK * TOP_K) // MOE_SUPER + N_EXPERTS
MOD_TN = 1536
PROJ_TN = 1536
PROJ_TM = 512
MERGE_TN = 512
HG_OUT_W = 512
DISPATCH_TOKENS = 64
COMBINE_TILE = 64

S5_SLAB_GROUPS = 8
S5_NSLAB = S5_GROUPS // S5_SLAB_GROUPS
S5_SLAB_CH = S5_SLAB_GROUPS * S5_CH
S5_SLAB_ST = S5_SLAB_GROUPS * S5_STATE
S5_BLOCK_ROWS = 1024
SUBLANES = 8

_bf16 = jnp.bfloat16
_f32 = jnp.float32


def _group_of_tile(i):
    ctx_tiles = N_CTX // ROW_TILE
    lat_tiles = DEC_SEQ // ROW_TILE
    return jnp.where(i < ctx_tiles, 0, 1 + (i - ctx_tiles) // lat_tiles)


def _split_bf16(a):
    hi = a.astype(_bf16)
    lo = (a - hi.astype(_f32)).astype(_bf16)
    return hi, lo


def _router_kernel(x_ref, g_ref, sc_ref, sh_ref, rw_ref, rb_ref, h_ref, idx_ref, w_ref, rank_ref, cnt_ref, cnt_s):
    x = x_ref[...]
    y = x * lax.rsqrt(jnp.mean(x * x, axis=-1, keepdims=True) + EPS) * g_ref[...]
    h = y * (1.0 + sc_ref[...]) + sh_ref[...]
    h_ref[...] = h
    h_hi, h_lo = _split_bf16(h)
    r_hi, r_lo = _split_bf16(rw_ref[...])
    logits = (jnp.dot(h_hi, r_hi, preferred_element_type=_f32)
              + jnp.dot(h_hi, r_lo, preferred_element_type=_f32)
              + jnp.dot(h_lo, r_hi, preferred_element_type=_f32)) + rb_ref[...]
    lane = lax.broadcasted_iota(jnp.int32, logits.shape, 1)
    work = logits
    vals, idxs = [], []
    for _ in range(TOP_K):
        m = jnp.max(work, axis=-1, keepdims=True)
        i = jnp.min(jnp.where(work == m, lane, N_EXPERTS), axis=-1, keepdims=True)
        vals.append(m)
        idxs.append(i)
        work = jnp.where(lane == i, -jnp.inf, work)
    es = [jnp.exp(v - vals[0]) for v in vals]
    den = es[0] + es[1] + es[2] + es[3]
    @pl.when(pl.program_id(0) == 0)
    def _():
        cnt_s[...] = jnp.zeros(cnt_s.shape, _f32)

    out_lane = lax.broadcasted_iota(jnp.int32, idx_ref.shape, 1)
    tr = lax.broadcasted_iota(jnp.int32, (ROW_TILE, ROW_TILE), 0)
    tc = lax.broadcasted_iota(jnp.int32, (ROW_TILE, ROW_TILE), 1)
    earlier = jnp.where(tc < tr, 1.0, 0.0).astype(_bf16)
    running = cnt_s[...]
    idx_out = jnp.zeros(idx_ref.shape, jnp.int32)
    w_out = jnp.zeros(w_ref.shape, _f32)
    rank_out = jnp.zeros(rank_ref.shape, jnp.int32)
    for k in range(TOP_K):
        onehot = jnp.where(out_lane == idxs[k], 1.0, 0.0)
        before = jnp.dot(earlier, onehot.astype(_bf16), preferred_element_type=_f32)
        rank = jnp.sum(onehot * (running + before), axis=-1, keepdims=True)
        running = running + jnp.sum(onehot, axis=0, keepdims=True)
        idx_out = jnp.where(out_lane == k, idxs[k], idx_out)
        w_out = jnp.where(out_lane == k, es[k] / den, w_out)
        rank_out = jnp.where(out_lane == k, rank.astype(jnp.int32), rank_out)
    cnt_s[...] = running
    idx_ref[...] = idx_out
    w_ref[...] = w_out
    rank_ref[...] = rank_out
    cnt_ref[...] = running


def _router(x, norm_g, sc, sh, router_w, router_b, layer):
    nt = N_TOK // ROW_TILE
    row = lambda i: (i, 0)
    grp = lambda i: (_group_of_tile(i), 0, 0)
    lay = lambda i: (layer, 0, 0)
    h, idx, w, rank, cnt = pl.pallas_call(
        _router_kernel,
        grid=(nt,),
        in_specs=[
            pl.BlockSpec((ROW_TILE, D_MODEL), row),
            pl.BlockSpec((None, 1, D_MODEL), lay),
            pl.BlockSpec((None, 1, D_MODEL), grp),
            pl.BlockSpec((None, 1, D_MODEL), grp),
            pl.BlockSpec((None, D_MODEL, N_EXPERTS), lay),
            pl.BlockSpec((None, 1, N_EXPERTS), lay),
        ],
        out_specs=[
            pl.BlockSpec((ROW_TILE, D_MODEL), row),
            pl.BlockSpec((ROW_TILE, LANES), row),
            pl.BlockSpec((ROW_TILE, LANES), row),
            pl.BlockSpec((ROW_TILE, LANES), row),
            pl.BlockSpec((1, LANES), lambda i: (0, 0)),
        ],
        out_shape=[
            jax.ShapeDtypeStruct((N_TOK, D_MODEL), _f32),
            jax.ShapeDtypeStruct((N_TOK, LANES), jnp.int32),
            jax.ShapeDtypeStruct((N_TOK, LANES), _f32),
            jax.ShapeDtypeStruct((N_TOK, LANES), jnp.int32),
            jax.ShapeDtypeStruct((1, LANES), _f32),
        ],
        scratch_shapes=[pltpu.VMEM((1, LANES), _f32)],
        compiler_params=pltpu.CompilerParams(dimension_semantics=("arbitrary",), vmem_limit_bytes=VMEM_LIMIT),
        name="router",
    )(x, norm_g.reshape(DEPTH, 1, D_MODEL), sc, sh, router_w, router_b.reshape(DEPTH, 1, N_EXPERTS))
    return h, idx[:, :TOP_K], w, rank[:, :TOP_K], cnt[0, :N_EXPERTS].astype(jnp.int32)


def _moe_kernel(exp_ref, rows_ref, used_ref, x_hbm, w1g_ref, w1u_ref, b1g_ref, b1u_ref, w2_ref, b2_ref,
                o_hbm, w1g_s, w1u_s, w2_s, x_s, acc_s, sem):
    s = pl.program_id(0)
    c = pl.program_id(1)
    nvalid = rows_ref[s]
    tile_rows = pl.ds(pl.multiple_of(s * MOE_SUPER, MOE_SUPER), MOE_SUPER)

    def write_out():
        out_copy = pltpu.make_async_copy(acc_s, o_hbm.at[tile_rows], sem)
        out_copy.start()
        out_copy.wait()

    @pl.when(nvalid > 0)
    def _():
        @pl.when(c == 0)
        def _():
            in_copy = pltpu.make_async_copy(x_hbm.at[tile_rows], acc_s, sem)
            in_copy.start()
            in_copy.wait()
            x_s[...] = acc_s[...].astype(_bf16)
            acc_s[...] = jnp.zeros(acc_s.shape, _f32)

        w1g_s[...] = w1g_ref[...].astype(_bf16)
        w1u_s[...] = w1u_ref[...].astype(_bf16)
        w2_s[...] = w2_ref[...].astype(_bf16)
        for r in range(MOE_SUPER // MOE_SUB):
            @pl.when(r * MOE_SUB < nvalid)
            def _():
                rows = pl.ds(r * MOE_SUB, MOE_SUB)
                xs = x_s[rows, :]
                zg = jnp.dot(xs, w1g_s[...], preferred_element_type=_f32) + b1g_ref[...]
                zu = jnp.dot(xs, w1u_s[...], preferred_element_type=_f32) + b1u_ref[...]
                g = jnp.minimum(zg, SWIGLU_LIMIT)
                u = jnp.clip(zu, -SWIGLU_LIMIT, SWIGLU_LIMIT)
                act = g * jax.nn.sigmoid(SWIGLU_ALPHA * g) * (u + 1.0)
                acc_s[rows, :] += jnp.dot(act.astype(_bf16), w2_s[...], preferred_element_type=_f32)

        @pl.when(c == MOE_NC - 1)
        def _():
            acc_s[...] = acc_s[...] + b2_ref[...]
            write_out()

    @pl.when((nvalid == 0) & (c == 0))
    def _():
        acc_s[...] = jnp.zeros(acc_s.shape, _f32)
        write_out()


def _moe_ffn(x_sorted, st_expert, st_rows, n_used, w1, b1, w2, b2, layer):
    def chunk(s, c, u_ref):
        return jnp.where(s < u_ref[0], c, MOE_NC - 1)

    def w1g_map(s, c, e_ref, r_ref, u_ref):
        return (layer, e_ref[s], 0, chunk(s, c, u_ref))

    def w1u_map(s, c, e_ref, r_ref, u_ref):
        return (layer, e_ref[s], 0, MOE_NC + chunk(s, c, u_ref))

    def w2_map(s, c, e_ref, r_ref, u_ref):
        return (layer, e_ref[s], chunk(s, c, u_ref), 0)

    def b2_map(s, c, e_ref, r_ref, u_ref):
        return (layer, e_ref[s], 0, 0)

    grid_spec = pltpu.PrefetchScalarGridSpec(
        num_scalar_prefetch=3,
        grid=(MOE_NS, MOE_NC),
        in_specs=[
            pl.BlockSpec(memory_space=pl.ANY),
            pl.BlockSpec((None, None, D_MODEL, MOE_FC), w1g_map),
            pl.BlockSpec((None, None, D_MODEL, MOE_FC), w1u_map),
            pl.BlockSpec((None, None, 1, MOE_FC), w1g_map),
            pl.BlockSpec((None, None, 1, MOE_FC), w1u_map),
            pl.BlockSpec((None, None, MOE_FC, D_MODEL), w2_map),
            pl.BlockSpec((None, None, 1, D_MODEL), b2_map),
        ],
        out_specs=pl.BlockSpec(memory_space=pl.ANY),
        scratch_shapes=[
            pltpu.VMEM((D_MODEL, MOE_FC), _bf16),
            pltpu.VMEM((D_MODEL, MOE_FC), _bf16),
            pltpu.VMEM((MOE_FC, D_MODEL), _bf16),
            pltpu.VMEM((MOE_SUPER, D_MODEL), _bf16),
            pltpu.VMEM((MOE_SUPER, D_MODEL), _f32),
            pltpu.SemaphoreType.DMA(()),
        ],
    )
    return pl.pallas_call(
        _moe_kernel,
        grid_spec=grid_spec,
        out_shape=jax.ShapeDtypeStruct((MOE_NS * MOE_SUPER, D_MODEL), _f32),
        compiler_params=pltpu.CompilerParams(dimension_semantics=("arbitrary", "arbitrary"),
                                             vmem_limit_bytes=VMEM_LIMIT),
        name="moe_ffn",
    )(st_expert, st_rows, n_used, x_sorted, w1, w1,
      b1.reshape(DEPTH, N_EXPERTS, 1, 2 * D_FF), b1.reshape(DEPTH, N_EXPERTS, 1, 2 * D_FF),
      w2, b2.reshape(DEPTH, N_EXPERTS, 1, D_MODEL))


def _issue_row_copies(idx_ref, src_hbm, dst_ref, sem, n):
    def start(p, carry):
        for prio in range(2):
            r = 2 * p + prio
            pltpu.make_async_copy(src_hbm.at[idx_ref[0, r]], dst_ref.at[r], sem).start(priority=prio)
        return carry

    lax.fori_loop(0, n // 2, start, 0, unroll=4)
    pltpu.make_async_copy(src_hbm.at[pl.ds(0, n)], dst_ref, sem).wait()


def _route_tables(top_idx, rank, counts):
    n_st = (counts + MOE_SUPER - 1) // MOE_SUPER
    st_end = jnp.cumsum(n_st)
    st_start = st_end - n_st
    n_used = st_end[-1]
    s_ids = jnp.arange(MOE_NS, dtype=jnp.int32)
    s_clamped = jnp.minimum(s_ids, n_used - 1)
    st_expert = jnp.sum((st_end[None, :] <= s_clamped[:, None]).astype(jnp.int32), axis=1)
    is_exp = st_expert[:, None] == jnp.arange(N_EXPERTS, dtype=jnp.int32)[None, :]
    tile_in_expert = s_clamped - jnp.sum(jnp.where(is_exp, st_start[None, :], 0), axis=1)
    st_rows = jnp.clip(jnp.sum(jnp.where(is_exp, counts[None, :], 0), axis=1) - tile_in_expert * MOE_SUPER,
                       0, MOE_SUPER)
    st_rows = jnp.where(s_ids < n_used, st_rows, 0).astype(jnp.int32)
    first_row = jnp.sum(jnp.where(top_idx[:, :, None] == jnp.arange(N_EXPERTS, dtype=jnp.int32),
                                  st_start * MOE_SUPER, 0), axis=-1)
    return st_expert, st_rows, n_used.reshape(1).astype(jnp.int32), first_row + rank


def _dispatch_kernel(pos_ref, h_ref, zero_hbm, o_hbm, sem):
    n = TOP_K * DISPATCH_TOKENS

    def start(p, carry):
        for prio in range(2):
            j = 2 * p + prio
            pltpu.make_async_copy(h_ref.at[j // TOP_K], o_hbm.at[pos_ref[0, j]], sem).start(priority=prio)
        return carry

    lax.fori_loop(0, n // 2, start, 0, unroll=4)
    pltpu.make_async_copy(o_hbm.at[pl.ds(0, n)], o_hbm.at[pl.ds(0, n)], sem).wait()


def _dispatch(h_words, pos):
    nt = N_TOK // DISPATCH_TOKENS
    width = h_words.shape[1]
    n_rows = MOE_NS * MOE_SUPER
    return pl.pallas_call(
        _dispatch_kernel,
        grid=(nt,),
        in_specs=[
            pl.BlockSpec((None, 1, TOP_K * DISPATCH_TOKENS), lambda i: (i, 0, 0), memory_space=pltpu.SMEM),
            pl.BlockSpec((DISPATCH_TOKENS, width), lambda i: (i, 0)),
            pl.BlockSpec(memory_space=pl.ANY),
        ],
        out_specs=pl.BlockSpec(memory_space=pl.ANY),
        out_shape=jax.ShapeDtypeStruct((n_rows, width), h_words.dtype),
        scratch_shapes=[pltpu.SemaphoreType.DMA(())],
        input_output_aliases={2: 0},
        compiler_params=pltpu.CompilerParams(dimension_semantics=("arbitrary",), vmem_limit_bytes=VMEM_LIMIT),
        name="moe_dispatch",
    )(pos.reshape(nt, 1, TOP_K * DISPATCH_TOKENS), h_words, jnp.zeros((n_rows, width), h_words.dtype))


def _combine_kernel(pos_ref, y_hbm, x_ref, gt_ref, w_ref, o_ref, buf, sem):
    _issue_row_copies(pos_ref, y_hbm, buf, sem, TOP_K * COMBINE_TILE)
    y = w_ref[:, 0:1] * buf[0:COMBINE_TILE, :]
    for k in range(1, TOP_K):
        y = y + w_ref[:, k:k + 1] * buf[k * COMBINE_TILE:(k + 1) * COMBINE_TILE, :]
    o_ref[...] = x_ref[...] + gt_ref[...] * y


def _combine(y_sorted, pos, top_w, x, gt):
    nt = N_TOK // COMBINE_TILE
    tiles_per_row_tile = ROW_TILE // COMBINE_TILE
    pos_tiles = pos.reshape(nt, COMBINE_TILE, TOP_K).transpose(0, 2, 1).reshape(nt, 1, TOP_K * COMBINE_TILE)
    return pl.pallas_call(
        _combine_kernel,
        grid=(nt,),
        in_specs=[
            pl.BlockSpec((None, 1, TOP_K * COMBINE_TILE), lambda i: (i, 0, 0), memory_space=pltpu.SMEM),
            pl.BlockSpec(memory_space=pl.ANY),
            pl.BlockSpec((COMBINE_TILE, D_MODEL), lambda i: (i, 0)),
            pl.BlockSpec((None, 1, D_MODEL), lambda i: (_group_of_tile(i // tiles_per_row_tile), 0, 0)),
            pl.BlockSpec((COMBINE_TILE, LANES), lambda i: (i, 0)),
        ],
        out_specs=pl.BlockSpec((COMBINE_TILE, D_MODEL), lambda i: (i, 0)),
        out_shape=jax.ShapeDtypeStruct((N_TOK, D_MODEL), _f32),
        scratch_shapes=[pltpu.VMEM((TOP_K * COMBINE_TILE, D_MODEL), _f32), pltpu.SemaphoreType.DMA(())],
        compiler_params=pltpu.CompilerParams(dimension_semantics=("arbitrary",), vmem_limit_bytes=VMEM_LIMIT),
        name="moe_combine",
    )(pos_tiles, y_sorted, x, gt, top_w)


def _moe_sublayer(x, mods, p, layer):
    sh2, sc2, gt2 = mods[3], mods[4], mods[5]
    h, top_idx, top_w, rank, counts = _router(x, p["norm2_g"], sc2, sh2, p["router_w"], p["router_b"], layer)
    st_expert, st_rows, n_used, pos = _route_tables(top_idx, rank, counts)
    x_sorted = _dispatch(h, pos)
    y_sorted = _moe_ffn(x_sorted, st_expert, st_rows, n_used, p["exp_w1"], p["exp_b1"], p["exp_w2"], p["exp_b2"],
                        layer)
    return _combine(y_sorted, pos, top_w, x, gt2)


def _mod_kernel(c_ref, w_ref, b_ref, o_ref):
    c = c_ref[...]
    a = (c * jax.nn.sigmoid(c)).astype(_bf16)
    o_ref[...] = jnp.dot(a, w_ref[...].astype(_bf16), preferred_element_type=_f32) + b_ref[...]


def _modulation(cond_rows, w_mod, b_mod, layer):
    n = N_MOD * D_MODEL
    return pl.pallas_call(
        _mod_kernel,
        grid=(n // MOD_TN,),
        in_specs=[
            pl.BlockSpec((SUBLANES, D_MODEL), lambda j: (0, 0)),
            pl.BlockSpec((None, D_MODEL, MOD_TN), lambda j: (layer, 0, j)),
            pl.BlockSpec((None, 1, MOD_TN), lambda j: (layer, 0, j)),
        ],
        out_specs=pl.BlockSpec((SUBLANES, MOD_TN), lambda j: (0, j)),
        out_shape=jax.ShapeDtypeStruct((SUBLANES, n), _f32),
        compiler_params=pltpu.CompilerParams(dimension_semantics=("arbitrary",), vmem_limit_bytes=VMEM_LIMIT),
        name="modulation",
    )(cond_rows, w_mod, b_mod.reshape(DEPTH, 1, n))


def _norm_mod_kernel(x_ref, g_ref, sc_ref, sh_ref, h_ref):
    x = x_ref[...]
    y = x * lax.rsqrt(jnp.mean(x * x, axis=-1, keepdims=True) + EPS) * g_ref[...]
    h_ref[...] = (y * (1.0 + sc_ref[...]) + sh_ref[...]).astype(_bf16)


def _norm_mod(x, norm_g, sc, sh, layer):
    row = lambda i: (i, 0)
    grp = lambda i: (_group_of_tile(i), 0, 0)
    lay = lambda i: (layer, 0, 0)
    return pl.pallas_call(
        _norm_mod_kernel,
        grid=(N_TOK // ROW_TILE,),
        in_specs=[
            pl.BlockSpec((ROW_TILE, D_MODEL), row),
            pl.BlockSpec((None, 1, D_MODEL), lay),
            pl.BlockSpec((None, 1, D_MODEL), grp),
            pl.BlockSpec((None, 1, D_MODEL), grp),
        ],
        out_specs=pl.BlockSpec((ROW_TILE, D_MODEL), row),
        out_shape=jax.ShapeDtypeStruct((N_TOK, D_MODEL), _bf16),
        compiler_params=pltpu.CompilerParams(dimension_semantics=("arbitrary",), vmem_limit_bytes=VMEM_LIMIT),
        name="norm_mod",
    )(x, norm_g.reshape(DEPTH, 1, D_MODEL), sc, sh)


def _in_proj_kernel(h_ref, w_ref, o_ref, w_s):
    @pl.when(pl.program_id(1) == 0)
    def _():
        w_s[...] = w_ref[...].astype(_bf16)

    o_ref[...] = jnp.dot(h_ref[...], w_s[...], preferred_element_type=_f32)


def _in_proj(h, w_in, layer):
    return pl.pallas_call(
        _in_proj_kernel,
        grid=(IN_COLS // PROJ_TN, N_TOK // PROJ_TM),
        in_specs=[
            pl.BlockSpec((PROJ_TM, D_MODEL), lambda j, i: (i, 0)),
            pl.BlockSpec((None, D_MODEL, PROJ_TN), lambda j, i: (layer, 0, j)),
        ],
        out_specs=pl.BlockSpec((PROJ_TM, PROJ_TN), lambda j, i: (i, j)),
        out_shape=jax.ShapeDtypeStruct((N_TOK, IN_COLS), _f32),
        scratch_shapes=[pltpu.VMEM((D_MODEL, PROJ_TN), _bf16)],
        compiler_params=pltpu.CompilerParams(dimension_semantics=("arbitrary", "arbitrary"),
                                             vmem_limit_bytes=VMEM_LIMIT),
        name="in_proj",
    )(h, w_in)


def _merge_kernel(a_ref, h_ref, s_ref, ga_ref, gh_ref, gs_ref, wa_ref, wh_ref, ws_ref, o_ref, wa_s, wh_s, ws_s):
    @pl.when(pl.program_id(1) == 0)
    def _():
        wa_s[...] = wa_ref[...].astype(_bf16)
        wh_s[...] = wh_ref[...].astype(_bf16)
        ws_s[...] = ws_ref[...].astype(_bf16)

    y = (jax.nn.sigmoid(ga_ref[...]) * jnp.dot(a_ref[...], wa_s[...], preferred_element_type=_f32)
         + jax.nn.sigmoid(gh_ref[...]) * jnp.dot(h_ref[...], wh_s[...], preferred_element_type=_f32)
         + jax.nn.sigmoid(gs_ref[...]) * jnp.dot(s_ref[...], ws_s[...], preferred_element_type=_f32))
    o_ref[...] = y.astype(_bf16)


def _merge(a_out, h_out, s_out, z, w_a, w_h, w_s, layer):
    ga0, gh0, gs0 = (sum(IN_SIZES[:k]) // MERGE_TN for k in (9, 10, 11))
    act = lambda j, i: (i, 0)
    wmap = lambda j, i: (layer, 0, j)
    return pl.pallas_call(
        _merge_kernel,
        grid=(D_MODEL // MERGE_TN, N_TOK // ROW_TILE),
        in_specs=[
            pl.BlockSpec((ROW_TILE, ATT_Q_W), act),
            pl.BlockSpec((ROW_TILE, HG_W), act),
            pl.BlockSpec((ROW_TILE, S5_W), act),
            pl.BlockSpec((ROW_TILE, MERGE_TN), lambda j, i: (i, ga0 + j)),
            pl.BlockSpec((ROW_TILE, MERGE_TN), lambda j, i: (i, gh0 + j)),
            pl.BlockSpec((ROW_TILE, MERGE_TN), lambda j, i: (i, gs0 + j)),
            pl.BlockSpec((None, ATT_Q_W, MERGE_TN), wmap),
            pl.BlockSpec((None, HG_W, MERGE_TN), wmap),
            pl.BlockSpec((None, S5_W, MERGE_TN), wmap),
        ],
        out_specs=pl.BlockSpec((ROW_TILE, MERGE_TN), lambda j, i: (i, j)),
        out_shape=jax.ShapeDtypeStruct((N_TOK, D_MODEL), _bf16),
        scratch_shapes=[pltpu.VMEM((ATT_Q_W, MERGE_TN), _bf16), pltpu.VMEM((HG_W, MERGE_TN), _bf16),
                        pltpu.VMEM((S5_W, MERGE_TN), _bf16)],
        compiler_params=pltpu.CompilerParams(dimension_semantics=("arbitrary", "arbitrary"),
                                             vmem_limit_bytes=VMEM_LIMIT),
        name="merge",
    )(a_out, h_out, s_out, z, z, z, w_a, w_h, w_s)


def _out_proj_kernel(y_ref, w_ref, x_ref, gt_ref, o_ref, w_s):
    @pl.when(pl.program_id(1) == 0)
    def _():
        w_s[...] = w_ref[...].astype(_bf16)

    o_ref[...] = x_ref[...] + gt_ref[...] * jnp.dot(y_ref[...], w_s[...], preferred_element_type=_f32)


def _out_proj(y, w_out, x, gt, layer):
    return pl.pallas_call(
        _out_proj_kernel,
        grid=(D_MODEL // MERGE_TN, N_TOK // ROW_TILE),
        in_specs=[
            pl.BlockSpec((ROW_TILE, D_MODEL), lambda j, i: (i, 0)),
            pl.BlockSpec((None, D_MODEL, MERGE_TN), lambda j, i: (layer, 0, j)),
            pl.BlockSpec((ROW_TILE, MERGE_TN), lambda j, i: (i, j)),
            pl.BlockSpec((None, 1, MERGE_TN), lambda j, i: (_group_of_tile(i), 0, j)),
        ],
        out_specs=pl.BlockSpec((ROW_TILE, MERGE_TN), lambda j, i: (i, j)),
        out_shape=jax.ShapeDtypeStruct((N_TOK, D_MODEL), _f32),
        scratch_shapes=[pltpu.VMEM((D_MODEL, MERGE_TN), _bf16)],
        compiler_params=pltpu.CompilerParams(dimension_semantics=("arbitrary", "arbitrary"),
                                             vmem_limit_bytes=VMEM_LIMIT),
        name="out_proj",
    )(y, w_out, x, gt)


COL_Q, COL_K, COL_V, COL_HQ, COL_HFF, COL_HFB, COL_HI, COL_HG, COL_SU = (sum(IN_SIZES[:k]) for k in range(9))
KV_GROUP_W = Q_GROUP * HEAD_DIM


def _head_norm(x, g):
    return x * lax.rsqrt(jnp.mean(x * x, axis=-1, keepdims=True) + EPS) * g


def _dot_nt(a, b):
    return lax.dot_general(a, b, (((1,), (1,)), ((), ())), preferred_element_type=_f32)


def _softmax_with_sink(s, sink):
    m = jnp.maximum(jnp.max(s, axis=-1, keepdims=True), sink)
    p = jnp.exp(s - m)
    den = jnp.sum(p, axis=-1, keepdims=True) + jnp.exp(sink - m)
    return p / den


def _ctx_attn_kernel(q_ref, k_ref, v_ref, qg_ref, kg_ref, sink_ref, o_ref, ko_ref, vo_ref):
    kn = _head_norm(k_ref[...], kg_ref[...])
    v = v_ref[...]
    ko_ref[...] = kn
    vo_ref[...] = v
    knb = kn.astype(_bf16)
    vb = v.astype(_bf16)
    for g in range(Q_GROUP):
        cols = slice(g * HEAD_DIM, (g + 1) * HEAD_DIM)
        qn = _head_norm(q_ref[:, cols], qg_ref[...])
        s = _dot_nt(qn.astype(_bf16), knb) * HEAD_DIM ** -0.5
        p = _softmax_with_sink(s, sink_ref[0:1, g:g + 1])
        o_ref[:, cols] = jnp.dot(p.astype(_bf16), vb, preferred_element_type=_f32).astype(_bf16)


def _ctx_attention(z, q_g, k_g, sink, layer):
    lay = lambda b, j: (layer, 0, 0)
    return pl.pallas_call(
        _ctx_attn_kernel,
        grid=(BATCH, N_KV_HEADS),
        in_specs=[
            pl.BlockSpec((SEQ, KV_GROUP_W), lambda b, j: (b, COL_Q // KV_GROUP_W + j)),
            pl.BlockSpec((SEQ, HEAD_DIM), lambda b, j: (b, COL_K // HEAD_DIM + j)),
            pl.BlockSpec((SEQ, HEAD_DIM), lambda b, j: (b, COL_V // HEAD_DIM + j)),
            pl.BlockSpec((None, 1, HEAD_DIM), lay),
            pl.BlockSpec((None, 1, HEAD_DIM), lay),
            pl.BlockSpec((None, None, 1, Q_GROUP), lambda b, j: (layer, j, 0, 0)),
        ],
        out_specs=[
            pl.BlockSpec((SEQ, KV_GROUP_W), lambda b, j: (b, j)),
            pl.BlockSpec((SEQ, HEAD_DIM), lambda b, j: (b, j)),
            pl.BlockSpec((SEQ, HEAD_DIM), lambda b, j: (b, j)),
        ],
        out_shape=[
            jax.ShapeDtypeStruct((N_CTX, ATT_Q_W), _bf16),
            jax.ShapeDtypeStruct((N_CTX, ATT_KV_W), _f32),
            jax.ShapeDtypeStruct((N_CTX, ATT_KV_W), _f32),
        ],
        compiler_params=pltpu.CompilerParams(dimension_semantics=("arbitrary", "arbitrary"),
                                             vmem_limit_bytes=VMEM_LIMIT),
        name="ctx_attention",
    )(z, z, z, q_g.reshape(DEPTH, 1, HEAD_DIM), k_g.reshape(DEPTH, 1, HEAD_DIM),
      sink.reshape(DEPTH, N_KV_HEADS, 1, Q_GROUP))


def _rope_tables():
    half = HEAD_DIM // 2
    nf = half // 2
    t = np.arange(DEC_SEQ)
    pos = np.stack([t // GRID_W, t % GRID_W], axis=1).astype(np.float64)
    inv = ROPE_BASE ** (-np.arange(nf, dtype=np.float64) / nf)
    dim = np.arange(HEAD_DIM)
    ang = pos[:, dim // half] * inv[dim % nf][None, :]
    sign = np.where((dim % half) < nf, -1.0, 1.0)
    return jnp.asarray(np.cos(ang), _f32), jnp.asarray(np.sin(ang) * sign, _f32)


def _rope(x, cos, sin_signed):
    nf = HEAD_DIM // 4
    lane = lax.broadcasted_iota(jnp.int32, x.shape, 1)
    partner = jnp.where((lane % (2 * nf)) < nf, pltpu.roll(x, HEAD_DIM - nf, 1), pltpu.roll(x, nf, 1))
    return x * cos + partner * sin_signed


def _win_attn_kernel(q_ref, kp_ref, kc_ref, kn_ref, vp_ref, vc_ref, vn_ref, cq_ref, sq_ref, cp_ref, sp_ref,
                     cn_ref, sn_ref, ck_ref, cv_ref, qg_ref, kg_ref, sink_ref, o_ref):
    n = pl.program_id(1)
    nb = pl.num_programs(1)
    kg = kg_ref[...]
    k_all = jnp.concatenate([
        _rope(_head_norm(kp_ref[...], kg), cp_ref[...], sp_ref[...]),
        _rope(_head_norm(kc_ref[...], kg), cq_ref[...], sq_ref[...]),
        _rope(_head_norm(kn_ref[...], kg), cn_ref[...], sn_ref[...]),
        ck_ref[...]], axis=0).astype(_bf16)
    v_all = jnp.concatenate([vp_ref[...], vc_ref[...], vn_ref[...], cv_ref[...]], axis=0).astype(_bf16)
    r = lax.broadcasted_iota(jnp.int32, (ATTN_BLOCK, ATTN_BLOCK), 0)
    c = lax.broadcasted_iota(jnp.int32, (ATTN_BLOCK, ATTN_BLOCK), 1)
    neg = jnp.full((ATTN_BLOCK, ATTN_BLOCK), -jnp.inf, _f32)
    zero = jnp.zeros((ATTN_BLOCK, ATTN_BLOCK), _f32)
    bias = jnp.concatenate([
        jnp.where(n > 0, jnp.where(c >= r, zero, neg), neg),
        zero,
        jnp.where(n < nb - 1, jnp.where(c <= r, zero, neg), neg),
        jnp.zeros((ATTN_BLOCK, ck_ref.shape[0]), _f32)], axis=1)
    for g in range(Q_GROUP):
        cols = slice(g * HEAD_DIM, (g + 1) * HEAD_DIM)
        qn = _rope(_head_norm(q_ref[:, cols], qg_ref[...]), cq_ref[...], sq_ref[...])
        s = _dot_nt(qn.astype(_bf16), k_all) * HEAD_DIM ** -0.5 + bias
        p = _softmax_with_sink(s, sink_ref[0:1, g:g + 1])
        o_ref[:, cols] = jnp.dot(p.astype(_bf16), v_all, preferred_element_type=_f32).astype(_bf16)


def _win_attention(z, cache_k, cache_v, q_g, k_g, sink, layer):
    cos, sin = _rope_tables()
    nb = DEC_SEQ // ATTN_BLOCK
    base = N_CTX // ATTN_BLOCK
    prev = lambda n: jnp.maximum(n - 1, 0)
    nxt = lambda n: jnp.minimum(n + 1, nb - 1)
    past = cache_k.shape[3]

    def zrow(sel, col0):
        return pl.BlockSpec((ATTN_BLOCK, HEAD_DIM), lambda b, n, j: (base + b * nb + sel(n), col0 // HEAD_DIM + j))

    def tab(sel):
        return pl.BlockSpec((ATTN_BLOCK, HEAD_DIM), lambda b, n, j: (sel(n), 0))

    same = lambda n: n
    lay = lambda b, n, j: (layer, 0, 0)
    cache = pl.BlockSpec((None, None, None, past, HEAD_DIM), lambda b, n, j: (b, layer, j, 0, 0))
    return pl.pallas_call(
        _win_attn_kernel,
        grid=(DEC_BATCH, nb, N_KV_HEADS),
        in_specs=[
            pl.BlockSpec((ATTN_BLOCK, KV_GROUP_W), lambda b, n, j: (base + b * nb + n, COL_Q // KV_GROUP_W + j)),
            zrow(prev, COL_K), zrow(same, COL_K), zrow(nxt, COL_K),
            zrow(prev, COL_V), zrow(same, COL_V), zrow(nxt, COL_V),
            tab(same), tab(same), tab(prev), tab(prev), tab(nxt), tab(nxt),
            cache, cache,
            pl.BlockSpec((None, 1, HEAD_DIM), lay),
            pl.BlockSpec((None, 1, HEAD_DIM), lay),
            pl.BlockSpec((None, None, 1, Q_GROUP), lambda b, n, j: (layer, j, 0, 0)),
        ],
        out_specs=pl.BlockSpec((ATTN_BLOCK, KV_GROUP_W), lambda b, n, j: (b * nb + n, j)),
        out_shape=jax.ShapeDtypeStruct((N_LAT, ATT_Q_W), _bf16),
        compiler_params=pltpu.CompilerParams(dimension_semantics=("arbitrary", "arbitrary", "arbitrary"),
                                             vmem_limit_bytes=VMEM_LIMIT),
        name="win_attention",
    )(z, z, z, z, z, z, z, cos, sin, cos, sin, cos, sin, cache_k, cache_v,
      q_g.reshape(DEPTH, 1, HEAD_DIM), k_g.reshape(DEPTH, 1, HEAD_DIM), sink.reshape(DEPTH, N_KV_HEADS, 1, Q_GROUP))


HG_BLOCK = 128
HG_LEVELS = 7
HG_NSEG = HG_LEVELS + 2
HG_HEADS_PER_STEP = 4


def _hgrn_constants():
    c = HG_BLOCK
    i = np.arange(c)
    seg = np.zeros((HG_NSEG, c, c), np.float32)
    mask = np.zeros((HG_LEVELS + 1, c, c), np.float32)
    mask[0] = np.eye(c)
    for lv in range(HG_LEVELS):
        b = 2 << lv
        h = b // 2
        mid = (i // b) * b + h
        second = (i % b) >= h
        for r in range(c):
            if second[r]:
                seg[lv, r, mid[r]:r + 1] = 1.0
            else:
                seg[lv, r, r + 1:mid[r]] = 1.0
        same = (i[:, None] // b) == (i[None, :] // b)
        mask[lv + 1] = same & second[:, None] & ~second[None, :]
    seg[HG_LEVELS] = np.tril(np.ones((c, c)))
    seg[HG_LEVELS + 1] = np.triu(np.ones((c, c)), 1)
    seg2 = np.stack([seg, seg[:, ::-1, ::-1]]).reshape(2, HG_NSEG * c, c)
    mask2 = np.stack([mask, mask[:, ::-1, ::-1]])
    return jnp.asarray(seg2, _bf16), jnp.asarray(mask2, _f32)


def _hgrn_schedule():
    rows, first, last, seq = [[], []], [], [], []
    for s in range(BATCH + DEC_BATCH):
        n = (SEQ if s < BATCH else DEC_SEQ) // HG_BLOCK
        base = s * (SEQ // HG_BLOCK) if s < BATCH else N_CTX // HG_BLOCK + (s - BATCH) * (DEC_SEQ // HG_BLOCK)
        for cpos in range(n):
            rows[0].append(base + cpos)
            rows[1].append(base + n - 1 - cpos)
            first.append(int(cpos == 0))
            last.append(int(cpos == n - 1))
            seq.append(s)
    as_i32 = lambda a: jnp.asarray(np.asarray(a, np.int32).reshape(-1))
    return as_i32(rows), as_i32(first), as_i32(last), as_i32(seq), len(seq)


def _hgrn_kernel(rows_ref, first_ref, last_ref, seq_ref, q_ref, f_ref, i_ref, lb_ref, seg_ref, mask_ref, s0_ref,
                 o_ref, sf_ref, st_s):
    job = pl.program_id(2)
    seg = seg_ref[...]

    @pl.when(first_ref[job] == 1)
    def _():
        for hh in range(HG_HEADS_PER_STEP):
            st_s[hh] = s0_ref[hh].T

    for hh in range(HG_HEADS_PER_STEP):
        cols = slice(hh * HG_DK, (hh + 1) * HG_DK)
        xq = q_ref[:, cols]
        q = xq * jax.nn.sigmoid(xq) * HG_DK ** -0.5
        xf = f_ref[:, cols]
        log_sig = jnp.minimum(xf, 0.0) - jnp.log(1.0 + jnp.exp(-jnp.abs(xf)))
        lb = lb_ref[hh]
        f = lb + (1.0 - lb) * jnp.exp(log_sig)
        log_f = jnp.where(lb > 0.0, jnp.log(f), log_sig)
        k = 1.0 - f
        v = i_ref[:, cols]
        vb = v.astype(_bf16)
        lf_hi, lf_lo = _split_bf16(log_f)
        decay = jnp.exp(jnp.dot(seg, lf_hi, preferred_element_type=_f32)
                        + jnp.dot(seg, lf_lo, preferred_element_type=_f32))
        scores = _dot_nt(q.astype(_bf16), k.astype(_bf16)) * mask_ref[0]
        for lv in range(HG_LEVELS):
            e = decay[lv * HG_BLOCK:(lv + 1) * HG_BLOCK]
            scores = scores + _dot_nt((q * e).astype(_bf16), (k * e).astype(_bf16)) * mask_ref[lv + 1]
        e_q = decay[HG_LEVELS * HG_BLOCK:(HG_LEVELS + 1) * HG_BLOCK]
        e_k = decay[(HG_LEVELS + 1) * HG_BLOCK:(HG_LEVELS + 2) * HG_BLOCK]
        st = st_s[hh]
        o_ref[:, cols] = (jnp.dot(scores.astype(_bf16), vb, preferred_element_type=_f32)
                          + _dot_nt((q * e_q).astype(_bf16), st.astype(_bf16)))
        total = jnp.sum(log_f, axis=0, keepdims=True)
        st_new = jnp.exp(total) * st + jnp.dot(v.T.astype(_bf16), (k * e_k).astype(_bf16),
                                               preferred_element_type=_f32)
        st_s[hh] = st_new

    @pl.when(last_ref[job] == 1)
    def _():
        for hh in range(HG_HEADS_PER_STEP):
            sf_ref[hh] = st_s[hh].T


def _hgrn(z, lb, s0):
    seg, mask = _hgrn_constants()
    rows, first, last, seq, n_jobs = _hgrn_schedule()

    hps = HG_HEADS_PER_STEP
    step_w = hps * HG_DK

    def zcol(col0):
        return pl.BlockSpec((HG_BLOCK, step_w),
                            lambda h, d, j, r, f, l, s: (r[d * n_jobs + j], col0 // step_w + h))

    state = pl.BlockSpec((None, None, hps, HG_DK, HG_DV), lambda h, d, j, r, f, l, s: (s[j], d, h, 0, 0))
    grid_spec = pltpu.PrefetchScalarGridSpec(
        num_scalar_prefetch=4,
        grid=(HG_HEADS // hps, 2, n_jobs),
        in_specs=[
            zcol(COL_HQ),
            pl.BlockSpec((HG_BLOCK, step_w),
                         lambda h, d, j, r, f, l, s: (r[d * n_jobs + j], (COL_HFF + d * HG_W) // step_w + h)),
            zcol(COL_HI),
            pl.BlockSpec((None, hps, 1, HG_DK), lambda h, d, j, r, f, l, s: (d, h, 0, 0)),
            pl.BlockSpec((None, HG_NSEG * HG_BLOCK, HG_BLOCK), lambda h, d, j, r, f, l, s: (d, 0, 0)),
            pl.BlockSpec((None, HG_LEVELS + 1, HG_BLOCK, HG_BLOCK), lambda h, d, j, r, f, l, s: (d, 0, 0, 0)),
            state,
        ],
        out_specs=[
            pl.BlockSpec((None, HG_BLOCK, step_w), lambda h, d, j, r, f, l, s: (d, r[d * n_jobs + j], h)),
            state,
        ],
        scratch_shapes=[pltpu.VMEM((hps, HG_DV, HG_DK), _f32)],
    )
    return pl.pallas_call(
        _hgrn_kernel,
        grid_spec=grid_spec,
        out_shape=[
            jax.ShapeDtypeStruct((2, N_TOK, HG_W), _f32),
            jax.ShapeDtypeStruct(s0.shape, _f32),
        ],
        compiler_params=pltpu.CompilerParams(dimension_semantics=("arbitrary", "arbitrary", "arbitrary"),
                                             vmem_limit_bytes=VMEM_LIMIT),
        name="hgrn",
    )(rows, first, last, seq, z, z, z, lb.reshape(2, HG_HEADS, 1, HG_DK), seg, mask, s0)


def _hgrn_out_kernel(of_ref, ob_ref, g_ref, ng_ref, o_ref):
    ng = ng_ref[...]
    for hh in range(HG_OUT_W // HG_DV):
        cols = slice(hh * HG_DV, (hh + 1) * HG_DV)
        o = _head_norm(of_ref[:, cols] + ob_ref[:, cols], ng)
        xg = g_ref[:, cols]
        o_ref[:, cols] = (o * (xg * jax.nn.sigmoid(xg))).astype(_bf16)


def _hgrn_out(o_dirs, z, norm_g, layer):
    return pl.pallas_call(
        _hgrn_out_kernel,
        grid=(N_TOK // ROW_TILE, HG_W // HG_OUT_W),
        in_specs=[
            pl.BlockSpec((None, ROW_TILE, HG_OUT_W), lambda i, j: (0, i, j)),
            pl.BlockSpec((None, ROW_TILE, HG_OUT_W), lambda i, j: (1, i, j)),
            pl.BlockSpec((ROW_TILE, HG_OUT_W), lambda i, j: (i, COL_HG // HG_OUT_W + j)),
            pl.BlockSpec((None, 1, HG_DV), lambda i, j: (layer, 0, 0)),
        ],
        out_specs=pl.BlockSpec((ROW_TILE, HG_OUT_W), lambda i, j: (i, j)),
        out_shape=jax.ShapeDtypeStruct((N_TOK, HG_W), _bf16),
        compiler_params=pltpu.CompilerParams(dimension_semantics=("arbitrary", "arbitrary"),
                                             vmem_limit_bytes=VMEM_LIMIT),
        name="hgrn_out",
    )(o_dirs, o_dirs, z, norm_g.reshape(DEPTH, 1, HG_DV))


def _s5_param_kernel(are_ref, aim_ref, ldt_ref, bre_ref, bim_ref, abre_ref, abim_ref, bpre_ref, bpim_ref):
    bre = bre_ref[...]
    bim = bim_ref[...]
    for d in range(2):
        lr = are_ref[d]
        li = aim_ref[d]
        dt = jnp.exp(ldt_ref[d])
        mag = jnp.exp(lr * dt)
        ar = mag * jnp.cos(li * dt)
        ai = mag * jnp.sin(li * dt)
        abre_ref[d] = ar
        abim_ref[d] = ai
        nr = ar - 1.0
        den = lr * lr + li * li
        gr = (nr * lr + ai * li) / den
        gi = (ai * lr - nr * li) / den
        gr3 = gr[:, None, :]
        gi3 = gi[:, None, :]
        bpre_ref[d] = gr3 * bre - gi3 * bim
        bpim_ref[d] = gr3 * bim + gi3 * bre


def _s5_params(a_re, a_im, log_dt, b_re, b_im, c_re, c_im):
    g, pst, ch = S5_GROUPS, S5_STATE, S5_CH
    abre, abim, bpre, bpim = pl.pallas_call(
        _s5_param_kernel,
        out_shape=[
            jax.ShapeDtypeStruct((2, g, pst), _f32),
            jax.ShapeDtypeStruct((2, g, pst), _f32),
            jax.ShapeDtypeStruct((2, g, ch, pst), _f32),
            jax.ShapeDtypeStruct((2, g, ch, pst), _f32),
        ],
        name="s5_params",
    )(a_re, a_im, log_dt.reshape(2, g, 1), b_re.transpose(0, 2, 1), b_im.transpose(0, 2, 1))
    eye = jnp.eye(S5_SLAB_GROUPS, dtype=_f32)

    def b_slabs(bp):
        bp = bp.reshape(2, S5_NSLAB, S5_SLAB_GROUPS, ch, pst)
        w = bp[:, :, :, :, None, :] * eye[None, None, :, None, :, None]
        return w.reshape(2, S5_NSLAB, S5_SLAB_CH, S5_SLAB_ST)

    def c_slabs(cm):
        cm = cm.reshape(S5_NSLAB, S5_SLAB_GROUPS, ch, pst).transpose(0, 1, 3, 2)
        w = cm[:, :, :, None, :] * eye[None, :, None, :, None]
        return w.reshape(S5_NSLAB, S5_SLAB_ST, S5_SLAB_CH)

    b_blk = jnp.concatenate([b_slabs(bpre), b_slabs(bpim)], axis=-1).astype(_bf16)
    return (abre.reshape(2, 1, g * pst), abim.reshape(2, 1, g * pst), b_blk,
            c_slabs(c_re).astype(_bf16), c_slabs(c_im).astype(_bf16))


def _s5_scan_kernel(u_ref, b_ref, cre_ref, cim_ref, are_ref, aim_ref, x0re_ref, x0im_ref,
                    y_ref, xfre_ref, xfim_ref, bu_s, zre_s, zim_s, *, tb_steps, rd):
    d = pl.program_id(0)
    tb = pl.program_id(2)
    n_rg = rd // SUBLANES
    st = S5_SLAB_ST

    @pl.when(tb == 0)
    def _():
        zre_s[...] = x0re_ref[...]
        zim_s[...] = x0im_ref[...]

    u = u_ref[...].reshape(tb_steps * rd, S5_SLAB_CH).astype(_bf16)
    bu_s[...] = jnp.dot(u, b_ref[...], preferred_element_type=_f32).reshape(tb_steps, rd, 2 * st)
    are = jnp.broadcast_to(are_ref[...], (SUBLANES, st))
    aim = jnp.broadcast_to(aim_ref[...], (SUBLANES, st))

    def body(i, carry):
        t = jnp.where(d == 0, i, tb_steps - 1 - i)
        new = []
        for rg in range(n_rg):
            zre, zim = carry[rg]
            rows = slice(rg * SUBLANES, (rg + 1) * SUBLANES)
            nre = are * zre - aim * zim + bu_s[t, rows, 0:st]
            nim = are * zim + aim * zre + bu_s[t, rows, st:2 * st]
            bu_s[t, rows, 0:st] = nre
            bu_s[t, rows, st:2 * st] = nim
            new.append((nre, nim))
        return tuple(new)

    init = tuple((zre_s[rg * SUBLANES:(rg + 1) * SUBLANES, :], zim_s[rg * SUBLANES:(rg + 1) * SUBLANES, :])
                 for rg in range(n_rg))
    fin = lax.fori_loop(0, tb_steps, body, init, unroll=4)
    for rg in range(n_rg):
        zre_s[rg * SUBLANES:(rg + 1) * SUBLANES, :] = fin[rg][0]
        zim_s[rg * SUBLANES:(rg + 1) * SUBLANES, :] = fin[rg][1]

    xs = bu_s[...].reshape(tb_steps * rd, 2 * st)
    y = (jnp.dot(xs[:, 0:st].astype(_bf16), cre_ref[...], preferred_element_type=_f32)
         - jnp.dot(xs[:, st:2 * st].astype(_bf16), cim_ref[...], preferred_element_type=_f32))
    y_ref[...] = y.reshape(tb_steps, rd, S5_SLAB_CH)

    @pl.when(tb == pl.num_programs(2) - 1)
    def _():
        xfre_ref[...] = zre_s[...]
        xfim_ref[...] = zim_s[...]


def _s5_scan(u, params, x0_re, x0_im):
    abre, abim, b_blk, c_re, c_im = params
    t_len, rd, _ = u.shape
    tb_steps = S5_BLOCK_ROWS // rd
    n_tb = t_len // tb_steps
    st = S5_SLAB_ST
    grid = (2, S5_NSLAB, n_tb)
    kern = functools.partial(_s5_scan_kernel, tb_steps=tb_steps, rd=rd)
    tblock = lambda d, t: jnp.where(d == 0, t, n_tb - 1 - t)
    return pl.pallas_call(
        kern,
        grid=grid,
        in_specs=[
            pl.BlockSpec((tb_steps, rd, S5_SLAB_CH), lambda d, s, t: (tblock(d, t), 0, s)),
            pl.BlockSpec((None, None, S5_SLAB_CH, 2 * st), lambda d, s, t: (d, s, 0, 0)),
            pl.BlockSpec((None, st, S5_SLAB_CH), lambda d, s, t: (s, 0, 0)),
            pl.BlockSpec((None, st, S5_SLAB_CH), lambda d, s, t: (s, 0, 0)),
            pl.BlockSpec((None, 1, st), lambda d, s, t: (d, 0, s)),
            pl.BlockSpec((None, 1, st), lambda d, s, t: (d, 0, s)),
            pl.BlockSpec((None, rd, st), lambda d, s, t: (d, 0, s)),
            pl.BlockSpec((None, rd, st), lambda d, s, t: (d, 0, s)),
        ],
        out_specs=[
            pl.BlockSpec((None, tb_steps, rd, S5_SLAB_CH), lambda d, s, t: (d, tblock(d, t), 0, s)),
            pl.BlockSpec((None, rd, st), lambda d, s, t: (d, 0, s)),
            pl.BlockSpec((None, rd, st), lambda d, s, t: (d, 0, s)),
        ],
        out_shape=[
            jax.ShapeDtypeStruct((2, t_len, rd, S5_W), _f32),
            jax.ShapeDtypeStruct((2, rd, S5_GROUPS * S5_STATE), _f32),
            jax.ShapeDtypeStruct((2, rd, S5_GROUPS * S5_STATE), _f32),
        ],
        scratch_shapes=[
            pltpu.VMEM((tb_steps, rd, 2 * st), _f32),
            pltpu.VMEM((rd, st), _f32),
            pltpu.VMEM((rd, st), _f32),
        ],
        compiler_params=pltpu.CompilerParams(dimension_semantics=("arbitrary", "arbitrary", "arbitrary"),
                                             vmem_limit_bytes=VMEM_LIMIT),
        name="s5_scan",
    )(u, b_blk, c_re, c_im, abre, abim, x0_re, x0_im)


def _gelu_tanh(x):
    return 0.5 * x * (1.0 + jnp.tanh(math.sqrt(2.0 / math.pi) * (x + 0.044715 * (x * x * x))))


def _s5_out_kernel(yf_ref, yb_ref, u_ref, d_ref, w_ref, b_ref, o_ref, w_s):
    @pl.when(pl.program_id(0) == 0)
    def _():
        w_s[...] = w_ref[...].astype(_bf16)

    y = _gelu_tanh(yf_ref[...] + yb_ref[...] + d_ref[...] * u_ref[...])
    z = jnp.dot(y.astype(_bf16), w_s[...], preferred_element_type=_f32) + b_ref[...]
    o_ref[...] = (y * jax.nn.sigmoid(z)).astype(_bf16)


def _s5_out(yf, yb, u, d_vec, w_glu, b_glu, layer):
    nt = N_TOK // ROW_TILE
    row = lambda i: (i, 0)
    lay = lambda i: (layer, 0, 0)
    return pl.pallas_call(
        _s5_out_kernel,
        grid=(nt,),
        in_specs=[
            pl.BlockSpec((ROW_TILE, S5_W), row),
            pl.BlockSpec((ROW_TILE, S5_W), row),
            pl.BlockSpec((ROW_TILE, S5_W), row),
            pl.BlockSpec((None, 1, S5_W), lay),
            pl.BlockSpec((None, S5_W, S5_W), lay),
            pl.BlockSpec((None, 1, S5_W), lay),
        ],
        out_specs=pl.BlockSpec((ROW_TILE, S5_W), row),
        out_shape=jax.ShapeDtypeStruct((N_TOK, S5_W), _bf16),
        scratch_shapes=[pltpu.VMEM((S5_W, S5_W), _bf16)],
        compiler_params=pltpu.CompilerParams(dimension_semantics=("arbitrary",), vmem_limit_bytes=VMEM_LIMIT),
        name="s5_out",
    )(yf, yb, u, d_vec.reshape(DEPTH, 1, S5_W), w_glu, b_glu.reshape(DEPTH, 1, S5_W))


def _s5_branch(su_c, su_l, sp, s5_re0, s5_im0, layer):
    params = _s5_params(sp['s5_a_re'][layer], sp['s5_a_im'][layer], sp['s5_log_dt'][layer], sp['s5_b_re'][layer],
                        sp['s5_b_im'][layer], sp['s5_c_re'][layer], sp['s5_c_im'][layer])
    nst = S5_GROUPS * S5_STATE

    def split_dirs(y):
        return y[0].transpose(1, 0, 2), y[1].transpose(1, 0, 2)

    zeros = jnp.zeros((2, BATCH, nst), _f32)
    y_c, xf_re, xf_im = _s5_scan(su_c.transpose(1, 0, 2), params, zeros, zeros)
    yf_c, yb_c = split_dirs(y_c)
    pad = ((0, 0), (0, SUBLANES - DEC_BATCH), (0, 0))
    u_l = jnp.pad(su_l.transpose(1, 0, 2), pad)

    def lat_state(x0):
        return jnp.pad(x0.transpose(1, 0, 2, 3).reshape(2, DEC_BATCH, nst), pad)

    y_l, _, _ = _s5_scan(u_l, params, lat_state(s5_re0), lat_state(s5_im0))
    yf_l, yb_l = split_dirs(y_l[:, :, :DEC_BATCH])
    yf = jnp.concatenate([yf_c.reshape(N_CTX, S5_W), yf_l.reshape(N_LAT, S5_W)], axis=0)
    yb = jnp.concatenate([yb_c.reshape(N_CTX, S5_W), yb_l.reshape(N_LAT, S5_W)], axis=0)
    u = jnp.concatenate([su_c.reshape(N_CTX, S5_W), su_l.reshape(N_LAT, S5_W)], axis=0)
    s_out = _s5_out(yf, yb, u, sp['s5_d'], sp['s5_w_glu'], sp['s5_b_glu'], layer)

    def ctx_state(xf):
        return xf.reshape(2, BATCH, S5_GROUPS, S5_STATE).transpose(1, 0, 2, 3)

    return s_out, ctx_state(xf_re), ctx_state(xf_im)


def kernel(x_prompt, x_sample, cache_k, cache_v, state_hgrn, state_s5_re, state_s5_im, c, c_ctx, w_mod, b_mod, norm1_g, norm2_g, w_in, q_norm_g, k_norm_g, attn_sink, hg_lb_logits, hg_norm_g, s5_a_re, s5_a_im, s5_log_dt, s5_b_re, s5_b_im, s5_c_re, s5_c_im, s5_d, s5_w_glu, s5_b_glu, w_br_attn, w_br_hg, w_br_s5, w_out, router_w, router_b, exp_w1, exp_b1, exp_w2, exp_b2):
    lb_all = jnp.cumsum(jax.nn.softmax(hg_lb_logits.astype(_f32), axis=0), axis=0)
    lb_all = lb_all - lb_all[:1]
    cond_rows = jnp.concatenate([c_ctx[None, :], c, jnp.zeros((SUBLANES - N_GROUPS, D_MODEL), _f32)], axis=0)
    x = jnp.concatenate([x_prompt.reshape(N_CTX, D_MODEL), x_sample.reshape(N_LAT, D_MODEL)], axis=0)
    cache_k_t = cache_k.transpose(0, 1, 3, 2, 4)
    cache_v_t = cache_v.transpose(0, 1, 3, 2, 4)
    zero_state = jnp.zeros((BATCH, 2, HG_HEADS, HG_DK, HG_DV), _f32)
    new_k, new_v, new_hg, new_re, new_im = [], [], [], [], []
    moe_p = {"norm2_g": norm2_g, "router_w": router_w, "router_b": router_b, "exp_w1": exp_w1, "exp_b1": exp_b1,
             "exp_w2": exp_w2, "exp_b2": exp_b2}
    s5_p = {'s5_a_re': s5_a_re, 's5_a_im': s5_a_im, 's5_log_dt': s5_log_dt, 's5_b_re': s5_b_re, 's5_b_im': s5_b_im,
            's5_c_re': s5_c_re, 's5_c_im': s5_c_im, 's5_d': s5_d, 's5_w_glu': s5_w_glu, 's5_b_glu': s5_b_glu}
    for l in range(DEPTH):
        m = _modulation(cond_rows, w_mod, b_mod, l)[:N_GROUPS]
        mods = [a[:, None, :] for a in jnp.split(m, N_MOD, axis=-1)]
        sh1, sc1, gt1 = mods[0], mods[1], mods[2]
        z = _in_proj(_norm_mod(x, norm1_g, sc1, sh1, l), w_in, l)
        a_ctx, k_l, v_l = _ctx_attention(z, q_norm_g, k_norm_g, attn_sink, l)
        a_lat = _win_attention(z, cache_k_t, cache_v_t, q_norm_g, k_norm_g, attn_sink, l)
        a_out = jnp.concatenate([a_ctx, a_lat], axis=0)
        o_dirs, hg_fin = _hgrn(z, lb_all[l], jnp.concatenate([zero_state, state_hgrn[:, l]], axis=0))
        h_out = _hgrn_out(o_dirs, z, hg_norm_g, l)
        su = z[:, COL_SU:COL_SU + S5_W]
        s_out, re_l, im_l = _s5_branch(su[:N_CTX].reshape(BATCH, SEQ, S5_W),
                                       su[N_CTX:].reshape(DEC_BATCH, DEC_SEQ, S5_W), s5_p,
                                       state_s5_re[:, l], state_s5_im[:, l], l)
        y = _merge(a_out, h_out, s_out, z, w_br_attn, w_br_hg, w_br_s5, l)
        x = _out_proj(y, w_out, x, gt1, l)
        new_k.append(k_l.reshape(BATCH, SEQ, N_KV_HEADS, HEAD_DIM))
        new_v.append(v_l.reshape(BATCH, SEQ, N_KV_HEADS, HEAD_DIM))
        new_hg.append(hg_fin[:BATCH])
        new_re.append(re_l)
        new_im.append(im_l)
        x = _moe_sublayer(x, mods, moe_p, l)
    return (x[:N_CTX].reshape(BATCH, SEQ, D_MODEL), x[N_CTX:].reshape(DEC_BATCH, DEC_SEQ, D_MODEL),
            jnp.stack(new_k, axis=1), jnp.stack(new_v, axis=1), jnp.stack(new_hg, axis=1),
            jnp.stack(new_re, axis=1), jnp.stack(new_im, axis=1))
```
